```python
import math
import jax, jax.numpy as jnp
from jax import lax
import numpy as np

D_MODEL = 2048
BATCH = 8
SEQ = 4096
DEPTH = 4

CHUNK = 64
Q_BLOCK = 128
HEAD_DIM = 128
FOX_HEADS = 8
FOX_WIDTH = FOX_HEADS * HEAD_DIM
SGU_GROUPS = 8
SGU_WIDTH = SGU_GROUPS * HEAD_DIM
SGU_GROUP_DIM = SGU_WIDTH // SGU_GROUPS
SGU_SPAN = 128
GDN_HEADS = 8
GDN_WIDTH = GDN_HEADS * HEAD_DIM
GDN_CONV = 4
N_BRANCH = 3
D_FF = 5504
FFN_CONV = 3
DEEPNORM_ALPHA = (2 * DEPTH) ** 0.25
DEEPNORM_BETA = (8 * DEPTH) ** -0.25
LN_EPS = 1e-5
RMS_EPS = 1e-6

IN_SIZES = (3 * FOX_WIDTH,
            FOX_HEADS,
            2 * SGU_WIDTH,
            3 * GDN_WIDTH,
            GDN_HEADS,
            GDN_HEADS,
            GDN_WIDTH,
            N_BRANCH * D_MODEL)
IN_OFFSETS = tuple(sum(IN_SIZES[:i]) for i in range(1, len(IN_SIZES)))
N_IN = sum(IN_SIZES)
FOX_F_OFFSET = 3 * FOX_WIDTH

kernel_name = "hybrid_fox_sgu_gdn_convffn_block"


def layer_norm(x, g, b):
    xf = x.astype(jnp.float32)
    mu = jnp.mean(xf, axis=-1, keepdims=True)
    var = jnp.mean(jnp.square(xf - mu), axis=-1, keepdims=True)
    return ((xf - mu) * lax.rsqrt(var + LN_EPS) * g + b).astype(x.dtype)


def causal_depthwise_conv(x, w):
    k = w.shape[0]
    c = x.shape[-1]
    return lax.conv_general_dilated(
        x, w[:, None, :].astype(x.dtype), window_strides=(1,),
        padding=[(k - 1, 0)], dimension_numbers=('NWC', 'WIO', 'NWC'),
        feature_group_count=c)


def forgetting_attention(q, k, v, log_f):
    b, s, h, dh = q.shape
    nb = s // Q_BLOCK
    c = jnp.cumsum(log_f, axis=1).transpose(0, 2, 1)
    qb = q.reshape(b, nb, Q_BLOCK, h, dh).transpose(1, 0, 3, 2, 4)
    cqb = c.reshape(b, h, nb, Q_BLOCK).transpose(2, 0, 1, 3)
    kpos = jnp.arange(s)
    scale = dh ** -0.5

    def block(args):
        qi, cqi, i = args
        logits = jnp.einsum('bhqd,bkhd->bhqk', qi, k).astype(jnp.float32) * scale
        logits = logits + cqi[..., None] - c[:, :, None, :]
        qpos = i * Q_BLOCK + jnp.arange(Q_BLOCK)
        mask = qpos[:, None] >= kpos[None, :]
        p = jax.nn.softmax(jnp.where(mask, logits, -jnp.inf), axis=-1)
        return jnp.einsum('bhqk,bkhd->bqhd', p.astype(v.dtype), v)

    out = lax.map(block, (qb, cqb, jnp.arange(nb)))
    return out.transpose(1, 0, 2, 3, 4).reshape(b, s, h * dh)


def spatial_gating(u, v, ln_g, ln_b, w_s, b_s):
    b, s, w = v.shape
    n = s // SGU_SPAN
    shape5 = (b, n, SGU_SPAN, SGU_GROUPS, SGU_GROUP_DIM)
    vg = layer_norm(v.reshape(shape5), ln_g.reshape(SGU_GROUPS, SGU_GROUP_DIM),
                    ln_b.reshape(SGU_GROUPS, SGU_GROUP_DIM))
    pos = jnp.arange(SGU_SPAN) // CHUNK
    mask = pos[:, None] >= pos[None, :]
    mixed = jnp.einsum('gts,bnsgc->bntgc', jnp.where(mask, w_s, 0.0).astype(vg.dtype), vg)
    mixed = mixed + b_s.T[:, :, None]
    return (u.reshape(shape5) * mixed).reshape(b, s, w)


def l2_normalize(t):
    return t * lax.rsqrt(jnp.sum(jnp.square(t), axis=-1, keepdims=True) + RMS_EPS)


def gated_delta_rule(q, k, v, g, beta):
    b, s, h, dh = q.shape
    n = s // CHUNK
    to_c = lambda t: t.transpose(0, 2, 1, 3).reshape(b, h, n, CHUNK, t.shape[-1])
    q, k, v = to_c(q) * dh ** -0.5, to_c(k), to_c(v)
    g = g.transpose(0, 2, 1).reshape(b, h, n, CHUNK)
    beta = beta.transpose(0, 2, 1).reshape(b, h, n, CHUNK)
    gc = jnp.cumsum(g, axis=-1)
    causal = jnp.tril(jnp.ones((CHUNK, CHUNK), bool))
    strict = jnp.tril(jnp.ones((CHUNK, CHUNK), bool), -1)
    decay = jnp.exp(jnp.where(causal, gc[..., :, None] - gc[..., None, :], -jnp.inf))
    k_beta = k * beta[..., None]
    a_kk = jnp.where(strict, jnp.einsum('bhnid,bhnjd->bhnij', k_beta, k) * decay, 0.0)
    eye = jnp.eye(CHUNK, dtype=q.dtype)
    rhs = jnp.concatenate([v * beta[..., None], k_beta * jnp.exp(gc)[..., None]], axis=-1)
    sol = lax.linalg.triangular_solve(eye + a_kk, rhs, left_side=True, lower=True)
    u, w = sol[..., :dh], sol[..., dh:]
    qk = jnp.where(causal, jnp.einsum('bhnid,bhnjd->bhnij', q, k) * decay, 0.0)
    g_last = gc[..., -1]
    k_dec = k * jnp.exp(g_last[..., None] - gc)[..., None]
    q_dec = q * jnp.exp(gc)[..., None]

    def step(state, xs):
        qd, kd, qk_i, u_i, w_i, gl = xs
        v_new = u_i - jnp.einsum('bhcd,bhde->bhce', w_i, state)
        o = jnp.einsum('bhcd,bhde->bhce', qd, state) + jnp.einsum('bhij,bhje->bhie', qk_i, v_new)
        state = state * jnp.exp(gl)[..., None, None] + jnp.einsum('bhcd,bhce->bhde', kd, v_new)
        return state, o

    xs = tuple(jnp.moveaxis(t, 2, 0) for t in (q_dec, k_dec, qk, u, w, g_last))
    state0 = jnp.zeros((b, h, dh, dh), jnp.float32)
    _, o = lax.scan(step, state0, xs)
    return jnp.moveaxis(o, 0, 2).reshape(b, h, s, dh).transpose(0, 2, 1, 3)


def gdn_mixer(qkv, a, beta_logit, gate, conv_w, a_log, dt_bias, norm_g):
    b, s, _ = qkv.shape
    qkv = jax.nn.silu(causal_depthwise_conv(qkv, conv_w)).astype(jnp.float32)
    q, k, v = [t.reshape(b, s, GDN_HEADS, HEAD_DIM) for t in jnp.split(qkv, 3, axis=-1)]
    q, k = l2_normalize(q), l2_normalize(k)
    g = -jnp.exp(a_log.astype(jnp.float32)) * jax.nn.softplus((a + dt_bias).astype(jnp.float32))
    beta = jax.nn.sigmoid(beta_logit.astype(jnp.float32))
    o = gated_delta_rule(q, k, v, g, beta)
    o = o * lax.rsqrt(jnp.mean(jnp.square(o), axis=-1, keepdims=True) + RMS_EPS) * norm_g
    o = o * jax.nn.silu(gate.reshape(b, s, GDN_HEADS, HEAD_DIM).astype(jnp.float32))
    return o.astype(gate.dtype).reshape(b, s, GDN_WIDTH)


def token_mixing(x, w_in, b_in, sgu_ln_g, sgu_ln_b, sgu_w, sgu_b, gdn_conv_w,
                 gdn_a_log, gdn_dt_bias, gdn_norm_g, w_proj_a, w_proj_b, w_proj_c, w_out):
    bsz, s, _ = x.shape
    proj = x @ w_in + b_in
    fox_qkv, fox_f, sgu_uv, gdn_qkv, gdn_a, gdn_b, gdn_gate, gates = jnp.split(
        proj, list(IN_OFFSETS), axis=-1)
    fq, fk, fv = [t.reshape(bsz, s, FOX_HEADS, HEAD_DIM) for t in jnp.split(fox_qkv, 3, axis=-1)]
    y_a = forgetting_attention(fq, fk, fv, jax.nn.log_sigmoid(fox_f.astype(jnp.float32)))
    su, sv = jnp.split(sgu_uv, 2, axis=-1)
    y_b = spatial_gating(su, sv, sgu_ln_g, sgu_ln_b, sgu_w, sgu_b)
    y_c = gdn_mixer(gdn_qkv, gdn_a, gdn_b, gdn_gate, gdn_conv_w, gdn_a_log, gdn_dt_bias, gdn_norm_g)
    gt = jax.nn.sigmoid(gates).reshape(bsz, s, N_BRANCH, D_MODEL)
    merged = (gt[:, :, 0] * (y_a @ w_proj_a) + gt[:, :, 1] * (y_b @ w_proj_b)
              + gt[:, :, 2] * (y_c @ w_proj_c))
    return merged @ w_out


def conv_ffn(x, w_up, conv_w, conv_b, w_down):
    h = causal_depthwise_conv(x @ w_up, conv_w) + conv_b
    h_gate, h_val = jnp.split(h, 2, axis=-1)
    return (jax.nn.silu(h_gate) * h_val) @ w_down


def _fwd_setup_inputs(seed: int = 0) -> dict:
    key = jax.random.key(seed)
    ks = jax.random.split(key, 24)
    L, D = DEPTH, D_MODEL
    f32 = jnp.float32
    nrm = lambda k, shape, scale: jax.random.normal(k, shape, f32) * scale
    x = nrm(ks[0], (BATCH, SEQ, D), 1.0)
    w_in = nrm(ks[1], (L, D, N_IN), D ** -0.5)
    b_in = nrm(ks[2], (L, N_IN), 0.01)
    fox_fb = jax.random.uniform(ks[3], (L, FOX_HEADS), f32, 1.0, 5.0)
    b_in = b_in.at[:, FOX_F_OFFSET:FOX_F_OFFSET + FOX_HEADS].add(fox_fb)
    sgu_ln_g = 1.0 + nrm(ks[4], (L, SGU_WIDTH), 0.01)
    sgu_ln_b = nrm(ks[5], (L, SGU_WIDTH), 0.01)
    sgu_w = nrm(ks[6], (L, SGU_GROUPS, SGU_SPAN, SGU_SPAN), SGU_SPAN ** -0.5)
    sgu_b = 1.0 + nrm(ks[7], (L, SGU_GROUPS, SGU_SPAN), 0.01)
    gdn_conv_w = nrm(ks[8], (L, GDN_CONV, 3 * GDN_WIDTH), GDN_CONV ** -0.5)
    gdn_a_log = jnp.log(jax.random.uniform(ks[9], (L, GDN_HEADS), f32, 1.0, 16.0))
    dt = jnp.exp(jax.random.uniform(ks[10], (L, GDN_HEADS), f32, math.log(1e-3), math.log(1e-1)))
    gdn_dt_bias = dt + jnp.log(-jnp.expm1(-dt))
    gdn_norm_g = 1.0 + nrm(ks[11], (L, HEAD_DIM), 0.01)
    w_proj_a = nrm(ks[12], (L, FOX_WIDTH, D), FOX_WIDTH ** -0.5 * DEEPNORM_BETA)
    w_proj_b = nrm(ks[13], (L, SGU_WIDTH, D), SGU_WIDTH ** -0.5 * DEEPNORM_BETA)
    w_proj_c = nrm(ks[14], (L, GDN_WIDTH, D), GDN_WIDTH ** -0.5 * DEEPNORM_BETA)
    w_out = nrm(ks[15], (L, D, D), D ** -0.5 * DEEPNORM_BETA)
    ln1_g = 1.0 + nrm(ks[16], (L, D), 0.01)
    ln1_b = nrm(ks[17], (L, D), 0.01)
    ffn_w_up = nrm(ks[18], (L, D, 2 * D_FF), D ** -0.5)
    ffn_conv_w = nrm(ks[19], (L, FFN_CONV, 2 * D_FF), FFN_CONV ** -0.5)
    ffn_conv_b = nrm(ks[20], (L, 2 * D_FF), 0.01)
    ffn_w_down = nrm(ks[21], (L, D_FF, D), D_FF ** -0.5 * DEEPNORM_BETA)
    ln2_g = 1.0 + nrm(ks[22], (L, D), 0.01)
    ln2_b = nrm(ks[23], (L, D), 0.01)
    return {"x": x, "w_in": w_in, "b_in": b_in, "sgu_ln_g": sgu_ln_g, "sgu_ln_b": sgu_ln_b,
            "sgu_w": sgu_w, "sgu_b": sgu_b, "gdn_conv_w": gdn_conv_w, "gdn_a_log": gdn_a_log,
            "gdn_dt_bias": gdn_dt_bias, "gdn_norm_g": gdn_norm_g, "w_proj_a": w_proj_a,
            "w_proj_b": w_proj_b, "w_proj_c": w_proj_c, "w_out": w_out, "ln1_g": ln1_g,
            "ln1_b": ln1_b, "ffn_w_up": ffn_w_up, "ffn_conv_w": ffn_conv_w,
            "ffn_conv_b": ffn_conv_b, "ffn_w_down": ffn_w_down, "ln2_g": ln2_g, "ln2_b": ln2_b}


def _fwd_reference(x, w_in, b_in, sgu_ln_g, sgu_ln_b, sgu_w, sgu_b, gdn_conv_w, gdn_a_log,
              gdn_dt_bias, gdn_norm_g, w_proj_a, w_proj_b, w_proj_c, w_out, ln1_g, ln1_b,
              ffn_w_up, ffn_conv_w, ffn_conv_b, ffn_w_down, ln2_g, ln2_b):
    for l in range(DEPTH):
        mix = token_mixing(x, w_in[l], b_in[l], sgu_ln_g[l], sgu_ln_b[l], sgu_w[l], sgu_b[l],
                           gdn_conv_w[l], gdn_a_log[l], gdn_dt_bias[l], gdn_norm_g[l],
                           w_proj_a[l], w_proj_b[l], w_proj_c[l], w_out[l])
        x = layer_norm(DEEPNORM_ALPHA * x + mix, ln1_g[l], ln1_b[l])
        ffn = conv_ffn(x, ffn_w_up[l], ffn_conv_w[l], ffn_conv_b[l], ffn_w_down[l])
        x = layer_norm(DEEPNORM_ALPHA * x + ffn, ln2_g[l], ln2_b[l])
    return x


import jax as _jax
import jax.numpy as _jnp

TWIN_FORMAT = 'train_step'
FWD_PARAMS = ['x', 'w_in', 'b_in', 'sgu_ln_g', 'sgu_ln_b', 'sgu_w', 'sgu_b', 'gdn_conv_w', 'gdn_a_log', 'gdn_dt_bias', 'gdn_norm_g', 'w_proj_a', 'w_proj_b', 'w_proj_c', 'w_out', 'ln1_g', 'ln1_b', 'ffn_w_up', 'ffn_conv_w', 'ffn_conv_b', 'ffn_w_down', 'ln2_g', 'ln2_b']
TWIN_WEIGHTS = ['w_in', 'b_in', 'sgu_ln_g', 'sgu_ln_b', 'sgu_w', 'sgu_b', 'gdn_conv_w', 'gdn_a_log', 'gdn_dt_bias', 'gdn_norm_g', 'w_proj_a', 'w_proj_b', 'w_proj_c', 'w_out', 'ln1_g', 'ln1_b', 'ffn_w_up', 'ffn_conv_w', 'ffn_conv_b', 'ffn_w_down', 'ln2_g', 'ln2_b']
TWIN_DIFF_INPUT = 'x'
TWIN_INPUTS = ['x', 'w_in', 'b_in', 'sgu_ln_g', 'sgu_ln_b', 'sgu_w', 'sgu_b', 'gdn_conv_w', 'gdn_a_log', 'gdn_dt_bias', 'gdn_norm_g', 'w_proj_a', 'w_proj_b', 'w_proj_c', 'w_out', 'ln1_g', 'ln1_b', 'ffn_w_up', 'ffn_conv_w', 'ffn_conv_b', 'ffn_w_down', 'ln2_g', 'ln2_b', 'loss_target', 'm_w_in', 'm_b_in', 'm_sgu_ln_g', 'm_sgu_ln_b', 'm_sgu_w', 'm_sgu_b', 'm_gdn_conv_w', 'm_gdn_a_log', 'm_gdn_dt_bias', 'm_gdn_norm_g', 'm_w_proj_a', 'm_w_proj_b', 'm_w_proj_c', 'm_w_out', 'm_ln1_g', 'm_ln1_b', 'm_ffn_w_up', 'm_ffn_conv_w', 'm_ffn_conv_b', 'm_ffn_w_down', 'm_ln2_g', 'm_ln2_b', 'v_w_in', 'v_b_in', 'v_sgu_ln_g', 'v_sgu_ln_b', 'v_sgu_w', 'v_sgu_b', 'v_gdn_conv_w', 'v_gdn_a_log', 'v_gdn_dt_bias', 'v_gdn_norm_g', 'v_w_proj_a', 'v_w_proj_b', 'v_w_proj_c', 'v_w_out', 'v_ln1_g', 'v_ln1_b', 'v_ffn_w_up', 'v_ffn_conv_w', 'v_ffn_conv_b', 'v_ffn_w_down', 'v_ln2_g', 'v_ln2_b']
TWIN_OUTPUTS = ['loss', 'grad_x', 'grad_w_in', 'grad_b_in', 'grad_sgu_ln_g', 'grad_sgu_ln_b', 'grad_sgu_w', 'grad_sgu_b', 'grad_gdn_conv_w', 'grad_gdn_a_log', 'grad_gdn_dt_bias', 'grad_gdn_norm_g', 'grad_w_proj_a', 'grad_w_proj_b', 'grad_w_proj_c', 'grad_w_out', 'grad_ln1_g', 'grad_ln1_b', 'grad_ffn_w_up', 'grad_ffn_conv_w', 'grad_ffn_conv_b', 'grad_ffn_w_down', 'grad_ln2_g', 'grad_ln2_b', 'delta_w_in', 'delta_b_in', 'delta_sgu_ln_g', 'delta_sgu_ln_b', 'delta_sgu_w', 'delta_sgu_b', 'delta_gdn_conv_w', 'delta_gdn_a_log', 'delta_gdn_dt_bias', 'delta_gdn_norm_g', 'delta_w_proj_a', 'delta_w_proj_b', 'delta_w_proj_c', 'delta_w_out', 'delta_ln1_g', 'delta_ln1_b', 'delta_ffn_w_up', 'delta_ffn_conv_w', 'delta_ffn_conv_b', 'delta_ffn_w_down', 'delta_ln2_g', 'delta_ln2_b', 'new_m_w_in', 'new_m_b_in', 'new_m_sgu_ln_g', 'new_m_sgu_ln_b', 'new_m_sgu_w', 'new_m_sgu_b', 'new_m_gdn_conv_w', 'new_m_gdn_a_log', 'new_m_gdn_dt_bias', 'new_m_gdn_norm_g', 'new_m_w_proj_a', 'new_m_w_proj_b', 'new_m_w_proj_c', 'new_m_w_out', 'new_m_ln1_g', 'new_m_ln1_b', 'new_m_ffn_w_up', 'new_m_ffn_conv_w', 'new_m_ffn_conv_b', 'new_m_ffn_w_down', 'new_m_ln2_g', 'new_m_ln2_b', 'new_v_w_in', 'new_v_b_in', 'new_v_sgu_ln_g', 'new_v_sgu_ln_b', 'new_v_sgu_w', 'new_v_sgu_b', 'new_v_gdn_conv_w', 'new_v_gdn_a_log', 'new_v_gdn_dt_bias', 'new_v_gdn_norm_g', 'new_v_w_proj_a', 'new_v_w_proj_b', 'new_v_w_proj_c', 'new_v_w_out', 'new_v_ln1_g', 'new_v_ln1_b', 'new_v_ffn_w_up', 'new_v_ffn_conv_w', 'new_v_ffn_conv_b', 'new_v_ffn_w_down', 'new_v_ln2_g', 'new_v_ln2_b']
TWIN_LEAF_KINDS = {'loss': 'loss', 'grad_x': 'grad_x', 'grad_w_in': 'grad_w', 'grad_b_in': 'grad_w', 'grad_sgu_ln_g': 'grad_w', 'grad_sgu_ln_b': 'grad_w', 'grad_sgu_w': 'grad_w', 'grad_sgu_b': 'grad_w', 'grad_gdn_conv_w': 'grad_w', 'grad_gdn_a_log': 'grad_w', 'grad_gdn_dt_bias': 'grad_w', 'grad_gdn_norm_g': 'grad_w', 'grad_w_proj_a': 'grad_w', 'grad_w_proj_b': 'grad_w', 'grad_w_proj_c': 'grad_w', 'grad_w_out': 'grad_w', 'grad_ln1_g': 'grad_w', 'grad_ln1_b': 'grad_w', 'grad_ffn_w_up': 'grad_w', 'grad_ffn_conv_w': 'grad_w', 'grad_ffn_conv_b': 'grad_w', 'grad_ffn_w_down': 'grad_w', 'grad_ln2_g': 'grad_w', 'grad_ln2_b': 'grad_w', 'delta_w_in': 'delta_w', 'delta_b_in': 'delta_w', 'delta_sgu_ln_g': 'delta_w', 'delta_sgu_ln_b': 'delta_w', 'delta_sgu_w': 'delta_w', 'delta_sgu_b': 'delta_w', 'delta_gdn_conv_w': 'delta_w', 'delta_gdn_a_log': 'delta_w', 'delta_gdn_dt_bias': 'delta_w', 'delta_gdn_norm_g': 'delta_w', 'delta_w_proj_a': 'delta_w', 'delta_w_proj_b': 'delta_w', 'delta_w_proj_c': 'delta_w', 'delta_w_out': 'delta_w', 'delta_ln1_g': 'delta_w', 'delta_ln1_b': 'delta_w', 'delta_ffn_w_up': 'delta_w', 'delta_ffn_conv_w': 'delta_w', 'delta_ffn_conv_b': 'delta_w', 'delta_ffn_w_down': 'delta_w', 'delta_ln2_g': 'delta_w', 'delta_ln2_b': 'delta_w', 'new_m_w_in': 'new_m', 'new_m_b_in': 'new_m', 'new_m_sgu_ln_g': 'new_m', 'new_m_sgu_ln_b': 'new_m', 'new_m_sgu_w': 'new_m', 'new_m_sgu_b': 'new_m', 'new_m_gdn_conv_w': 'new_m', 'new_m_gdn_a_log': 'new_m', 'new_m_gdn_dt_bias': 'new_m', 'new_m_gdn_norm_g': 'new_m', 'new_m_w_proj_a': 'new_m', 'new_m_w_proj_b': 'new_m', 'new_m_w_proj_c': 'new_m', 'new_m_w_out': 'new_m', 'new_m_ln1_g': 'new_m', 'new_m_ln1_b': 'new_m', 'new_m_ffn_w_up': 'new_m', 'new_m_ffn_conv_w': 'new_m', 'new_m_ffn_conv_b': 'new_m', 'new_m_ffn_w_down': 'new_m', 'new_m_ln2_g': 'new_m', 'new_m_ln2_b': 'new_m', 'new_v_w_in': 'new_v', 'new_v_b_in': 'new_v', 'new_v_sgu_ln_g': 'new_v', 'new_v_sgu_ln_b': 'new_v', 'new_v_sgu_w': 'new_v', 'new_v_sgu_b': 'new_v', 'new_v_gdn_conv_w': 'new_v', 'new_v_gdn_a_log': 'new_v', 'new_v_gdn_dt_bias': 'new_v', 'new_v_gdn_norm_g': 'new_v', 'new_v_w_proj_a': 'new_v', 'new_v_w_proj_b': 'new_v', 'new_v_w_proj_c': 'new_v', 'new_v_w_out': 'new_v', 'new_v_ln1_g': 'new_v', 'new_v_ln1_b': 'new_v', 'new_v_ffn_w_up': 'new_v', 'new_v_ffn_conv_w': 'new_v', 'new_v_ffn_conv_b': 'new_v', 'new_v_ffn_w_down': 'new_v', 'new_v_ln2_g': 'new_v', 'new_v_ln2_b': 'new_v'}


def _forward(args):
    return _fwd_reference(*[args[k] for k in FWD_PARAMS])


def _output_shape():
    def fwd():
        inp = _fwd_setup_inputs(0)
        return _fwd_reference(*[inp[k] for k in FWD_PARAMS])
    out = _jax.eval_shape(fwd)
    return out.shape, out.dtype

N_MICROBATCH = 1
ADAM_LR = 0.001
ADAM_B1 = 0.9
ADAM_B2 = 0.999
ADAM_EPS = 1e-08
ADAM_WD = 0.01
ADAM_STEP = 10
PER_EXAMPLE_BATCH_AXIS = {'x': 0, 'loss_target': 0}
SHARED_INPUTS = []
_WEIGHT_DTYPES = {'w_in': _jnp.float32, 'b_in': _jnp.float32, 'sgu_ln_g': _jnp.float32, 'sgu_ln_b': _jnp.float32, 'sgu_w': _jnp.float32, 'sgu_b': _jnp.float32, 'gdn_conv_w': _jnp.float32, 'gdn_a_log': _jnp.float32, 'gdn_dt_bias': _jnp.float32, 'gdn_norm_g': _jnp.float32, 'w_proj_a': _jnp.float32, 'w_proj_b': _jnp.float32, 'w_proj_c': _jnp.float32, 'w_out': _jnp.float32, 'ln1_g': _jnp.float32, 'ln1_b': _jnp.float32, 'ffn_w_up': _jnp.float32, 'ffn_conv_w': _jnp.float32, 'ffn_conv_b': _jnp.float32, 'ffn_w_down': _jnp.float32, 'ln2_g': _jnp.float32, 'ln2_b': _jnp.float32}
MOMENT_SCALE = {'w_in': 3.844143e-03, 'b_in': 5.427781e-03, 'sgu_ln_g': 6.448307e-03, 'sgu_ln_b': 6.297402e-03, 'sgu_w': 6.381696e-03, 'sgu_b': 7.553268e-03, 'gdn_conv_w': 3.387716e-03, 'gdn_a_log': 1.852057e-02, 'gdn_dt_bias': 1.802906e-02, 'gdn_norm_g': 1.203543e-02, 'w_proj_a': 4.591051e-03, 'w_proj_b': 1.637293e-02, 'w_proj_c': 7.467437e-03, 'w_out': 1.845478e-02, 'ln1_g': 2.570233e-01, 'ln1_b': 1.548573e-01, 'ffn_w_up': 8.501726e-03, 'ffn_conv_w': 8.475049e-03, 'ffn_conv_b': 8.725374e-03, 'ffn_w_down': 3.261050e-02, 'ln2_g': 8.005853e+00, 'ln2_b': 2.546708e-01}


def _to_microbatches(a, axis):
    t = _jnp.moveaxis(a, axis, 0)
    t = t.reshape((N_MICROBATCH, t.shape[0] // N_MICROBATCH) + t.shape[1:])
    return _jnp.moveaxis(t, 1, axis + 1)


def setup_inputs(seed: int = 0) -> dict:
    inp = _fwd_setup_inputs(seed)
    key = _jax.random.fold_in(_jax.random.key(seed), 7919)
    shape, _ = _output_shape()
    out = dict(inp)
    out["loss_target"] = _jax.random.normal(_jax.random.fold_in(key, 0), shape, _jnp.float32)
    for i, name in enumerate(TWIN_WEIGHTS):
        w = inp[name].astype(_jnp.float32)
        if MOMENT_SCALE is None:
            s = _jnp.sqrt(_jnp.mean(_jnp.square(w)) + 1e-30)
        else:
            s = MOMENT_SCALE[name]
        km, kv = _jax.random.split(_jax.random.fold_in(key, i + 1))
        out[name] = w
        out["m_" + name] = s * _jax.random.normal(km, w.shape, _jnp.float32)
        out["v_" + name] = (s * s) * _jax.random.uniform(kv, w.shape, _jnp.float32, 0.5, 1.5)
    if N_MICROBATCH > 1:
        for name, axis in PER_EXAMPLE_BATCH_AXIS.items():
            out[name] = _to_microbatches(out[name], axis)
    return {'x': out['x'], 'w_in': out['w_in'], 'b_in': out['b_in'], 'sgu_ln_g': out['sgu_ln_g'], 'sgu_ln_b': out['sgu_ln_b'], 'sgu_w': out['sgu_w'], 'sgu_b': out['sgu_b'], 'gdn_conv_w': out['gdn_conv_w'], 'gdn_a_log': out['gdn_a_log'], 'gdn_dt_bias': out['gdn_dt_bias'], 'gdn_norm_g': out['gdn_norm_g'], 'w_proj_a': out['w_proj_a'], 'w_proj_b': out['w_proj_b'], 'w_proj_c': out['w_proj_c'], 'w_out': out['w_out'], 'ln1_g': out['ln1_g'], 'ln1_b': out['ln1_b'], 'ffn_w_up': out['ffn_w_up'], 'ffn_conv_w': out['ffn_conv_w'], 'ffn_conv_b': out['ffn_conv_b'], 'ffn_w_down': out['ffn_w_down'], 'ln2_g': out['ln2_g'], 'ln2_b': out['ln2_b'], 'loss_target': out['loss_target'], 'm_w_in': out['m_w_in'], 'm_b_in': out['m_b_in'], 'm_sgu_ln_g': out['m_sgu_ln_g'], 'm_sgu_ln_b': out['m_sgu_ln_b'], 'm_sgu_w': out['m_sgu_w'], 'm_sgu_b': out['m_sgu_b'], 'm_gdn_conv_w': out['m_gdn_conv_w'], 'm_gdn_a_log': out['m_gdn_a_log'], 'm_gdn_dt_bias': out['m_gdn_dt_bias'], 'm_gdn_norm_g': out['m_gdn_norm_g'], 'm_w_proj_a': out['m_w_proj_a'], 'm_w_proj_b': out['m_w_proj_b'], 'm_w_proj_c': out['m_w_proj_c'], 'm_w_out': out['m_w_out'], 'm_ln1_g': out['m_ln1_g'], 'm_ln1_b': out['m_ln1_b'], 'm_ffn_w_up': out['m_ffn_w_up'], 'm_ffn_conv_w': out['m_ffn_conv_w'], 'm_ffn_conv_b': out['m_ffn_conv_b'], 'm_ffn_w_down': out['m_ffn_w_down'], 'm_ln2_g': out['m_ln2_g'], 'm_ln2_b': out['m_ln2_b'], 'v_w_in': out['v_w_in'], 'v_b_in': out['v_b_in'], 'v_sgu_ln_g': out['v_sgu_ln_g'], 'v_sgu_ln_b': out['v_sgu_ln_b'], 'v_sgu_w': out['v_sgu_w'], 'v_sgu_b': out['v_sgu_b'], 'v_gdn_conv_w': out['v_gdn_conv_w'], 'v_gdn_a_log': out['v_gdn_a_log'], 'v_gdn_dt_bias': out['v_gdn_dt_bias'], 'v_gdn_norm_g': out['v_gdn_norm_g'], 'v_w_proj_a': out['v_w_proj_a'], 'v_w_proj_b': out['v_w_proj_b'], 'v_w_proj_c': out['v_w_proj_c'], 'v_w_out': out['v_w_out'], 'v_ln1_g': out['v_ln1_g'], 'v_ln1_b': out['v_ln1_b'], 'v_ffn_w_up': out['v_ffn_w_up'], 'v_ffn_conv_w': out['v_ffn_conv_w'], 'v_ffn_conv_b': out['v_ffn_conv_b'], 'v_ffn_w_down': out['v_ffn_w_down'], 'v_ln2_g': out['v_ln2_g'], 'v_ln2_b': out['v_ln2_b']}


def _loss(weights, diff, rest, loss_target):
    with _jax.named_scope("forward"):
        args = {**rest, TWIN_DIFF_INPUT: diff, **{k: w.astype(_WEIGHT_DTYPES[k]) for k, w in weights.items()}}
        y = _forward(args)
    with _jax.named_scope("loss_head"):
        err = _jnp.square(y.astype(_jnp.float32) - loss_target)
        return 0.5 * _jnp.sum(_jnp.mean(err, axis=-1)) if err.ndim else 0.5 * err


def _adamw(w, g, m, v):
    m = ADAM_B1 * m + (1.0 - ADAM_B1) * g
    v = ADAM_B2 * v + (1.0 - ADAM_B2) * _jnp.square(g)
    m_hat = m / (1.0 - ADAM_B1 ** ADAM_STEP)
    v_hat = v / (1.0 - ADAM_B2 ** ADAM_STEP)
    delta = -ADAM_LR * (m_hat / (_jnp.sqrt(v_hat) + ADAM_EPS) + ADAM_WD * w)
    return delta, m, v


def reference(x, w_in, b_in, sgu_ln_g, sgu_ln_b, sgu_w, sgu_b, gdn_conv_w, gdn_a_log, gdn_dt_bias, gdn_norm_g, w_proj_a, w_proj_b, w_proj_c, w_out, ln1_g, ln1_b, ffn_w_up, ffn_conv_w, ffn_conv_b, ffn_w_down, ln2_g, ln2_b, loss_target, m_w_in, m_b_in, m_sgu_ln_g, m_sgu_ln_b, m_sgu_w, m_sgu_b, m_gdn_conv_w, m_gdn_a_log, m_gdn_dt_bias, m_gdn_norm_g, m_w_proj_a, m_w_proj_b, m_w_proj_c, m_w_out, m_ln1_g, m_ln1_b, m_ffn_w_up, m_ffn_conv_w, m_ffn_conv_b, m_ffn_w_down, m_ln2_g, m_ln2_b, v_w_in, v_b_in, v_sgu_ln_g, v_sgu_ln_b, v_sgu_w, v_sgu_b, v_gdn_conv_w, v_gdn_a_log, v_gdn_dt_bias, v_gdn_norm_g, v_w_proj_a, v_w_proj_b, v_w_proj_c, v_w_out, v_ln1_g, v_ln1_b, v_ffn_w_up, v_ffn_conv_w, v_ffn_conv_b, v_ffn_w_down, v_ln2_g, v_ln2_b):
    given = dict(x=x, w_in=w_in, b_in=b_in, sgu_ln_g=sgu_ln_g, sgu_ln_b=sgu_ln_b, sgu_w=sgu_w, sgu_b=sgu_b, gdn_conv_w=gdn_conv_w, gdn_a_log=gdn_a_log, gdn_dt_bias=gdn_dt_bias, gdn_norm_g=gdn_norm_g, w_proj_a=w_proj_a, w_proj_b=w_proj_b, w_proj_c=w_proj_c, w_out=w_out, ln1_g=ln1_g, ln1_b=ln1_b, ffn_w_up=ffn_w_up, ffn_conv_w=ffn_conv_w, ffn_conv_b=ffn_conv_b, ffn_w_down=ffn_w_down, ln2_g=ln2_g, ln2_b=ln2_b, loss_target=loss_target, m_w_in=m_w_in, m_b_in=m_b_in, m_sgu_ln_g=m_sgu_ln_g, m_sgu_ln_b=m_sgu_ln_b, m_sgu_w=m_sgu_w, m_sgu_b=m_sgu_b, m_gdn_conv_w=m_gdn_conv_w, m_gdn_a_log=m_gdn_a_log, m_gdn_dt_bias=m_gdn_dt_bias, m_gdn_norm_g=m_gdn_norm_g, m_w_proj_a=m_w_proj_a, m_w_proj_b=m_w_proj_b, m_w_proj_c=m_w_proj_c, m_w_out=m_w_out, m_ln1_g=m_ln1_g, m_ln1_b=m_ln1_b, m_ffn_w_up=m_ffn_w_up, m_ffn_conv_w=m_ffn_conv_w, m_ffn_conv_b=m_ffn_conv_b, m_ffn_w_down=m_ffn_w_down, m_ln2_g=m_ln2_g, m_ln2_b=m_ln2_b, v_w_in=v_w_in, v_b_in=v_b_in, v_sgu_ln_g=v_sgu_ln_g, v_sgu_ln_b=v_sgu_ln_b, v_sgu_w=v_sgu_w, v_sgu_b=v_sgu_b, v_gdn_conv_w=v_gdn_conv_w, v_gdn_a_log=v_gdn_a_log, v_gdn_dt_bias=v_gdn_dt_bias, v_gdn_norm_g=v_gdn_norm_g, v_w_proj_a=v_w_proj_a, v_w_proj_b=v_w_proj_b, v_w_proj_c=v_w_proj_c, v_w_out=v_w_out, v_ln1_g=v_ln1_g, v_ln1_b=v_ln1_b, v_ffn_w_up=v_ffn_w_up, v_ffn_conv_w=v_ffn_conv_w, v_ffn_conv_b=v_ffn_conv_b, v_ffn_w_down=v_ffn_w_down, v_ln2_g=v_ln2_g, v_ln2_b=v_ln2_b)
    weights = {n: given[n] for n in TWIN_WEIGHTS}
    shared = {n: given[n] for n in SHARED_INPUTS}
    per_example = {n: given[n] for n in ['x']}
    grad_fn = _jax.value_and_grad(_loss, argnums=(0, 1))

    def one_microbatch(ex, loss_target):
        ex = dict(ex)
        diff = ex.pop(TWIN_DIFF_INPUT)
        return grad_fn(weights, diff, {**shared, **ex}, loss_target)

    if N_MICROBATCH == 1:
        loss, (grad_w, grad_x) = one_microbatch(per_example, given["loss_target"])
    else:
        def body(carry, xs):
            loss_sum, grad_sum = carry
            l_k, (gw_k, gx_k) = one_microbatch(xs[0], xs[1])
            with _jax.named_scope("update"):
                return (loss_sum + l_k, _jax.tree.map(_jnp.add, grad_sum, gw_k)), gx_k

        init = (_jnp.zeros((), _jnp.float32), _jax.tree.map(_jnp.zeros_like, weights))
        (loss, grad_w), grad_x = _jax.lax.scan(body, init, (per_example, given["loss_target"]))
    with _jax.named_scope("update"):
        delta_w, new_m, new_v = {}, {}, {}
        for n in TWIN_WEIGHTS:
            delta_w[n], new_m[n], new_v[n] = _adamw(weights[n], grad_w[n], given["m_" + n], given["v_" + n])
    return (loss, grad_x, *[grad_w[n] for n in TWIN_WEIGHTS], *[delta_w[n] for n in TWIN_WEIGHTS],
            *[new_m[n] for n in TWIN_WEIGHTS], *[new_v[n] for n in TWIN_WEIGHTS])
```

```python
import functools
import math

import jax
import jax.numpy as jnp
from jax import lax
from jax.experimental import pallas as pl
from jax.experimental.pallas import tpu as pltpu

F32 = jnp.float32
BF16 = jnp.bfloat16

N_DEV = 8
DEPTH = 4
D_MODEL = 2048
HEAD_DIM = 128
N_HEADS = 8
WIDTH = N_HEADS * HEAD_DIM
CHUNK = 64
SGU_SPAN = 128
GDN_CONV = 4
FFN_CONV = 3
D_FF = 5504
D_FF_PAD = 5632
N_IN = 15384
DEEPNORM_ALPHA = (2 * DEPTH) ** 0.25
LN_EPS = 1e-5
RMS_EPS = 1e-6
ADAM_LR = 0.001
ADAM_B1 = 0.9
ADAM_B2 = 0.999
ADAM_EPS = 1e-08
ADAM_WD = 0.01
ADAM_STEP = 10

LANES = 128
SUBLANES = 8
VMEM_LIMIT = 56 * 1024 * 1024
FLAT_COLS = 1024

MESH = pl.DeviceIdType.MESH


def _params(sem, vmem=VMEM_LIMIT):
    return pltpu.CompilerParams(dimension_semantics=sem, vmem_limit_bytes=vmem)


def _tile(n, pref):
    best = None
    for d in range(LANES, min(n, pref) + 1, LANES):
        if n % d == 0:
            best = d
    return n if best is None else best


def _sigmoid(x):
    return jax.nn.sigmoid(x)


def _silu(x):
    return x * jax.nn.sigmoid(x)


def _raw_dot(a, b, dims):
    return lax.dot_general(a.astype(BF16), b.astype(BF16), (dims, ((), ())), preferred_element_type=F32)


@jax.custom_vjp
def _bdot(a, b):
    return _raw_dot(a, b, ((1,), (0,)))


def _bdot_fwd(a, b):
    return _bdot(a, b), (a, b)


def _bdot_bwd(res, ct):
    a, b = res
    return _raw_dot(ct, b, ((1,), (1,))), _raw_dot(a, ct, ((0,), (0,)))


_bdot.defvjp(_bdot_fwd, _bdot_bwd)


@jax.custom_vjp
def _bdot_nt(a, b):
    return _raw_dot(a, b, ((1,), (1,)))


def _bdot_nt_fwd(a, b):
    return _bdot_nt(a, b), (a, b)


def _bdot_nt_bwd(res, ct):
    a, b = res
    return _raw_dot(ct, b, ((1,), (0,))), _raw_dot(ct, a, ((0,), (0,)))


_bdot_nt.defvjp(_bdot_nt_fwd, _bdot_nt_bwd)


@jax.custom_vjp
def _bdot_tn(a, b):
    return _raw_dot(a, b, ((0,), (0,)))


def _bdot_tn_fwd(a, b):
    return _bdot_tn(a, b), (a, b)


def _bdot_tn_bwd(res, ct):
    a, b = res
    return _raw_dot(b, ct, ((1,), (1,))), _raw_dot(a, ct, ((1,), (0,)))


_bdot_tn.defvjp(_bdot_tn_fwd, _bdot_tn_bwd)


def _split3(a):
    hi = a.astype(BF16)
    r = a - hi.astype(F32)
    lo = r.astype(BF16)
    return hi, lo


def _dot3_raw(a, b, dims):
    ah, al = _split3(a)
    bh, bl = _split3(b)
    d = functools.partial(lax.dot_general, dimension_numbers=(dims, ((), ())), preferred_element_type=F32)
    return d(ah, bh) + (d(ah, bl) + d(al, bh))


@jax.custom_vjp
def _dot3(a, b):
    return _dot3_raw(a, b, ((1,), (0,)))


def _dot3_fwd(a, b):
    return _dot3(a, b), (a, b)


def _dot3_bwd(res, ct):
    a, b = res
    return _dot3_raw(ct, b, ((1,), (1,))), _dot3_raw(a, ct, ((0,), (0,)))


_dot3.defvjp(_dot3_fwd, _dot3_bwd)


def _mm(a, b, *, mode, name, bias=None, add=None, add_scale=1.0, out_dtype=F32, b_off=0, b_len=None):
    if mode == "nn":
        (m, k), (k2, n) = a.shape, (b.shape[0], b_len or b.shape[1])
    elif mode == "nt":
        (m, k), (n, k2) = a.shape, (b.shape[0], b_len or b.shape[1])
    else:
        (k, m), (k2, n) = a.shape, b.shape
        assert b_off == 0 and b_len is None
    assert k == k2, (a.shape, b.shape, mode)
    tn = _tile(n, 1408)
    tm = _tile(m, 1024 if tn >= 1024 else 2048)
    tk = _tile(k, 512) if k % 512 == 0 else _tile(k, 1408)
    nk = k // tk
    on, ok = (b_off // tn, 0) if mode == "nn" else (0, b_off // tk)
    assert b_off == (on * tn if mode == "nn" else ok * tk), (b_off, tn, tk)
    dims = {"nn": ((1,), (0,)), "nt": ((1,), (1,)), "tn": ((0,), (0,))}[mode]
    a_spec = pl.BlockSpec((tk, tm), lambda i, j, kk: (kk, i)) if mode == "tn" else pl.BlockSpec((tm, tk), lambda i, j, kk: (i, kk))
    b_spec = (pl.BlockSpec((tn, tk), lambda i, j, kk: (j, kk + ok)) if mode == "nt"
              else pl.BlockSpec((tk, tn), lambda i, j, kk: (kk, j + on)))
    in_specs = [a_spec, b_spec]
    operands = [a, b]
    if bias is not None:
        in_specs.append(pl.BlockSpec((1, tn), lambda i, j, kk: (0, j)))
        operands.append(bias.reshape(1, n))
    if add is not None:
        in_specs.append(pl.BlockSpec((tm, tn), lambda i, j, kk: (i, j)))
        operands.append(add)
    has_bias, has_add = bias is not None, add is not None

    def body(*refs):
        a_ref, b_ref = refs[0], refs[1]
        pos = 2
        bias_ref = add_ref = None
        if has_bias:
            bias_ref = refs[pos]
            pos += 1
        if has_add:
            add_ref = refs[pos]
            pos += 1
        o_ref, acc_ref = refs[pos], refs[pos + 1]
        kk = pl.program_id(2)

        @pl.when(kk == 0)
        def _():
            acc_ref[...] = jnp.zeros_like(acc_ref)

        acc_ref[...] += _raw_dot(a_ref[...], b_ref[...], dims)

        @pl.when(kk == nk - 1)
        def _():
            r = acc_ref[...]
            if has_bias:
                r = r + bias_ref[...]
            if has_add:
                r = r + add_scale * add_ref[...].astype(F32)
            o_ref[...] = r.astype(out_dtype)

    return pl.pallas_call(
        body, name=name, grid=(m // tm, n // tn, nk),
        in_specs=in_specs, out_specs=pl.BlockSpec((tm, tn), lambda i, j, kk: (i, j)),
        out_shape=jax.ShapeDtypeStruct((m, n), out_dtype),
        scratch_shapes=[pltpu.VMEM((tm, tn), F32)],
        compiler_params=_params(("parallel", "parallel", "arbitrary")),
    )(*operands)


def _ln_fwd(x, r, g, b, *, name):
    m, d = x.shape
    tm = 256

    def body(x_ref, r_ref, g_ref, b_ref, y_ref, yb_ref):
        s = DEEPNORM_ALPHA * x_ref[...] + r_ref[...]
        mu = jnp.mean(s, axis=-1, keepdims=True)
        c = s - mu
        var = jnp.mean(c * c, axis=-1, keepdims=True)
        y = c * lax.rsqrt(var + LN_EPS) * g_ref[...] + b_ref[...]
        y_ref[...] = y
        yb_ref[...] = y.astype(BF16)

    row = pl.BlockSpec((tm, d), lambda i: (i, 0))
    vec = pl.BlockSpec((1, d), lambda i: (0, 0))
    return pl.pallas_call(
        body, name=name, grid=(m // tm,), in_specs=[row, row, vec, vec], out_specs=[row, row],
        out_shape=[jax.ShapeDtypeStruct((m, d), F32), jax.ShapeDtypeStruct((m, d), BF16)],
        compiler_params=_params(("parallel",)),
    )(x, r, g.reshape(1, d), b.reshape(1, d))


def _ln_bwd(x, r, g, dy, *, name):
    m, d = x.shape
    tm = 256

    def body(x_ref, r_ref, g_ref, dy_ref, ds_ref, dsb_ref, dg_ref, db_ref):
        i = pl.program_id(0)
        s = DEEPNORM_ALPHA * x_ref[...] + r_ref[...]
        mu = jnp.mean(s, axis=-1, keepdims=True)
        c = s - mu
        var = jnp.mean(c * c, axis=-1, keepdims=True)
        rstd = lax.rsqrt(var + LN_EPS)
        xhat = c * rstd
        dy_ = dy_ref[...]
        dxhat = dy_ * g_ref[...]
        m1 = jnp.mean(dxhat, axis=-1, keepdims=True)
        m2 = jnp.mean(dxhat * xhat, axis=-1, keepdims=True)
        ds = rstd * (dxhat - m1 - xhat * m2)
        ds_ref[...] = ds
        dsb_ref[...] = ds.astype(BF16)

        @pl.when(i == 0)
        def _():
            dg_ref[...] = jnp.zeros_like(dg_ref)
            db_ref[...] = jnp.zeros_like(db_ref)

        dg_ref[...] += jnp.sum(dy_ * xhat, axis=0, keepdims=True)
        db_ref[...] += jnp.sum(dy_, axis=0, keepdims=True)

    row = pl.BlockSpec((tm, d), lambda i: (i, 0))
    vec = pl.BlockSpec((1, d), lambda i: (0, 0))
    return pl.pallas_call(
        body, name=name, grid=(m // tm,), in_specs=[row, row, vec, row], out_specs=[row, row, vec, vec],
        out_shape=[jax.ShapeDtypeStruct((m, d), F32), jax.ShapeDtypeStruct((m, d), BF16),
                   jax.ShapeDtypeStruct((1, d), F32), jax.ShapeDtypeStruct((1, d), F32)],
        compiler_params=_params(("arbitrary",)),
    )(x, r, g.reshape(1, d), dy)


def _loss_head(y, t):
    m, d = y.shape
    tm = 256

    def body(y_ref, t_ref, dy_ref, l_ref):
        i = pl.program_id(0)
        e = y_ref[...] - t_ref[...]
        dy_ref[...] = e * (1.0 / d)

        @pl.when(i == 0)
        def _():
            l_ref[...] = jnp.zeros_like(l_ref)

        part = jnp.sum(jnp.sum(e * e, axis=-1, keepdims=True) * (0.5 / d), axis=0, keepdims=True)
        l_ref[...] += jnp.broadcast_to(part, l_ref.shape)

    row = pl.BlockSpec((tm, d), lambda i: (i, 0))
    dy, l = pl.pallas_call(
        body, name="loss_head", grid=(m // tm,), in_specs=[row, row],
        out_specs=[row, pl.BlockSpec((SUBLANES, LANES), lambda i: (0, 0))],
        out_shape=[jax.ShapeDtypeStruct((m, d), F32), jax.ShapeDtypeStruct((SUBLANES, LANES), F32)],
        compiler_params=_params(("arbitrary",)),
    )(y, t)
    return dy, l[0, 0]


def _conv_ext(ext, w, width):
    acc = None
    for k in range(width):
        s = width - 1 - k
        xs = ext if s == 0 else pltpu.roll(ext, s, 0)
        term = xs[SUBLANES:, :] * w[k:k + 1, :]
        acc = term if acc is None else acc + term
    return acc


def _ffn_act_fwd(hug, huv, wg, wv, bg, bv):
    m, fp = hug.shape
    ts, tc = 512, 512
    hb = ts // SUBLANES

    def body(g_ref, gh_ref, v_ref, vh_ref, wg_ref, wv_ref, bg_ref, bv_ref, o_ref):
        i = pl.program_id(1)
        keep = (i > 0).astype(F32)
        hg = _conv_ext(jnp.concatenate([gh_ref[...] * keep, g_ref[...]], axis=0), wg_ref[...], FFN_CONV) + bg_ref[...]
        hv = _conv_ext(jnp.concatenate([vh_ref[...] * keep, v_ref[...]], axis=0), wv_ref[...], FFN_CONV) + bv_ref[...]
        o_ref[...] = (_silu(hg) * hv).astype(BF16)

    tile = pl.BlockSpec((ts, tc), lambda j, i: (i, j))
    halo = pl.BlockSpec((SUBLANES, tc), lambda j, i: (jnp.maximum(i * hb - 1, 0), j))
    wsp = pl.BlockSpec((FFN_CONV, tc), lambda j, i: (0, j))
    bsp = pl.BlockSpec((1, tc), lambda j, i: (0, j))
    return pl.pallas_call(
        body, name="ffn_act_fwd", grid=(fp // tc, m // ts),
        in_specs=[tile, halo, tile, halo, wsp, wsp, bsp, bsp], out_specs=tile,
        out_shape=jax.ShapeDtypeStruct((m, fp), BF16),
        compiler_params=_params(("parallel", "parallel")),
    )(hug, hug, huv, huv, wg, wv, bg, bv)


def _ffn_act_bwd(hug, huv, wg, wv, bg, bv, dact):
    m, fp = hug.shape
    ts, tc = 512, 512
    hb = ts // SUBLANES
    ni = m // ts

    def body(g_ref, gp_ref, gn_ref, v_ref, vp_ref, vn_ref, wg_ref, wv_ref, bg_ref, bv_ref, da_ref, dan_ref,
             dg_ref, dv_ref, dwg_ref, dwv_ref, dbg_ref, dbv_ref):
        i = pl.program_id(1)
        first = (i > 0).astype(F32)
        last = (i < ni - 1).astype(F32)
        wg_, wv_ = wg_ref[...], wv_ref[...]
        ext_g = jnp.concatenate([gp_ref[...] * first, g_ref[...], gn_ref[...]], axis=0)
        ext_v = jnp.concatenate([vp_ref[...] * first, v_ref[...], vn_ref[...]], axis=0)
        hg = _conv_ext(ext_g, wg_, FFN_CONV) + bg_ref[...]
        hv = _conv_ext(ext_v, wv_, FFN_CONV) + bv_ref[...]
        da = jnp.concatenate([da_ref[...], dan_ref[...] * last], axis=0)
        sg = _sigmoid(hg)
        dhg = da * hv * (sg * (1.0 + hg * (1.0 - sg)))
        dhv = da * (hg * sg)
        n_ext = ts + SUBLANES

        def back(dh, w_, ext):
            dx = None
            dw_rows = []
            for k in range(FFN_CONV):
                s = FFN_CONV - 1 - k
                sh = dh if s == 0 else pltpu.roll(dh, n_ext - s, 0)
                term = sh[:ts, :] * w_[k:k + 1, :]
                dx = term if dx is None else dx + term
                xs = ext if s == 0 else pltpu.roll(ext, s, 0)
                dw_rows.append(jnp.sum(dh[:ts, :] * xs[SUBLANES:SUBLANES + ts, :], axis=0, keepdims=True))
            return dx, jnp.concatenate(dw_rows, axis=0), jnp.sum(dh[:ts, :], axis=0, keepdims=True)

        dxg, dwg, dbg = back(dhg, wg_, ext_g)
        dxv, dwv, dbv = back(dhv, wv_, ext_v)
        dg_ref[...] = dxg.astype(BF16)
        dv_ref[...] = dxv.astype(BF16)

        @pl.when(i == 0)
        def _():
            dwg_ref[...] = jnp.zeros_like(dwg_ref)
            dwv_ref[...] = jnp.zeros_like(dwv_ref)
            dbg_ref[...] = jnp.zeros_like(dbg_ref)
            dbv_ref[...] = jnp.zeros_like(dbv_ref)

        dwg_ref[...] += dwg
        dwv_ref[...] += dwv
        dbg_ref[...] += dbg
        dbv_ref[...] += dbv

    tile = pl.BlockSpec((ts, tc), lambda j, i: (i, j))
    prev = pl.BlockSpec((SUBLANES, tc), lambda j, i: (jnp.maximum(i * hb - 1, 0), j))
    nxt = pl.BlockSpec((SUBLANES, tc), lambda j, i: (jnp.minimum((i + 1) * hb, m // SUBLANES - 1), j))
    wsp = pl.BlockSpec((FFN_CONV, tc), lambda j, i: (0, j))
    bsp = pl.BlockSpec((1, tc), lambda j, i: (0, j))
    return pl.pallas_call(
        body, name="ffn_act_bwd", grid=(fp // tc, ni),
        in_specs=[tile, prev, nxt, tile, prev, nxt, wsp, wsp, bsp, bsp, tile, nxt],
        out_specs=[tile, tile, wsp, wsp, bsp, bsp],
        out_shape=[jax.ShapeDtypeStruct((m, fp), BF16), jax.ShapeDtypeStruct((m, fp), BF16),
                   jax.ShapeDtypeStruct((FFN_CONV, fp), F32), jax.ShapeDtypeStruct((FFN_CONV, fp), F32),
                   jax.ShapeDtypeStruct((1, fp), F32), jax.ShapeDtypeStruct((1, fp), F32)],
        compiler_params=_params(("parallel", "arbitrary")),
    )(hug, hug, hug, huv, huv, huv, wg, wv, bg, bv, dact, dact)


def _log1p_small(e):
    u = 1.0 + e
    safe = jnp.where(u == 1.0, 1.0, u - 1.0)
    return jnp.where(u == 1.0, e, jnp.log(u) * (e / safe))


def _log_sigmoid(x):
    return jnp.minimum(x, 0.0) - _log1p_small(jnp.exp(-jnp.abs(x)))


def _exact_tri_dot(tri, x):
    p1 = x.astype(BF16)
    r1 = x - p1.astype(F32)
    p2 = r1.astype(BF16)
    p3 = (r1 - p2.astype(F32)).astype(BF16)
    d = lambda p: lax.dot_general(tri, p, (((1,), (0,)), ((), ())), preferred_element_type=F32)
    return d(p1) + (d(p2) + d(p3))


def _fox_prefix_fwd(small):
    m, w = small.shape
    tb = 512
    nb = m // tb

    def body(f_ref, c_ref, carry_ref):
        i = pl.program_id(0)

        @pl.when(i == 0)
        def _():
            carry_ref[...] = jnp.zeros_like(carry_ref)

        lf = _log_sigmoid(f_ref[...])
        r = lax.broadcasted_iota(jnp.int32, (tb, tb), 0)
        c = lax.broadcasted_iota(jnp.int32, (tb, tb), 1)
        tri = (c <= r).astype(BF16)
        c_ref[...] = _exact_tri_dot(tri, lf) + carry_ref[0:1, :]
        carry_ref[0:1, :] += jnp.sum(lf, axis=0, keepdims=True)

    blk = pl.BlockSpec((tb, w), lambda i: (i, 0))
    return pl.pallas_call(
        body, name="fox_prefix_fwd", grid=(nb,), in_specs=[blk], out_specs=blk,
        out_shape=jax.ShapeDtypeStruct((m, w), F32), scratch_shapes=[pltpu.VMEM((SUBLANES, w), F32)],
        compiler_params=_params(("arbitrary",)),
    )(small)


def _fox_prefix_bwd(small, dc):
    m, w = small.shape
    tb = 512
    nb = m // tb

    def body(f_ref, dc_ref, df_ref, carry_ref):
        i = pl.program_id(0)

        @pl.when(i == 0)
        def _():
            carry_ref[...] = jnp.zeros_like(carry_ref)

        d = dc_ref[...]
        r = lax.broadcasted_iota(jnp.int32, (tb, tb), 0)
        c = lax.broadcasted_iota(jnp.int32, (tb, tb), 1)
        tri = (c >= r).astype(BF16)
        dlf = _exact_tri_dot(tri, d) + carry_ref[0:1, :]
        carry_ref[0:1, :] += jnp.sum(d, axis=0, keepdims=True)
        lane = lax.broadcasted_iota(jnp.int32, (tb, w), 1)
        df_ref[...] = jnp.where(lane < N_HEADS, dlf * _sigmoid(-f_ref[...]), 0.0)

    blk = pl.BlockSpec((tb, w), lambda i: (nb - 1 - i, 0))
    return pl.pallas_call(
        body, name="fox_prefix_bwd", grid=(nb,), in_specs=[blk, blk], out_specs=blk,
        out_shape=jax.ShapeDtypeStruct((m, w), F32), scratch_shapes=[pltpu.VMEM((SUBLANES, w), F32)],
        compiler_params=_params(("arbitrary",)),
    )(small, dc)


ATT_T = 512
NEG = -1e30


def _lane_pick(block, h):
    lane = lax.broadcasted_iota(jnp.int32, block.shape, 1)
    return jnp.sum(jnp.where(lane == h, block, 0.0), axis=1, keepdims=True)


def _att_scores(q, k, cq, ck, i, j, t):
    s = _raw_dot(q, k, ((1,), (1,))) * (HEAD_DIM ** -0.5) + cq - ck
    qpos = i * t + lax.broadcasted_iota(jnp.int32, (t, t), 0)
    kpos = j * t + lax.broadcasted_iota(jnp.int32, (t, t), 1)
    return s, qpos >= kpos


def _fox_fwd(q, k, v, c_col, c_row):
    m = q.shape[0]
    t = ATT_T
    nb = m // t

    def body(q_ref, k_ref, v_ref, cc_ref, cr_ref, o_ref, o32_ref, lse_ref):
        h, i = pl.program_id(0), pl.program_id(1)
        q = q_ref[...]
        cq = _lane_pick(cc_ref[...], h)

        def step(j, carry):
            mx, l, acc = carry
            rows = pl.ds(pl.multiple_of(j * t, t), t)
            s, ok = _att_scores(q, k_ref[rows, :], cq, cr_ref[h * nb + j], i, j, t)
            s = jnp.where(ok, s, NEG)
            mn = jnp.maximum(mx, jnp.max(s, axis=1, keepdims=True))
            p = jnp.exp(s - mn)
            a = jnp.exp(mx - mn)
            return mn, a * l + jnp.sum(p, axis=1, keepdims=True), a * acc + _raw_dot(p, v_ref[rows, :], ((1,), (0,)))

        init = (jnp.full((t, 1), NEG, F32), jnp.zeros((t, 1), F32), jnp.zeros((t, HEAD_DIM), F32))
        mx, l, acc = lax.fori_loop(0, i + 1, step, init)
        o = acc / l
        o_ref[...] = o.astype(BF16)
        o32_ref[...] = o
        lse_ref[0] = jnp.broadcast_to(mx + jnp.log(l), (t, LANES))

    return pl.pallas_call(
        body, name="fox_fwd", grid=(N_HEADS, nb),
        in_specs=[pl.BlockSpec((t, HEAD_DIM), lambda h, i: (i, h)),
                  pl.BlockSpec((m, HEAD_DIM), lambda h, i: (0, h)),
                  pl.BlockSpec((m, HEAD_DIM), lambda h, i: (0, h)),
                  pl.BlockSpec((t, LANES), lambda h, i: (i, 0)),
                  pl.BlockSpec((N_HEADS * nb, 1, t), lambda h, i: (0, 0, 0))],
        out_specs=[pl.BlockSpec((t, HEAD_DIM), lambda h, i: (i, h)), pl.BlockSpec((t, HEAD_DIM), lambda h, i: (i, h)),
                   pl.BlockSpec((1, t, LANES), lambda h, i: (h, i, 0))],
        out_shape=[jax.ShapeDtypeStruct((m, WIDTH), BF16), jax.ShapeDtypeStruct((m, WIDTH), F32),
                   jax.ShapeDtypeStruct((N_HEADS, m, LANES), F32)],
        compiler_params=_params(("parallel", "parallel")),
    )(q, k, v, c_col, c_row)


def _fox_bwd_kv(q, k, v, o, do, lse, c_col, c_row):
    m = q.shape[0]
    t = ATT_T
    nb = m // t

    def body(q_ref, k_ref, v_ref, o_ref, do_ref, lse_ref, cc_ref, cr_ref, dk_ref, dv_ref, dc_ref):
        h, j = pl.program_id(0), pl.program_id(1)
        k, v = k_ref[...], v_ref[...]
        ck = cr_ref[h * nb + j]

        def step(i, carry):
            dk, dv, dc = carry
            rows = pl.ds(pl.multiple_of(i * t, t), t)
            q, do_ = q_ref[rows, :], do_ref[rows, :]
            cq = _lane_pick(cc_ref[rows, :], h)
            lse_i = jnp.max(lse_ref[0, rows, :], axis=1, keepdims=True)
            dd = jnp.sum(do_.astype(F32) * o_ref[rows, :].astype(F32), axis=1, keepdims=True)
            s, ok = _att_scores(q, k, cq, ck, i, j, t)
            p = jnp.where(ok, jnp.exp(s - lse_i), 0.0)
            dp = _raw_dot(do_, v, ((1,), (1,)))
            ds = p * (dp - dd)
            return (dk + _raw_dot(ds, q, ((0,), (0,))), dv + _raw_dot(p, do_, ((0,), (0,))),
                    dc - jnp.sum(ds, axis=0, keepdims=True))

        init = (jnp.zeros((t, HEAD_DIM), F32), jnp.zeros((t, HEAD_DIM), F32), jnp.zeros((1, t), F32))
        dk, dv, dc = lax.fori_loop(j, nb, step, init)
        dk_ref[...] = (dk * (HEAD_DIM ** -0.5)).astype(BF16)
        dv_ref[...] = dv.astype(BF16)
        dc_ref[0] = dc

    whole = lambda col: pl.BlockSpec((m, HEAD_DIM), col)
    return pl.pallas_call(
        body, name="fox_bwd_kv", grid=(N_HEADS, nb),
        in_specs=[whole(lambda h, j: (0, h)),
                  pl.BlockSpec((t, HEAD_DIM), lambda h, j: (j, h)),
                  pl.BlockSpec((t, HEAD_DIM), lambda h, j: (j, h)),
                  whole(lambda h, j: (0, h)), whole(lambda h, j: (0, h)),
                  pl.BlockSpec((1, m, LANES), lambda h, j: (h, 0, 0)),
                  pl.BlockSpec((m, LANES), lambda h, j: (0, 0)),
                  pl.BlockSpec((N_HEADS * nb, 1, t), lambda h, j: (0, 0, 0))],
        out_specs=[pl.BlockSpec((t, HEAD_DIM), lambda h, j: (j, h)),
                   pl.BlockSpec((t, HEAD_DIM), lambda h, j: (j, h)),
                   pl.BlockSpec((1, 1, t), lambda h, j: (h * nb + j, 0, 0))],
        out_shape=[jax.ShapeDtypeStruct((m, WIDTH), BF16), jax.ShapeDtypeStruct((m, WIDTH), BF16),
                   jax.ShapeDtypeStruct((N_HEADS * nb, 1, t), F32)],
        compiler_params=_params(("parallel", "parallel")),
    )(q, k, v, o, do, lse, c_col, c_row)


def _fox_bwd_q(q, k, v, o, do, lse, c_col, c_row):
    m = q.shape[0]
    t = ATT_T
    nb = m // t

    def body(q_ref, k_ref, v_ref, o_ref, do_ref, lse_ref, cc_ref, cr_ref, dq_ref, dcq_ref):
        h, i = pl.program_id(0), pl.program_id(1)
        q, do_ = q_ref[...], do_ref[...]
        cq = _lane_pick(cc_ref[...], h)
        lse_i = jnp.max(lse_ref[0], axis=1, keepdims=True)
        dd = jnp.sum(do_.astype(F32) * o_ref[...].astype(F32), axis=1, keepdims=True)

        def step(j, carry):
            dq, dcq = carry
            rows = pl.ds(pl.multiple_of(j * t, t), t)
            k = k_ref[rows, :]
            s, ok = _att_scores(q, k, cq, cr_ref[h * nb + j], i, j, t)
            p = jnp.where(ok, jnp.exp(s - lse_i), 0.0)
            ds = p * (_raw_dot(do_, v_ref[rows, :], ((1,), (1,))) - dd)
            return dq + _raw_dot(ds, k, ((1,), (0,))), dcq + jnp.sum(ds, axis=1, keepdims=True)

        dq, dcq = lax.fori_loop(0, i + 1, step, (jnp.zeros((t, HEAD_DIM), F32), jnp.zeros((t, 1), F32)))
        dq_ref[...] = (dq * (HEAD_DIM ** -0.5)).astype(BF16)
        dcq_ref[0] = jnp.broadcast_to(dcq, (t, LANES))

    tile = lambda col: pl.BlockSpec((t, HEAD_DIM), col)
    return pl.pallas_call(
        body, name="fox_bwd_q", grid=(N_HEADS, nb),
        in_specs=[tile(lambda h, i: (i, h)),
                  pl.BlockSpec((m, HEAD_DIM), lambda h, i: (0, h)),
                  pl.BlockSpec((m, HEAD_DIM), lambda h, i: (0, h)),
                  tile(lambda h, i: (i, h)), tile(lambda h, i: (i, h)),
                  pl.BlockSpec((1, t, LANES), lambda h, i: (h, i, 0)),
                  pl.BlockSpec((t, LANES), lambda h, i: (i, 0)),
                  pl.BlockSpec((N_HEADS * nb, 1, t), lambda h, i: (0, 0, 0))],
        out_specs=[tile(lambda h, i: (i, h)), pl.BlockSpec((1, t, LANES), lambda h, i: (h, i, 0))],
        out_shape=[jax.ShapeDtypeStruct((m, WIDTH), BF16), jax.ShapeDtypeStruct((N_HEADS, m, LANES), F32)],
        compiler_params=_params(("parallel", "parallel")),
    )(q, k, v, o, do, lse, c_col, c_row)


def _row_form(c_col, t):
    m = c_col.shape[0]
    return c_col[:, :N_HEADS].T.reshape(N_HEADS * (m // t), 1, t)


def _col_form(c_row, m):
    c = c_row.reshape(N_HEADS, m).T
    return jnp.pad(c, ((0, 0), (0, LANES - N_HEADS)))


SGU_T = 512


def _sgu_tile(u, v, lg, lb, w, bsb):
    r = lax.broadcasted_iota(jnp.int32, (SGU_SPAN, SGU_SPAN), 0) // CHUNK
    c = lax.broadcasted_iota(jnp.int32, (SGU_SPAN, SGU_SPAN), 1) // CHUNK
    wm = jnp.where(r >= c, w, 0.0)
    outs = []
    for n in range(u.shape[0] // SGU_SPAN):
        rows = slice(n * SGU_SPAN, (n + 1) * SGU_SPAN)
        vs = v[rows]
        mu = jnp.mean(vs, axis=-1, keepdims=True)
        d = vs - mu
        var = jnp.mean(d * d, axis=-1, keepdims=True)
        vg = d * lax.rsqrt(var + LN_EPS) * lg + lb
        outs.append(u[rows] * (_bdot(wm, vg) + bsb))
    return jnp.concatenate(outs, axis=0)


def _sgu_specs(m):
    t = SGU_T
    tile = lambda off: pl.BlockSpec((t, HEAD_DIM), lambda g, i: (i, off + g))
    vec = pl.BlockSpec((1, 1, HEAD_DIM), lambda g, i: (g, 0, 0))
    mat = pl.BlockSpec((1, SGU_SPAN, SGU_SPAN), lambda g, i: (g, 0, 0))
    return tile, vec, mat


def _sgu_operands(ln_g, ln_b, w, b):
    return (ln_g.reshape(N_HEADS, 1, HEAD_DIM), ln_b.reshape(N_HEADS, 1, HEAD_DIM), w,
            jnp.broadcast_to(b[:, :, None], (N_HEADS, SGU_SPAN, SGU_SPAN)))


def _sgu_fwd(u, v, ln_g, ln_b, w, b):
    m = u.shape[0]
    tile, vec, mat = _sgu_specs(m)

    def body(u_ref, v_ref, lg_ref, lb_ref, w_ref, b_ref, o_ref):
        o_ref[...] = _sgu_tile(u_ref[...], v_ref[...], lg_ref[0], lb_ref[0], w_ref[0], b_ref[0]).astype(BF16)

    return pl.pallas_call(
        body, name="sgu_fwd", grid=(N_HEADS, m // SGU_T),
        in_specs=[tile(0), tile(0), vec, vec, mat, mat], out_specs=tile(0),
        out_shape=jax.ShapeDtypeStruct((m, WIDTH), BF16),
        compiler_params=_params(("parallel", "parallel")),
    )(u, v, *_sgu_operands(ln_g, ln_b, w, b))


def _sgu_bwd(u, v, ln_g, ln_b, w, b, dy):
    m = u.shape[0]
    tile, vec, mat = _sgu_specs(m)

    def body(u_ref, v_ref, lg_ref, lb_ref, w_ref, b_ref, dy_ref, du_ref, dv_ref, dlg_ref, dlb_ref, dw_ref, db_ref):
        i = pl.program_id(1)
        _, vjp = jax.vjp(_sgu_tile, u_ref[...], v_ref[...], lg_ref[0], lb_ref[0], w_ref[0], b_ref[0])
        du, dv, dlg, dlb, dw, dbsb = vjp(dy_ref[...])
        du_ref[...] = du.astype(BF16)
        dv_ref[...] = dv.astype(BF16)

        @pl.when(i == 0)
        def _():
            dlg_ref[...] = jnp.zeros_like(dlg_ref)
            dlb_ref[...] = jnp.zeros_like(dlb_ref)
            dw_ref[...] = jnp.zeros_like(dw_ref)
            db_ref[...] = jnp.zeros_like(db_ref)

        dlg_ref[0] += dlg
        dlb_ref[0] += dlb
        dw_ref[0] += dw
        db_ref[0] += jnp.broadcast_to(jnp.sum(dbsb, axis=1, keepdims=True), (SGU_SPAN, SGU_SPAN))

    vshape = jax.ShapeDtypeStruct((N_HEADS, 1, HEAD_DIM), F32)
    mshape = jax.ShapeDtypeStruct((N_HEADS, SGU_SPAN, SGU_SPAN), F32)
    return pl.pallas_call(
        body, name="sgu_bwd", grid=(N_HEADS, m // SGU_T),
        in_specs=[tile(0), tile(0), vec, vec, mat, mat, tile(0)],
        out_specs=[tile(0), tile(0), vec, vec, mat, mat],
        out_shape=[jax.ShapeDtypeStruct((m, WIDTH), BF16), jax.ShapeDtypeStruct((m, WIDTH), BF16),
                   vshape, vshape, mshape, mshape],
        compiler_params=_params(("parallel", "arbitrary")),
    )(u, v, *_sgu_operands(ln_g, ln_b, w, b), dy)


def _merge_fwd(ys, ws, gates):
    m, d = gates[0].shape
    tm, tn = 512, _tile(d, 512)

    def body(ya, yb, yc, wa, wb, wc, ga, gb, gc, o_ref, za, zb, zc):
        acc = None
        for y_ref, w_ref, g_ref, z_ref in ((ya, wa, ga, za), (yb, wb, gb, zb), (yc, wc, gc, zc)):
            z = _raw_dot(y_ref[...], w_ref[...], ((1,), (0,)))
            z_ref[...] = z
            term = _sigmoid(g_ref[...]) * z
            acc = term if acc is None else acc + term
        o_ref[...] = acc.astype(BF16)

    ysp = pl.BlockSpec((tm, WIDTH), lambda i, j: (i, 0))
    wsp = pl.BlockSpec((WIDTH, tn), lambda i, j: (0, j))
    tsp = pl.BlockSpec((tm, tn), lambda i, j: (i, j))
    zs = jax.ShapeDtypeStruct((m, d), F32)
    return pl.pallas_call(
        body, name="merge_fwd", grid=(m // tm, d // tn),
        in_specs=[ysp] * 3 + [wsp] * 3 + [tsp] * 3, out_specs=[tsp] * 4,
        out_shape=[jax.ShapeDtypeStruct((m, d), BF16), zs, zs, zs],
        compiler_params=_params(("parallel", "parallel")),
    )(*ys, *ws, *gates)


def _merge_bwd(dmerged, zs, gates):
    m, d = dmerged.shape
    tm, tn = 512, _tile(d, 1024)

    def body(dm_ref, za, zb, zc, ga, gb, gc, dza, dzb, dzc, dga, dgb, dgc):
        dm = dm_ref[...]
        for z_ref, g_ref, dz_ref, dg_ref in ((za, ga, dza, dga), (zb, gb, dzb, dgb), (zc, gc, dzc, dgc)):
            s = _sigmoid(g_ref[...])
            dz_ref[...] = (dm * s).astype(BF16)
            dg_ref[...] = (dm * z_ref[...] * (s * (1.0 - s))).astype(BF16)

    tsp = pl.BlockSpec((tm, tn), lambda i, j: (i, j))
    bs = jax.ShapeDtypeStruct((m, d), BF16)
    return pl.pallas_call(
        body, name="merge_bwd", grid=(m // tm, d // tn),
        in_specs=[tsp] * 7, out_specs=[tsp] * 6, out_shape=[bs] * 6,
        compiler_params=_params(("parallel", "parallel")),
    )(dmerged, *zs, *gates)


GDN_T = 512
GDN_A_LANE = N_HEADS
GDN_B_LANE = 2 * N_HEADS


@functools.partial(jax.custom_vjp, nondiff_argnums=(2,))
def _conv_halo(ext, w, width):
    return _conv_ext(ext, w, width)


def _conv_halo_fwd(ext, w, width):
    return _conv_ext(ext, w, width), (ext, w)


def _conv_halo_bwd(width, res, dy):
    ext, w = res
    n = ext.shape[0]
    dyp = jnp.concatenate([jnp.zeros((SUBLANES, dy.shape[1]), dy.dtype), dy], axis=0)
    dext = None
    dw_rows = []
    for k in range(width):
        s = width - 1 - k
        sh = dyp if s == 0 else pltpu.roll(dyp, n - s, 0)
        term = sh * w[k:k + 1, :]
        dext = term if dext is None else dext + term
        xs = ext if s == 0 else pltpu.roll(ext, s, 0)
        dw_rows.append(jnp.sum(dy * xs[SUBLANES:, :], axis=0, keepdims=True))
    return dext, jnp.concatenate(dw_rows, axis=0)


_conv_halo.defvjp(_conv_halo_fwd, _conv_halo_bwd)


@jax.custom_jvp
def _softplus(x):
    return jnp.maximum(x, 0.0) + _log1p_small(jnp.exp(-jnp.abs(x)))


@_softplus.defjvp
def _softplus_jvp(primals, tangents):
    (x,), (t,) = primals, tangents
    return _softplus(x), t * _sigmoid(x)


def _tri_inv(a, ii, jj):
    eye = (ii == jj).astype(F32)
    same = (ii // 16) == (jj // 16)
    ad = jnp.where(same, a, 0.0)
    ao = a - ad
    b1 = -ad
    b2 = _dot3(b1, b1)
    b4 = _dot3(b2, b2)
    b8 = _dot3(b4, b4)
    dinv = _dot3(_dot3(_dot3(eye + b1, eye + b2), eye + b4), eye + b8)
    n1 = _dot3(dinv, ao)
    n2 = _dot3(n1, n1)
    return _dot3(_dot3(eye - n1, eye + n2), dinv)


def _gdn_chunk(s_in, eq, ek, ev, a_col, b_col, gate, wq, wk, wv, a_log, dt_bias, ng):
    q = _silu(_conv_halo(eq, wq, GDN_CONV))
    k = _silu(_conv_halo(ek, wk, GDN_CONV))
    v = _silu(_conv_halo(ev, wv, GDN_CONV))
    q = q * lax.rsqrt(jnp.sum(q * q, axis=-1, keepdims=True) + RMS_EPS) * (HEAD_DIM ** -0.5)
    k = k * lax.rsqrt(jnp.sum(k * k, axis=-1, keepdims=True) + RMS_EPS)
    g = -jnp.exp(a_log) * _softplus(a_col + dt_bias)
    beta = _sigmoid(b_col)
    ii = lax.broadcasted_iota(jnp.int32, (CHUNK, CHUNK), 0)
    jj = lax.broadcasted_iota(jnp.int32, (CHUNK, CHUNK), 1)
    gb = jnp.broadcast_to(g, (CHUNK, CHUNK))
    g_row = jnp.sum(jnp.where(ii == jj, gb, 0.0), axis=0, keepdims=True)
    gc_row = jnp.sum(jnp.where(ii <= jj, gb, 0.0), axis=0, keepdims=True)
    gc_col = jnp.sum(jnp.where(jj <= ii, jnp.broadcast_to(g_row, (CHUNK, CHUNK)), 0.0), axis=1, keepdims=True)
    g_last = jnp.sum(g, axis=0, keepdims=True)
    causal = ii >= jj
    decay = jnp.where(causal, jnp.exp(jnp.where(causal, gc_col - gc_row, 0.0)), 0.0)
    kb = k * beta
    a_kk = jnp.where(ii > jj, _bdot_nt(kb, k) * decay, 0.0)
    t_inv = _tri_inv(a_kk, ii, jj)
    u = _dot3(t_inv, v * beta)
    w = _dot3(t_inv, kb * jnp.exp(gc_col))
    qk = jnp.where(causal, _bdot_nt(q, k) * decay, 0.0)
    k_dec = k * jnp.exp(g_last - gc_col)
    q_dec = q * jnp.exp(gc_col)
    v_new = u - _bdot(w, s_in)
    o = _bdot(q_dec, s_in) + _bdot(qk, v_new)
    s_out = s_in * jnp.exp(g_last) + _bdot_tn(k_dec, v_new)
    o = o * lax.rsqrt(jnp.mean(o * o, axis=-1, keepdims=True) + RMS_EPS) * ng
    return s_out, o * _silu(gate)


def _gdn_specs(m, rev):
    t = GDN_T
    ns = m // t
    hb = t // SUBLANES
    pos = (lambda s: ns - 1 - s) if rev else (lambda s: s)
    tile = pl.BlockSpec((t, HEAD_DIM), lambda h, s: (pos(s), h))
    halo = pl.BlockSpec((SUBLANES, HEAD_DIM), lambda h, s: (jnp.maximum(pos(s) * hb - 1, 0), h))
    small = pl.BlockSpec((t, LANES), lambda h, s: (pos(s), 0))
    wsp = pl.BlockSpec((GDN_CONV, HEAD_DIM), lambda h, s: (0, h))
    sc = pl.BlockSpec((1, 1, LANES), lambda h, s: (h, 0, 0))
    ngs = pl.BlockSpec((1, HEAD_DIM), lambda h, s: (0, 0))
    st = pl.BlockSpec((1, t // CHUNK, HEAD_DIM, HEAD_DIM), lambda h, s: (h, pos(s), 0, 0))
    return tile, halo, small, wsp, sc, ngs, st


def _gdn_scalars(a_log, dt_bias):
    bc = lambda p: jnp.broadcast_to(p.reshape(N_HEADS, 1, 1), (N_HEADS, 1, LANES))
    return bc(a_log), bc(dt_bias)


def _gdn_fill_ext(xpad_ref, tiles, halos, keep):
    for sec in range(3):
        xpad_ref[sec, 0:SUBLANES, :] = halos[sec][...] * keep
        xpad_ref[sec, SUBLANES:, :] = tiles[sec][...]


def _gdn_fwd(xq, xk, xv, small, gate, wq, wk, wv, a_log, dt_bias, norm_g):
    m = xq.shape[0]
    t = GDN_T
    nc = t // CHUNK
    tile, halo, smallsp, wsp, sc, ngs, st = _gdn_specs(m, False)

    def body(q_ref, qh_ref, k_ref, kh_ref, v_ref, vh_ref, sm_ref, gate_ref, wq_ref, wk_ref, wv_ref,
             al_ref, dt_ref, ng_ref, y_ref, st_ref, xpad_ref, s_ref):
        h, s = pl.program_id(0), pl.program_id(1)
        _gdn_fill_ext(xpad_ref, (q_ref, k_ref, v_ref), (qh_ref, kh_ref, vh_ref), (s > 0).astype(F32))

        @pl.when(s == 0)
        def _():
            s_ref[...] = jnp.zeros_like(s_ref)

        wq_, wk_, wv_, ng = wq_ref[...], wk_ref[...], wv_ref[...], ng_ref[...]
        al, dt = al_ref[0][:, 0:1], dt_ref[0][:, 0:1]

        def step(c, state):
            off = pl.multiple_of(c * CHUNK, CHUNK)
            sm = sm_ref[pl.ds(off, CHUNK), :]
            st_ref[0, c] = state
            state, y = _gdn_chunk(
                state, xpad_ref[0, pl.ds(off, CHUNK + SUBLANES), :], xpad_ref[1, pl.ds(off, CHUNK + SUBLANES), :],
                xpad_ref[2, pl.ds(off, CHUNK + SUBLANES), :], _lane_pick(sm, GDN_A_LANE + h), _lane_pick(sm, GDN_B_LANE + h),
                gate_ref[pl.ds(off, CHUNK), :], wq_, wk_, wv_, al, dt, ng)
            y_ref[pl.ds(off, CHUNK), :] = y.astype(BF16)
            return state

        s_ref[...] = lax.fori_loop(0, nc, step, s_ref[...])

    return pl.pallas_call(
        body, name="gdn_fwd", grid=(N_HEADS, m // t),
        in_specs=[tile, halo, tile, halo, tile, halo, smallsp, tile, wsp, wsp, wsp, sc, sc, ngs],
        out_specs=[tile, st],
        out_shape=[jax.ShapeDtypeStruct((m, WIDTH), BF16),
                   jax.ShapeDtypeStruct((N_HEADS, m // CHUNK, HEAD_DIM, HEAD_DIM), F32)],
        scratch_shapes=[pltpu.VMEM((3, t + SUBLANES, HEAD_DIM), F32), pltpu.VMEM((HEAD_DIM, HEAD_DIM), F32)],
        compiler_params=_params(("parallel", "arbitrary")),
    )(xq, xq, xk, xk, xv, xv, small, gate, wq, wk, wv, *_gdn_scalars(a_log, dt_bias), norm_g.reshape(1, HEAD_DIM))


def _gdn_bwd(xq, xk, xv, small, gate, wq, wk, wv, a_log, dt_bias, norm_g, states, dy):
    m = xq.shape[0]
    t = GDN_T
    nc = t // CHUNK
    tile, halo, smallsp, wsp, sc, ngs, st = _gdn_specs(m, True)

    def body(q_ref, qh_ref, k_ref, kh_ref, v_ref, vh_ref, sm_ref, gate_ref, wq_ref, wk_ref, wv_ref,
             al_ref, dt_ref, ng_ref, st_ref, dy_ref,
             dq_ref, dk_ref, dv_ref, dgate_ref, dab_ref, dwq_ref, dwk_ref, dwv_ref, dal_ref, ddt_ref, dng_ref,
             xpad_ref, dxpad_ref, ds_ref, carry_ref):
        h, s = pl.program_id(0), pl.program_id(1)
        _gdn_fill_ext(xpad_ref, (q_ref, k_ref, v_ref), (qh_ref, kh_ref, vh_ref), (s < m // t - 1).astype(F32))

        @pl.when(s == 0)
        def _():
            ds_ref[...] = jnp.zeros_like(ds_ref)
            carry_ref[...] = jnp.zeros_like(carry_ref)
            for r in (dwq_ref, dwk_ref, dwv_ref, dal_ref, ddt_ref):
                r[...] = jnp.zeros_like(r)

        @pl.when((s == 0) & (h == 0))
        def _():
            dng_ref[...] = jnp.zeros_like(dng_ref)

        dxpad_ref[:, 0:t, :] = jnp.zeros((3, t, HEAD_DIM), F32)
        dxpad_ref[:, t:, :] = carry_ref[...]
        wq_, wk_, wv_, ng = wq_ref[...], wk_ref[...], wv_ref[...], ng_ref[...]
        al, dt = al_ref[0][:, 0:1], dt_ref[0][:, 0:1]
        lane = lax.broadcasted_iota(jnp.int32, (CHUNK, LANES), 1)

        def step(cc, carry):
            d_state, acc = carry
            c = nc - 1 - cc
            off = pl.multiple_of(c * CHUNK, CHUNK)
            ext = pl.ds(off, CHUNK + SUBLANES)
            sm = sm_ref[pl.ds(off, CHUNK), :]
            _, vjp = jax.vjp(
                _gdn_chunk, st_ref[0, c], xpad_ref[0, ext, :], xpad_ref[1, ext, :], xpad_ref[2, ext, :],
                _lane_pick(sm, GDN_A_LANE + h), _lane_pick(sm, GDN_B_LANE + h), gate_ref[pl.ds(off, CHUNK), :],
                wq_, wk_, wv_, al, dt, ng)
            g = vjp((d_state, dy_ref[pl.ds(off, CHUNK), :]))
            for sec in range(3):
                dxpad_ref[sec, ext, :] += g[1 + sec]
            dab_ref[0, pl.ds(off, CHUNK), :] = jnp.where(lane == 0, g[4], jnp.where(lane == 1, g[5], 0.0))
            dgate_ref[pl.ds(off, CHUNK), :] = g[6].astype(BF16)
            return g[0], tuple(a + b for a, b in zip(acc, g[7:]))

        zero = (jnp.zeros((GDN_CONV, HEAD_DIM), F32),) * 3 + (jnp.zeros((1, 1), F32),) * 2 + (jnp.zeros((1, HEAD_DIM), F32),)
        d_state, acc = lax.fori_loop(0, nc, step, (ds_ref[...], zero))
        ds_ref[...] = d_state
        carry_ref[...] = dxpad_ref[:, 0:SUBLANES, :]
        dq_ref[...] = dxpad_ref[0, SUBLANES:, :].astype(BF16)
        dk_ref[...] = dxpad_ref[1, SUBLANES:, :].astype(BF16)
        dv_ref[...] = dxpad_ref[2, SUBLANES:, :].astype(BF16)
        dwq_ref[...] += acc[0]
        dwk_ref[...] += acc[1]
        dwv_ref[...] += acc[2]
        dal_ref[0] += jnp.broadcast_to(acc[3], (1, LANES))
        ddt_ref[0] += jnp.broadcast_to(acc[4], (1, LANES))
        dng_ref[...] += acc[5]

    bs = jax.ShapeDtypeStruct((m, WIDTH), BF16)
    ws = jax.ShapeDtypeStruct((GDN_CONV, WIDTH), F32)
    ss = jax.ShapeDtypeStruct((N_HEADS, 1, LANES), F32)
    ns = m // t
    dabsp = pl.BlockSpec((1, t, LANES), lambda h, s: (h, ns - 1 - s, 0))
    return pl.pallas_call(
        body, name="gdn_bwd", grid=(N_HEADS, ns),
        in_specs=[tile, halo, tile, halo, tile, halo, smallsp, tile, wsp, wsp, wsp, sc, sc, ngs, st, tile],
        out_specs=[tile, tile, tile, tile, dabsp, wsp, wsp, wsp, sc, sc, ngs],
        out_shape=[bs, bs, bs, bs, jax.ShapeDtypeStruct((N_HEADS, m, LANES), F32), ws, ws, ws, ss, ss,
                   jax.ShapeDtypeStruct((1, HEAD_DIM), F32)],
        scratch_shapes=[pltpu.VMEM((3, t + SUBLANES, HEAD_DIM), F32), pltpu.VMEM((3, t + SUBLANES, HEAD_DIM), F32),
                        pltpu.VMEM((HEAD_DIM, HEAD_DIM), F32), pltpu.VMEM((3, SUBLANES, HEAD_DIM), F32)],
        compiler_params=_params(("arbitrary", "arbitrary")),
    )(xq, xq, xk, xk, xv, xv, small, gate, wq, wk, wv, *_gdn_scalars(a_log, dt_bias), norm_g.reshape(1, HEAD_DIM),
      states, dy)


def _colsum(a, name):
    m, n = a.shape
    tm, tn = 512, _tile(n, 1024)

    def body(a_ref, o_ref):
        @pl.when(pl.program_id(1) == 0)
        def _():
            o_ref[...] = jnp.zeros_like(o_ref)

        o_ref[...] += jnp.sum(a_ref[...].astype(F32), axis=0, keepdims=True)

    return pl.pallas_call(
        body, name=name, grid=(n // tn, m // tm),
        in_specs=[pl.BlockSpec((tm, tn), lambda j, i: (i, j))], out_specs=pl.BlockSpec((1, tn), lambda j, i: (0, j)),
        out_shape=jax.ShapeDtypeStruct((1, n), F32),
        compiler_params=_params(("parallel", "arbitrary")),
    )(a)[0]


ANY = pl.BlockSpec(memory_space=pl.ANY)


def _mesh_pos():
    return lax.axis_index("x"), lax.axis_index("y"), lax.axis_index("c")


def _other_chips(px, py):
    return [(1 - px, py), (px, 1 - py), (1 - px, 1 - py)]


def _all_gather(xs, name):
    n = len(xs)

    def body(*refs):
        x_refs, out_refs = refs[:n], refs[n:2 * n]
        send_sems, recv_sems, local_sems = refs[2 * n:]
        px, py, pc = _mesh_pos()
        me, sibling = (px, py, pc), (px, py, 1 - pc)
        chips = _other_chips(px, py)

        def slot(a, qx, qy, qc):
            return out_refs[a].at[4 * qx + 2 * qy + qc]

        def copy(a, k, block, to, src=None):
            return pltpu.make_async_remote_copy(
                src_ref=slot(a, *block) if src is None else src, dst_ref=slot(a, *block),
                send_sem=send_sems.at[a, k], recv_sem=recv_sems.at[a, k], device_id=to, device_id_type=MESH)

        mine = [pltpu.make_async_copy(x_refs[a], slot(a, *me), local_sems.at[a]) for a in range(n)]
        for cp in mine:
            cp.start()
        first = []
        for a in range(n):
            first.append(copy(a, 0, me, sibling, src=x_refs[a]))
            first += [copy(a, 1 + j, me, (*chip, pc), src=x_refs[a]) for j, chip in enumerate(chips)]
        for cp in first:
            cp.start()
        passed = []
        for j, chip in enumerate(chips):
            for a in range(n):
                copy(a, 1 + j, (*chip, pc), me).wait_recv()
                passed.append(copy(a, 4 + j, (*chip, pc), sibling))
                passed[-1].start()
        for a in range(n):
            copy(a, 0, sibling, me).wait_recv()
            for j, chip in enumerate(chips):
                copy(a, 4 + j, (*chip, 1 - pc), me).wait_recv()
        for cp in first + passed:
            cp.wait_send()
        for cp in mine:
            cp.wait()

    return pl.pallas_call(
        body, name=name, out_shape=[jax.ShapeDtypeStruct((N_DEV,) + x.shape, x.dtype) for x in xs],
        in_specs=[ANY] * n, out_specs=[ANY] * n,
        scratch_shapes=[pltpu.SemaphoreType.DMA((n, 7)), pltpu.SemaphoreType.DMA((n, 7)), pltpu.SemaphoreType.DMA((n,))],
    )(*xs)


def _pair_exchange(gs, name):
    n = len(gs)

    def body(*refs):
        g_refs, r_refs = refs[:n], refs[n:2 * n]
        send_sems, recv_sems = refs[2 * n:]
        px, py, pc = _mesh_pos()
        copies = [pltpu.make_async_remote_copy(
            src_ref=g_refs[a].at[2 * j + (1 - pc)], dst_ref=r_refs[a].at[j], send_sem=send_sems.at[a, j],
            recv_sem=recv_sems.at[a, j], device_id=(px, py, 1 - pc), device_id_type=MESH)
            for a in range(n) for j in range(4)]
        for cp in copies:
            cp.start()
        for cp in copies:
            cp.wait()

    return pl.pallas_call(
        body, name=name, out_shape=[jax.ShapeDtypeStruct((4,) + g.shape[1:], g.dtype) for g in gs],
        in_specs=[ANY] * n, out_specs=[ANY] * n,
        scratch_shapes=[pltpu.SemaphoreType.DMA((n, 4)), pltpu.SemaphoreType.DMA((n, 4))],
    )(*gs)


def _chip_exchange(hs, name):
    n = len(hs)

    def body(*refs):
        h_refs, r_refs = refs[:n], refs[n:2 * n]
        send_sems, recv_sems = refs[2 * n:]
        px, py, pc = _mesh_pos()
        copies = [pltpu.make_async_remote_copy(
            src_ref=h_refs[a].at[2 * cx + cy], dst_ref=r_refs[a].at[k], send_sem=send_sems.at[a, k],
            recv_sem=recv_sems.at[a, k], device_id=(cx, cy, pc), device_id_type=MESH)
            for a in range(n) for k, (cx, cy) in enumerate(_other_chips(px, py))]
        for cp in copies:
            cp.start()
        for cp in copies:
            cp.wait()

    return pl.pallas_call(
        body, name=name, out_shape=[jax.ShapeDtypeStruct((3,) + h.shape[1:], h.dtype) for h in hs],
        in_specs=[ANY] * n, out_specs=[ANY] * n,
        scratch_shapes=[pltpu.SemaphoreType.DMA((n, 3)), pltpu.SemaphoreType.DMA((n, 3))],
    )(*hs)


def _select_dot(blocks, target, valid):
    src = jnp.concatenate(blocks, axis=1)
    n = src.shape[1]
    pos = lax.broadcasted_iota(jnp.int32, (n, LANES), 0)
    sel = ((pos == target) & valid).astype(src.dtype)
    return lax.dot_general(src, sel, (((1,), (0,)), ((), ())), preferred_element_type=F32)


def _shards_to_pieces(g, pieces, shard_w, name):
    _, nl, r, pw = g.shape
    nb = pw // LANES
    c0s, nvs, blks = [], [], []
    for start, real, padded in pieces:
        for u in range(padded // LANES):
            c0, nv = start + LANES * u, max(0, min(LANES, real - LANES * u))
            cp_lo = c0 + (pw - shard_w) * min(c0 // shard_w, N_DEV - 1)
            cp_hi = c0 + nv - 1 + (pw - shard_w) * min((c0 + nv - 1) // shard_w, N_DEV - 1)
            assert nv == 0 or cp_hi - (cp_lo // LANES) * LANES < 3 * LANES
            c0s.append(c0)
            nvs.append(nv)
            blks.append(cp_lo // LANES if nv else 0)
    table = jnp.array([c0s, nvs, blks], jnp.int32)
    tr = min(r, 1024)
    last = N_DEV * nb - 1

    def body(tab_ref, b0, b1, b2, o_ref):
        t = pl.program_id(2)
        c0, nv, blk0 = tab_ref[0, t], tab_ref[1, t], tab_ref[2, t]
        j = lax.broadcasted_iota(jnp.int32, (1, LANES), 1)
        c = c0 + j
        k = jnp.zeros_like(c)
        for q in range(1, N_DEV):
            k = k + (c >= q * shard_w).astype(jnp.int32)
        target = c + (pw - shard_w) * k - blk0 * LANES
        o_ref[0] = _select_dot([b0[0, 0], b1[0, 0], b2[0, 0]], target, j < nv).astype(o_ref.dtype)

    def src(d):
        def index(l, i, t, tab_ref):
            blk = jnp.minimum(tab_ref[2, t] + d, last)
            return (blk // nb, l, i, blk % nb)
        return pl.BlockSpec((1, 1, tr, LANES), index)

    return pl.pallas_call(
        body, name=name, out_shape=jax.ShapeDtypeStruct((nl, r, LANES * len(c0s)), g.dtype),
        grid_spec=pltpu.PrefetchScalarGridSpec(
            num_scalar_prefetch=1, grid=(nl, r // tr, len(c0s)), in_specs=[src(0), src(1), src(2)],
            out_specs=pl.BlockSpec((1, tr, LANES), lambda l, i, t, tab_ref: (l, i, t))),
        compiler_params=_params(("parallel", "parallel", "parallel")),
    )(table, g, g, g)


def _pieces_to_shards(d, segs, far_start, shard_w, pw, name):
    nl, r, s_cols = d.shape
    nb = pw // LANES
    shifts = [src - gs for gs, src in segs]

    def source(c):
        return c + [sh for (gs, _), sh in zip(segs, shifts) if c >= gs][-1]

    nvs, blks = [], []
    for k in range(N_DEV):
        for b in range(nb):
            c0, nv = k * shard_w + LANES * b, max(0, min(LANES, shard_w - LANES * b))
            near = [source(c) for c in range(c0, c0 + nv)]
            near = [s for s in near if far_start is None or s < far_start]
            blk0 = min(near) // LANES if near else 0
            assert not near or max(near) - blk0 * LANES < 3 * LANES
            nvs.append(nv)
            blks.append(blk0)
    table = jnp.array([nvs, blks], jnp.int32)
    tr = min(r, 1024)
    last = s_cols // LANES - 1
    far_blk = 0 if far_start is None else far_start // LANES

    def body(tab_ref, b0, b1, b2, bf, o_ref):
        t = pl.program_id(2)
        nv, blk0 = tab_ref[0, t], tab_ref[1, t]
        j = lax.broadcasted_iota(jnp.int32, (1, LANES), 1)
        c = (t // nb) * shard_w + (t % nb) * LANES + j
        s = c + shifts[0]
        for (gs, _), prev, sh in zip(segs[1:], shifts[:-1], shifts[1:]):
            s = s + (sh - prev) * (c >= gs).astype(jnp.int32)
        target = s - blk0 * LANES
        if far_start is not None:
            target = jnp.where(s >= far_start, 3 * LANES + s - far_start, target)
        o_ref[0, 0] = _select_dot([b0[0], b1[0], b2[0], bf[0]], target, j < nv).astype(o_ref.dtype)

    def src(dd):
        return pl.BlockSpec((1, tr, LANES), lambda l, i, t, tab_ref: (l, i, jnp.minimum(tab_ref[1, t] + dd, last)))

    return pl.pallas_call(
        body, name=name, out_shape=jax.ShapeDtypeStruct((N_DEV, nl, r, pw), d.dtype),
        grid_spec=pltpu.PrefetchScalarGridSpec(
            num_scalar_prefetch=1, grid=(nl, r // tr, N_DEV * nb),
            in_specs=[src(0), src(1), src(2), pl.BlockSpec((1, tr, LANES), lambda l, i, t, tab_ref: (l, i, far_blk))],
            out_specs=pl.BlockSpec((1, 1, tr, LANES), lambda l, i, t, tab_ref: (t // nb, l, i, t % nb))),
        compiler_params=_params(("parallel", "parallel", "parallel")),
    )(table, d, d, d, d)


def _cat_blocks(g, axis):
    _, nl, r, c = g.shape
    shape = (nl, N_DEV * r, c) if axis == 0 else (nl, r, N_DEV * c)

    def body(g_ref, o_ref):
        o_ref[0] = g_ref[0, 0]

    return pl.pallas_call(
        body, name="cat_blocks", grid=(nl, N_DEV), out_shape=jax.ShapeDtypeStruct(shape, g.dtype),
        in_specs=[pl.BlockSpec((1, 1, r, c), lambda l, k: (k, l, 0, 0))],
        out_specs=pl.BlockSpec((1, r, c), (lambda l, k: (l, k, 0)) if axis == 0 else (lambda l, k: (l, 0, k))),
        compiler_params=_params(("parallel", "parallel")),
    )(g)


def _split_blocks(d, axis):
    nl = d.shape[0]
    r, c = (d.shape[1] // N_DEV, d.shape[2]) if axis == 0 else (d.shape[1], d.shape[2] // N_DEV)

    def body(d_ref, o_ref):
        o_ref[0, 0] = d_ref[0]

    return pl.pallas_call(
        body, name="split_blocks", grid=(nl, N_DEV), out_shape=jax.ShapeDtypeStruct((N_DEV, nl, r, c), d.dtype),
        in_specs=[pl.BlockSpec((1, r, c), (lambda l, k: (l, k, 0)) if axis == 0 else (lambda l, k: (l, 0, k)))],
        out_specs=pl.BlockSpec((1, 1, r, c), lambda l, k: (k, l, 0, 0)),
        compiler_params=_params(("parallel", "parallel")),
    )(d)


def _row_tile(r, mult, pref=256):
    return max(d for d in range(mult, pref + 1, mult) if r % d == 0)


def _pair_sum(g, r1):
    _, r, c = g.shape
    tr = _row_tile(r, 16)

    def body(g_ref, r_ref, h_ref, own_ref):
        j = pl.program_id(1)
        px, py, pc = _mesh_pos()
        mine = jnp.where(pc == 0, g_ref[0, 0].astype(F32), g_ref[0, 1].astype(F32))
        val = (mine + r_ref[0].astype(F32)).astype(h_ref.dtype)
        h_ref[0] = val

        @pl.when(j == 2 * px + py)
        def _():
            own_ref[...] = val

    return pl.pallas_call(
        body, name="rs_pair_sum", grid=(r // tr, 4),
        in_specs=[pl.BlockSpec((1, 2, tr, c), lambda i, j: (j, 0, i, 0)), pl.BlockSpec((1, tr, c), lambda i, j: (j, i, 0))],
        out_specs=[pl.BlockSpec((1, tr, c), lambda i, j: (j, i, 0)), pl.BlockSpec((tr, c), lambda i, j: (i, 0))],
        out_shape=[jax.ShapeDtypeStruct((4, r, c), g.dtype), jax.ShapeDtypeStruct((r, c), g.dtype)],
        compiler_params=_params(("parallel", "arbitrary")),
    )(g.reshape(4, 2, r, c), r1)


def _adamw(parts, w, m, v, name, tr):
    r, c = w.shape
    n = len(parts)
    slots = [slot for _, slot in parts]

    def body(*refs):
        w_ref, m_ref, v_ref = refs[n:n + 3]
        g_ref, d_ref, nm_ref, nv_ref = refs[n + 3:]
        g = None
        for slot, ref in zip(slots, refs[:n]):
            t = (ref[...] if slot is None else ref[0]).astype(F32)
            g = t if g is None else g + t
        nm = ADAM_B1 * m_ref[...] + (1.0 - ADAM_B1) * g
        nv = ADAM_B2 * v_ref[...] + (1.0 - ADAM_B2) * (g * g)
        m_hat = nm / (1.0 - ADAM_B1 ** ADAM_STEP)
        v_hat = nv / (1.0 - ADAM_B2 ** ADAM_STEP)
        g_ref[...] = g
        d_ref[...] = -ADAM_LR * (m_hat / (jnp.sqrt(v_hat) + ADAM_EPS) + ADAM_WD * w_ref[...])
        nm_ref[...] = nm
        nv_ref[...] = nv

    flat = pl.BlockSpec((tr, c), lambda i: (i, 0))
    specs = [flat if slot is None else pl.BlockSpec((1, tr, c), functools.partial(lambda i, s: (s, i, 0), s=slot))
             for _, slot in parts]
    out = jax.ShapeDtypeStruct((r, c), F32)
    return pl.pallas_call(
        body, name=name, grid=(r // tr,), in_specs=specs + [flat] * 3, out_specs=[flat] * 4, out_shape=[out] * 4,
        compiler_params=_params(("parallel",)),
    )(*[a for a, _ in parts], w, m, v)


def _adamw_whole(parts, w, m, v, name):
    n_parts = parts.shape[0]
    shape = w.shape[1:]
    zeros = (0,) * len(shape)

    def body(p_ref, w_ref, m_ref, v_ref, g_ref, d_ref, nm_ref, nv_ref):
        g = p_ref[0]
        for k in range(1, n_parts):
            g = g + p_ref[k]
        nm = ADAM_B1 * m_ref[...] + (1.0 - ADAM_B1) * g
        nv = ADAM_B2 * v_ref[...] + (1.0 - ADAM_B2) * (g * g)
        m_hat = nm / (1.0 - ADAM_B1 ** ADAM_STEP)
        v_hat = nv / (1.0 - ADAM_B2 ** ADAM_STEP)
        g_ref[...] = g
        d_ref[...] = -ADAM_LR * (m_hat / (jnp.sqrt(v_hat) + ADAM_EPS) + ADAM_WD * w_ref[...])
        nm_ref[...] = nm
        nv_ref[...] = nv

    one = pl.BlockSpec((1,) + shape, lambda l: (l,) + zeros)
    out = jax.ShapeDtypeStruct(w.shape, F32)
    return pl.pallas_call(
        body, name=name, grid=(w.shape[0],),
        in_specs=[pl.BlockSpec((n_parts, 1) + shape, lambda l: (0, l) + zeros), one, one, one],
        out_specs=[one] * 4, out_shape=[out] * 4, compiler_params=_params(("parallel",)),
    )(parts, w, m, v)


def _sum_parts(parts, name):
    n_parts = parts.shape[0]
    shape = parts.shape[2:]
    zeros = (0,) * len(shape)

    def body(p_ref, o_ref):
        g = p_ref[0]
        for k in range(1, n_parts):
            g = g + p_ref[k]
        o_ref[...] = g

    return pl.pallas_call(
        body, name=name, grid=(parts.shape[1],),
        in_specs=[pl.BlockSpec((n_parts, 1) + shape, lambda l: (0, l) + zeros)],
        out_specs=pl.BlockSpec((1,) + shape, lambda l: (l,) + zeros),
        out_shape=jax.ShapeDtypeStruct(parts.shape[1:], F32), compiler_params=_params(("parallel",)),
    )(parts)


SMALL = ("b_in", "sgu_ln_g", "sgu_ln_b", "sgu_w", "sgu_b", "gdn_a_log", "gdn_dt_bias", "gdn_norm_g", "ln1_g", "ln1_b",
         "ffn_conv_b", "ln2_g", "ln2_b")
WEIGHT_ORDER = ("w_in", "b_in", "sgu_ln_g", "sgu_ln_b", "sgu_w", "sgu_b", "gdn_conv_w", "gdn_a_log", "gdn_dt_bias",
                "gdn_norm_g", "w_proj_a", "w_proj_b", "w_proj_c", "w_out", "ln1_g", "ln1_b", "ffn_w_up", "ffn_conv_w",
                "ffn_conv_b", "ffn_w_down", "ln2_g", "ln2_b")
IN_SHARD = N_IN // N_DEV
IN_SHARD_PAD = -(-IN_SHARD // LANES) * LANES
UP_SHARD = 2 * D_FF // N_DEV
UP_SHARD_PAD = -(-UP_SHARD // LANES) * LANES

O_FOX, O_FF, O_SGU, O_GDN, O_A, O_B, O_GATE, O_GATES = 0, 3072, 3080, 5128, 8200, 8208, 8216, 9240
IN_PIECES = (("fq", O_FOX, WIDTH, BF16), ("fk", O_FOX + WIDTH, WIDTH, BF16), ("fv", O_FOX + 2 * WIDTH, WIDTH, BF16),
             ("su", O_SGU, WIDTH, F32), ("sv", O_SGU + WIDTH, WIDTH, F32),
             ("gq", O_GDN, WIDTH, F32), ("gk", O_GDN + WIDTH, WIDTH, F32), ("gv", O_GDN + 2 * WIDTH, WIDTH, F32),
             ("gg", O_GATE, WIDTH, F32),
             ("ga", O_GATES, D_MODEL, F32), ("gb", O_GATES + D_MODEL, D_MODEL, F32), ("gc", O_GATES + 2 * D_MODEL, D_MODEL, F32))
N_SMALL_COLS = 3 * N_HEADS


def _split_in(w):
    out = {n: w[..., o:o + k] for n, o, k, _ in IN_PIECES}
    sm = jnp.concatenate([w[..., O_FF:O_FF + N_HEADS], w[..., O_A:O_A + 2 * N_HEADS]], axis=-1)
    out["sm"] = jnp.pad(sm, [(0, 0)] * (w.ndim - 1) + [(0, LANES - N_SMALL_COLS)])
    return out


def _join_in(p):
    sm = p["sm"]
    return jnp.concatenate([p["fq"], p["fk"], p["fv"], sm[..., :N_HEADS], p["su"], p["sv"], p["gq"], p["gk"], p["gv"],
                            sm[..., N_HEADS:N_SMALL_COLS], p["gg"], p["ga"], p["gb"], p["gc"]], axis=-1)


def _pad_to(a, axis, n):
    pad = [(0, 0)] * a.ndim
    pad[axis] = (0, n - a.shape[axis])
    return jnp.pad(a, pad)


IN_MAIN = ((O_FOX, 3 * WIDTH, 3 * WIDTH), (O_SGU, O_A - O_SGU, O_A - O_SGU), (O_GATE, N_IN - O_GATE, N_IN - O_GATE))
N_MAIN = sum(p[2] for p in IN_MAIN)
N_RE = N_MAIN + LANES
IN_OFF = {"fq": 0, "fk": WIDTH, "fv": 2 * WIDTH, "su": 3 * WIDTH, "sv": 4 * WIDTH, "gq": 5 * WIDTH, "gk": 6 * WIDTH,
          "gv": 7 * WIDTH, "gg": 8 * WIDTH, "ga": 9 * WIDTH, "gb": 9 * WIDTH + D_MODEL, "gc": 9 * WIDTH + 2 * D_MODEL,
          "sm": N_MAIN}
IN_SEGS = ((0, 0), (O_FF, N_MAIN), (O_SGU, 3 * WIDTH), (O_A, N_MAIN + N_HEADS), (O_GATE, 8 * WIDTH))
UP_PIECES = ((0, D_FF, D_FF_PAD), (D_FF, D_FF, D_FF_PAD))
UP_SEGS = ((0, 0), (D_FF, D_FF_PAD))


def _shard_cols(g, start, n, shard_w):
    k, r = start // shard_w, start % shard_w
    assert r + n <= shard_w
    return g[k, :, :, r:r + n]


def _reorder_bias(v):
    return jnp.concatenate([v[:3 * WIDTH], v[N_MAIN:N_MAIN + N_HEADS], v[3 * WIDTH:8 * WIDTH],
                            v[N_MAIN + N_HEADS:N_MAIN + N_SMALL_COLS], v[8 * WIDTH:N_MAIN]])


def _layer_weights(wt, l, conv, small):
    w = {"re": wt["in"][l], "b_in": _split_in(small["b_in"][None, :])}
    w["pa"], w["pb"], w["pc"], w["out"] = wt["pa"][l], wt["pb"][l], wt["pc"][l], wt["out"][l]
    w["up"], w["down"] = wt["up"][l], wt["down"][l]
    gcw, fcw, fcb = conv["gdn_conv_w"], conv["ffn_conv_w"], small["ffn_conv_b"][None, :]
    w["gcq"], w["gck"], w["gcv"] = gcw[:, :WIDTH], gcw[:, WIDTH:2 * WIDTH], gcw[:, 2 * WIDTH:]
    w["fcg"], w["fcv"] = _pad_to(fcw[:, :D_FF], 1, D_FF_PAD), _pad_to(fcw[:, D_FF:], 1, D_FF_PAD)
    w["fbg"], w["fbv"] = _pad_to(fcb[:, :D_FF], 1, D_FF_PAD), _pad_to(fcb[:, D_FF:], 1, D_FF_PAD)
    for n in ("sgu_ln_g", "sgu_ln_b", "sgu_w", "sgu_b", "gdn_a_log", "gdn_dt_bias", "gdn_norm_g", "ln1_g", "ln1_b",
              "ln2_g", "ln2_b"):
        w[n] = small[n]
    return w


def _layer_fwd(x, xb, w, tag):
    s = {"x": x, "xb": xb}
    for n, _, k, dt in IN_PIECES + (("sm", 0, LANES, F32),):
        s[n] = _mm(xb, w["re"], mode="nn", name="in_" + n, bias=w["b_in"][n][0], out_dtype=dt, b_off=IN_OFF[n], b_len=k)
    s["c_col"] = _fox_prefix_fwd(s["sm"])
    s["c_row"] = _row_form(s["c_col"], ATT_T)
    s["ya"], s["ya32"], s["lse"] = _fox_fwd(s["fq"], s["fk"], s["fv"], s["c_col"], s["c_row"])
    s["yb"] = _sgu_fwd(s["su"], s["sv"], w["sgu_ln_g"], w["sgu_ln_b"], w["sgu_w"], w["sgu_b"])
    s["yc"], s["states"] = _gdn_fwd(s["gq"], s["gk"], s["gv"], s["sm"], s["gg"], w["gcq"], w["gck"], w["gcv"],
                                    w["gdn_a_log"], w["gdn_dt_bias"], w["gdn_norm_g"])
    s["merged"], s["za"], s["zb"], s["zc"] = _merge_fwd(
        [s["ya"], s["yb"], s["yc"]], [w["pa"], w["pb"], w["pc"]], [s["ga"], s["gb"], s["gc"]])
    s["mix"] = _mm(s["merged"], w["out"], mode="nn", name="out_proj")
    s["x1"], s["x1b"] = _ln_fwd(x, s["mix"], w["ln1_g"], w["ln1_b"], name="ln_fwd")
    s["hug"] = _mm(s["x1b"], w["up"], mode="nn", name="ffn_up_gate", b_off=0, b_len=D_FF_PAD)
    s["huv"] = _mm(s["x1b"], w["up"], mode="nn", name="ffn_up_val", b_off=D_FF_PAD, b_len=D_FF_PAD)
    s["act"] = _ffn_act_fwd(s["hug"], s["huv"], w["fcg"], w["fcv"], w["fbg"], w["fbv"])
    s["ffn"] = _mm(s["act"], w["down"], mode="nn", name="ffn_down")
    x2, x2b = _ln_fwd(s["x1"], s["ffn"], w["ln2_g"], w["ln2_b"], name="ln_fwd")
    return x2, x2b, s


def _layer_bwd(dx2, s, w):
    gb, gc, gs = {}, {}, {}
    ds2, ds2b, dg2, db2 = _ln_bwd(s["x1"], s["ffn"], w["ln2_g"], dx2, name="ln_bwd")
    gs["ln2_g"], gs["ln2_b"] = dg2[0], db2[0]
    gb["ffn_w_down"] = _mm(s["act"], ds2b, mode="tn", name="dw_down", out_dtype=BF16)[:D_FF]
    dact = _mm(ds2b, w["down"], mode="nt", name="dact")
    dhug, dhuv, dcwg, dcwv, dcbg, dcbv = _ffn_act_bwd(s["hug"], s["huv"], w["fcg"], w["fcv"], w["fbg"], w["fbv"], dact)
    gc["ffn_conv_w"] = jnp.concatenate([dcwg[:, :D_FF], dcwv[:, :D_FF]], axis=1)
    gs["ffn_conv_b"] = jnp.concatenate([dcbg[0, :D_FF], dcbv[0, :D_FF]])
    dhu = jnp.concatenate([dhug, dhuv], axis=1)
    gb["ffn_w_up"] = _mm(s["x1b"], dhu, mode="tn", name="dw_up", out_dtype=BF16)
    dx1 = _mm(dhu, w["up"], mode="nt", name="dx_up", add=ds2, add_scale=DEEPNORM_ALPHA)
    ds1, ds1b, dg1, db1 = _ln_bwd(s["x"], s["mix"], w["ln1_g"], dx1, name="ln_bwd")
    gs["ln1_g"], gs["ln1_b"] = dg1[0], db1[0]
    gb["w_out"] = _mm(s["merged"], ds1b, mode="tn", name="dw_out", out_dtype=BF16)
    dmerged = _mm(ds1b, w["out"], mode="nt", name="dmerged")
    dza, dzb, dzc, dga, dgb, dgc = _merge_bwd(dmerged, [s["za"], s["zb"], s["zc"]], [s["ga"], s["gb"], s["gc"]])
    gb["w_proj_a"] = _mm(s["ya"], dza, mode="tn", name="dw_proj", out_dtype=BF16)
    gb["w_proj_b"] = _mm(s["yb"], dzb, mode="tn", name="dw_proj", out_dtype=BF16)
    gb["w_proj_c"] = _mm(s["yc"], dzc, mode="tn", name="dw_proj", out_dtype=BF16)
    dya = _mm(dza, w["pa"], mode="nt", name="dy_proj_bf16", out_dtype=BF16)
    dyb = _mm(dzb, w["pb"], mode="nt", name="dy_proj")
    dyc = _mm(dzc, w["pc"], mode="nt", name="dy_proj")
    d = {"ga": dga, "gb": dgb, "gc": dgc}
    fox = (s["fq"], s["fk"], s["fv"], s["ya32"], dya, s["lse"], s["c_col"], s["c_row"])
    d["fk"], d["fv"], dc_row = _fox_bwd_kv(*fox)
    d["fq"], dc_q = _fox_bwd_q(*fox)
    dc_col = _col_form(dc_row, x_rows(s)) + jnp.pad(dc_q[:, :, 0].T, ((0, 0), (0, LANES - N_HEADS)))
    dsm = _fox_prefix_bwd(s["sm"], dc_col)
    d["su"], d["sv"], dlg, dlb, dsw, dsb = _sgu_bwd(s["su"], s["sv"], w["sgu_ln_g"], w["sgu_ln_b"], w["sgu_w"], w["sgu_b"], dyb)
    gs["sgu_ln_g"], gs["sgu_ln_b"], gs["sgu_w"], gs["sgu_b"] = dlg.reshape(-1), dlb.reshape(-1), dsw, dsb[:, :, 0]
    (d["gq"], d["gk"], d["gv"], d["gg"], dab, dwq, dwk, dwv, dal, ddt, dng) = _gdn_bwd(
        s["gq"], s["gk"], s["gv"], s["sm"], s["gg"], w["gcq"], w["gck"], w["gcv"],
        w["gdn_a_log"], w["gdn_dt_bias"], w["gdn_norm_g"], s["states"], dyc)
    gc["gdn_conv_w"] = jnp.concatenate([dwq, dwk, dwv], axis=1)
    gs["gdn_a_log"], gs["gdn_dt_bias"], gs["gdn_norm_g"] = dal[:, 0, 0], ddt[:, 0, 0], dng[0]
    dab_cols = jnp.concatenate([dab[:, :, 0].T, dab[:, :, 1].T], axis=1)
    d["sm"] = (dsm + jnp.pad(dab_cols, ((0, 0), (N_HEADS, LANES - N_SMALL_COLS)))).astype(BF16)
    dp = jnp.concatenate([d[n] for n in sorted(IN_OFF, key=IN_OFF.get)], axis=1)
    gb["w_in"] = _mm(s["xb"], dp, mode="tn", name="dw_in", out_dtype=BF16)
    gs["b_in"] = _reorder_bias(_colsum(dp, "db_in"))
    dx = _mm(dp, w["re"], mode="nt", name="dx_in", add=ds1, add_scale=DEEPNORM_ALPHA)
    return dx, gb, gc, gs


def x_rows(s):
    return s["x"].shape[0]


def _step(a):
    x = a["x"][0]
    kinds = ("grad_", "delta_", "new_m_", "new_v_")
    res = {}
    bf = lambda n: a[n].astype(BF16)
    pad_in = lambda t: _pad_to(t, 2, IN_SHARD_PAD)
    pad_up = lambda t: _pad_to(t, 2, UP_SHARD_PAD)
    g_in, g_up, g_pa, g_pb, g_pc, g_out, g_down, g_gc, g_fc = _all_gather(
        [pad_in(bf("w_in")), pad_up(bf("ffn_w_up")), bf("w_proj_a"), bf("w_proj_b"), bf("w_proj_c"), bf("w_out"),
         bf("ffn_w_down"), a["gdn_conv_w"], a["ffn_conv_w"]], "gather_weights")
    sm = jnp.concatenate([_shard_cols(g_in, O_FF, N_HEADS, IN_SHARD), _shard_cols(g_in, O_A, 2 * N_HEADS, IN_SHARD)], axis=-1)
    wt = {"in": jnp.concatenate([_shards_to_pieces(g_in, IN_MAIN, IN_SHARD, "in_to_pieces"), _pad_to(sm, 2, LANES)], axis=-1),
          "up": _shards_to_pieces(g_up, UP_PIECES, UP_SHARD, "up_to_pieces"),
          "pa": _cat_blocks(g_pa, 1), "pb": _cat_blocks(g_pb, 1), "pc": _cat_blocks(g_pc, 1),
          "out": _cat_blocks(g_out, 0), "down": _pad_to(_cat_blocks(g_down, 0), 1, D_FF_PAD)}
    conv_full = {"gdn_conv_w": g_gc.transpose(1, 2, 0, 3).reshape(DEPTH, GDN_CONV, 3 * WIDTH),
                 "ffn_conv_w": g_fc.transpose(1, 2, 0, 3).reshape(DEPTH, FFN_CONV, 2 * D_FF)}
    layers = [_layer_weights(wt, l, {n: t[l] for n, t in conv_full.items()}, {n: a[n][l] for n in SMALL})
              for l in range(DEPTH)]

    xb = x.astype(BF16)
    saved = []
    for l in range(DEPTH):
        x, xb, s = _layer_fwd(x, xb, layers[l], l)
        saved.append(s)
    dx, loss = _loss_head(x, a["loss_target"][0])
    res["loss"] = lax.psum(loss, ("x", "y", "c"))

    grads = [None] * DEPTH
    for l in reversed(range(DEPTH)):
        dx, gb, gc, gs = _layer_bwd(dx, saved[l], layers[l])
        grads[l] = {**gb, **gc, **gs}
    res["grad_x"] = dx[None]
    stacked = {n: jnp.stack([g[n] for g in grads]) for n in grads[0]}

    big = (("w_in", _pieces_to_shards(stacked["w_in"], IN_SEGS, N_MAIN, IN_SHARD, IN_SHARD_PAD, "in_to_shards"), pad_in),
           ("ffn_w_up", _pieces_to_shards(stacked["ffn_w_up"], UP_SEGS, None, UP_SHARD, UP_SHARD_PAD, "up_to_shards"), pad_up),
           ("w_proj_a", _split_blocks(stacked["w_proj_a"], 1), None), ("w_proj_b", _split_blocks(stacked["w_proj_b"], 1), None),
           ("w_proj_c", _split_blocks(stacked["w_proj_c"], 1), None), ("w_out", _split_blocks(stacked["w_out"], 0), None),
           ("ffn_w_down", _split_blocks(stacked["ffn_w_down"], 0), None))
    flat3 = lambda t: t.reshape(t.shape[0], -1, t.shape[-1])
    gs_ = [flat3(g) for _, g, _ in big]
    r1s = _pair_exchange(gs_, "rs_pair_exchange")
    sums = [_pair_sum(g, r1) for g, r1 in zip(gs_, r1s)]
    r2s = _chip_exchange([h for h, _ in sums], "rs_chip_exchange")
    for (n, _, pad), (_, own), r2 in zip(big, sums, r2s):
        prep = lambda t: (t if pad is None else pad(t)).reshape(own.shape)
        outs = _adamw([(own, None), (r2, 0), (r2, 1), (r2, 2)], prep(a[n]), prep(a["m_" + n]), prep(a["v_" + n]),
                      "adamw_" + n, _row_tile(own.shape[0], 16, max(16, (1 << 18) // own.shape[1])))
        for kind, o in zip(kinds, outs):
            o = o.reshape((DEPTH, -1, own.shape[1]))
            res[kind + n] = o if pad is None else o[:, :, :a[n].shape[2]]

    nd = lambda t: t.reshape(t.shape[0], 1, t.shape[1]) if t.ndim == 2 else t
    names = list(SMALL) + ["gdn_conv_w", "ffn_conv_w"]
    parts = dict(zip(names, _all_gather([nd(stacked[n]) for n in names], "gather_small_grads")))
    for n in SMALL:
        outs = _adamw_whole(parts[n], nd(a[n]), nd(a["m_" + n]), nd(a["v_" + n]), "adamw_" + n)
        for kind, o in zip(kinds, outs):
            res[kind + n] = o.reshape(a[n].shape)
    me = 4 * lax.axis_index("x") + 2 * lax.axis_index("y") + lax.axis_index("c")
    for n in ("gdn_conv_w", "ffn_conv_w"):
        width = a[n].shape[2]
        g_own = lax.dynamic_slice_in_dim(_sum_parts(parts[n], "sum_" + n), me * width, width, axis=2)
        outs = _adamw_whole(g_own[None], a[n], a["m_" + n], a["v_" + n], "adamw_" + n)
        for kind, o in zip(kinds, outs):
            res[kind + n] = o
    return res


INPUT_ORDER = (("x",) + WEIGHT_ORDER + ("loss_target",) + tuple("m_" + n for n in WEIGHT_ORDER)
               + tuple("v_" + n for n in WEIGHT_ORDER))
OUTPUT_ORDER = (("loss", "grad_x") + tuple(k + n for k in ("grad_", "delta_", "new_m_", "new_v_") for n in WEIGHT_ORDER))


def kernel(x, w_in, b_in, sgu_ln_g, sgu_ln_b, sgu_w, sgu_b, gdn_conv_w, gdn_a_log, gdn_dt_bias, gdn_norm_g, w_proj_a, w_proj_b, w_proj_c, w_out, ln1_g, ln1_b, ffn_w_up, ffn_conv_w, ffn_conv_b, ffn_w_down, ln2_g, ln2_b, loss_target, m_w_in, m_b_in, m_sgu_ln_g, m_sgu_ln_b, m_sgu_w, m_sgu_b, m_gdn_conv_w, m_gdn_a_log, m_gdn_dt_bias, m_gdn_norm_g, m_w_proj_a, m_w_proj_b, m_w_proj_c, m_w_out, m_ln1_g, m_ln1_b, m_ffn_w_up, m_ffn_conv_w, m_ffn_conv_b, m_ffn_w_down, m_ln2_g, m_ln2_b, v_w_in, v_b_in, v_sgu_ln_g, v_sgu_ln_b, v_sgu_w, v_sgu_b, v_gdn_conv_w, v_gdn_a_log, v_gdn_dt_bias, v_gdn_norm_g, v_w_proj_a, v_w_proj_b, v_w_proj_c, v_w_out, v_ln1_g, v_ln1_b, v_ffn_w_up, v_ffn_conv_w, v_ffn_conv_b, v_ffn_w_down, v_ln2_g, v_ln2_b):
    args = (x, w_in, b_in, sgu_ln_g, sgu_ln_b, sgu_w, sgu_b, gdn_conv_w, gdn_a_log, gdn_dt_bias, gdn_norm_g, w_proj_a, w_proj_b, w_proj_c, w_out, ln1_g, ln1_b, ffn_w_up, ffn_conv_w, ffn_conv_b, ffn_w_down, ln2_g, ln2_b, loss_target, m_w_in, m_b_in, m_sgu_ln_g, m_sgu_ln_b, m_sgu_w, m_sgu_b, m_gdn_conv_w, m_gdn_a_log, m_gdn_dt_bias, m_gdn_norm_g, m_w_proj_a, m_w_proj_b, m_w_proj_c, m_w_out, m_ln1_g, m_ln1_b, m_ffn_w_up, m_ffn_conv_w, m_ffn_conv_b, m_ffn_w_down, m_ln2_g, m_ln2_b, v_w_in, v_b_in, v_sgu_ln_g, v_sgu_ln_b, v_sgu_w, v_sgu_b, v_gdn_conv_w, v_gdn_a_log, v_gdn_dt_bias, v_gdn_norm_g, v_w_proj_a, v_w_proj_b, v_w_proj_c, v_w_out, v_ln1_g, v_ln1_b, v_ffn_w_up, v_ffn_conv_w, v_ffn_conv_b, v_ffn_w_down, v_ln2_g, v_ln2_b)
    res = _step(dict(zip(INPUT_ORDER, args)))
    return tuple(res[n] for n in OUTPUT_ORDER)
```

```python
import functools

import jax
import jax.numpy as jnp
from jax import lax
from jax.experimental import pallas as pl
from jax.experimental.pallas import tpu as pltpu

F32 = jnp.float32
BF16 = jnp.bfloat16

N_DEV = 8
DEPTH = 4
D_MODEL = 2048
HEAD_DIM = 128
N_HEADS = 8
WIDTH = N_HEADS * HEAD_DIM
CHUNK = 64
SGU_SPAN = 128
GDN_CONV = 4
FFN_CONV = 3
D_FF = 5504
D_FF_PAD = 5632
N_IN = 15384
DEEPNORM_ALPHA = (2 * DEPTH) ** 0.25
LN_EPS = 1e-5
RMS_EPS = 1e-6
ADAM_LR = 0.001
ADAM_B1 = 0.9
ADAM_B2 = 0.999
ADAM_EPS = 1e-08
ADAM_WD = 0.01
ADAM_STEP = 10

LANES = 128
SUBLANES = 8
VMEM_LIMIT = 56 * 1024 * 1024

MESH = pl.DeviceIdType.MESH


def _params(sem, vmem=VMEM_LIMIT):
    return pltpu.CompilerParams(dimension_semantics=sem, vmem_limit_bytes=vmem)


def _tile(n, pref):
    best = None
    for d in range(LANES, min(n, pref) + 1, LANES):
        if n % d == 0:
            best = d
    return n if best is None else best


def _sigmoid(x):
    return jax.nn.sigmoid(x)


def _silu(x):
    return x * jax.nn.sigmoid(x)


def _raw_dot(a, b, dims):
    return lax.dot_general(a.astype(BF16), b.astype(BF16), (dims, ((), ())), preferred_element_type=F32)


@jax.custom_vjp
def _bdot(a, b):
    return _raw_dot(a, b, ((1,), (0,)))


def _bdot_fwd(a, b):
    return _bdot(a, b), (a, b)


def _bdot_bwd(res, ct):
    a, b = res
    return _raw_dot(ct, b, ((1,), (1,))), _raw_dot(a, ct, ((0,), (0,)))


_bdot.defvjp(_bdot_fwd, _bdot_bwd)


@jax.custom_vjp
def _bdot_nt(a, b):
    return _raw_dot(a, b, ((1,), (1,)))


def _bdot_nt_fwd(a, b):
    return _bdot_nt(a, b), (a, b)


def _bdot_nt_bwd(res, ct):
    a, b = res
    return _raw_dot(ct, b, ((1,), (0,))), _raw_dot(ct, a, ((0,), (0,)))


_bdot_nt.defvjp(_bdot_nt_fwd, _bdot_nt_bwd)


@jax.custom_vjp
def _bdot_tn(a, b):
    return _raw_dot(a, b, ((0,), (0,)))


def _bdot_tn_fwd(a, b):
    return _bdot_tn(a, b), (a, b)


def _bdot_tn_bwd(res, ct):
    a, b = res
    return _raw_dot(b, ct, ((1,), (1,))), _raw_dot(a, ct, ((1,), (0,)))


_bdot_tn.defvjp(_bdot_tn_fwd, _bdot_tn_bwd)


def _split3(a):
    hi = a.astype(BF16)
    r = a - hi.astype(F32)
    lo = r.astype(BF16)
    return hi, lo


def _dot3_raw(a, b, dims):
    ah, al = _split3(a)
    bh, bl = _split3(b)
    d = functools.partial(lax.dot_general, dimension_numbers=(dims, ((), ())), preferred_element_type=F32)
    return d(ah, bh) + (d(ah, bl) + d(al, bh))


@jax.custom_vjp
def _dot3(a, b):
    return _dot3_raw(a, b, ((1,), (0,)))


def _dot3_fwd(a, b):
    return _dot3(a, b), (a, b)


def _dot3_bwd(res, ct):
    a, b = res
    return _dot3_raw(ct, b, ((1,), (1,))), _dot3_raw(a, ct, ((0,), (0,)))


_dot3.defvjp(_dot3_fwd, _dot3_bwd)


def _mm(a, b, *, mode, name, bias=None, add=None, add_scale=1.0, out_dtype=F32, b_off=0, b_len=None):
    if mode == "nn":
        (m, k), (k2, n) = a.shape, (b.shape[0], b_len or b.shape[1])
    elif mode == "nt":
        (m, k), (n, k2) = a.shape, (b.shape[0], b_len or b.shape[1])
    else:
        (k, m), (k2, n) = a.shape, b.shape
        assert b_off == 0 and b_len is None
    assert k == k2, (a.shape, b.shape, mode)
    tn = _tile(n, 1408)
    tm = _tile(m, 1024 if tn >= 1024 else 2048)
    tk = _tile(k, 512) if k % 512 == 0 else _tile(k, 1408)
    nk = k // tk
    on, ok = (b_off // tn, 0) if mode == "nn" else (0, b_off // tk)
    assert b_off == (on * tn if mode == "nn" else ok * tk), (b_off, tn, tk)
    dims = {"nn": ((1,), (0,)), "nt": ((1,), (1,)), "tn": ((0,), (0,))}[mode]
    a_spec = pl.BlockSpec((tk, tm), lambda i, j, kk: (kk, i)) if mode == "tn" else pl.BlockSpec((tm, tk), lambda i, j, kk: (i, kk))
    b_spec = (pl.BlockSpec((tn, tk), lambda i, j, kk: (j, kk + ok)) if mode == "nt"
              else pl.BlockSpec((tk, tn), lambda i, j, kk: (kk, j + on)))
    in_specs = [a_spec, b_spec]
    operands = [a, b]
    if bias is not None:
        in_specs.append(pl.BlockSpec((1, tn), lambda i, j, kk: (0, j)))
        operands.append(bias.reshape(1, n))
    if add is not None:
        in_specs.append(pl.BlockSpec((tm, tn), lambda i, j, kk: (i, j)))
        operands.append(add)
    has_bias, has_add = bias is not None, add is not None

    def body(*refs):
        a_ref, b_ref = refs[0], refs[1]
        pos = 2
        bias_ref = add_ref = None
        if has_bias:
            bias_ref = refs[pos]
            pos += 1
        if has_add:
            add_ref = refs[pos]
            pos += 1
        o_ref, acc_ref = refs[pos], refs[pos + 1]
        kk = pl.program_id(2)

        @pl.when(kk == 0)
        def _():
            acc_ref[...] = jnp.zeros_like(acc_ref)

        acc_ref[...] += _raw_dot(a_ref[...], b_ref[...], dims)

        @pl.when(kk == nk - 1)
        def _():
            r = acc_ref[...]
            if has_bias:
                r = r + bias_ref[...]
            if has_add:
                r = r + add_scale * add_ref[...].astype(F32)
            o_ref[...] = r.astype(out_dtype)

    return pl.pallas_call(
        body, name=name, grid=(m // tm, n // tn, nk),
        in_specs=in_specs, out_specs=pl.BlockSpec((tm, tn), lambda i, j, kk: (i, j)),
        out_shape=jax.ShapeDtypeStruct((m, n), out_dtype),
        scratch_shapes=[pltpu.VMEM((tm, tn), F32)],
        compiler_params=_params(("parallel", "parallel", "arbitrary")),
    )(*operands)


def _ln_fwd(x, r, g, b, *, name):
    m, d = x.shape
    tm = 256

    def body(x_ref, r_ref, g_ref, b_ref, y_ref, yb_ref):
        s = DEEPNORM_ALPHA * x_ref[...] + r_ref[...]
        mu = jnp.mean(s, axis=-1, keepdims=True)
        c = s - mu
        var = jnp.mean(c * c, axis=-1, keepdims=True)
        y = c * lax.rsqrt(var + LN_EPS) * g_ref[...] + b_ref[...]
        y_ref[...] = y
        yb_ref[...] = y.astype(BF16)

    row = pl.BlockSpec((tm, d), lambda i: (i, 0))
    vec = pl.BlockSpec((1, d), lambda i: (0, 0))
    return pl.pallas_call(
        body, name=name, grid=(m // tm,), in_specs=[row, row, vec, vec], out_specs=[row, row],
        out_shape=[jax.ShapeDtypeStruct((m, d), F32), jax.ShapeDtypeStruct((m, d), BF16)],
        compiler_params=_params(("parallel",)),
    )(x, r, g.reshape(1, d), b.reshape(1, d))


def _ln_bwd(x, r, g, dy, *, name):
    m, d = x.shape
    tm = 256

    def body(x_ref, r_ref, g_ref, dy_ref, ds_ref, dsb_ref, dg_ref, db_ref):
        i = pl.program_id(0)
        s = DEEPNORM_ALPHA * x_ref[...] + r_ref[...]
        mu = jnp.mean(s, axis=-1, keepdims=True)
        c = s - mu
        var = jnp.mean(c * c, axis=-1, keepdims=True)
        rstd = lax.rsqrt(var + LN_EPS)
        xhat = c * rstd
        dy_ = dy_ref[...]
        dxhat = dy_ * g_ref[...]
        m1 = jnp.mean(dxhat, axis=-1, keepdims=True)
        m2 = jnp.mean(dxhat * xhat, axis=-1, keepdims=True)
        ds = rstd * (dxhat - m1 - xhat * m2)
        ds_ref[...] = ds
        dsb_ref[...] = ds.astype(BF16)

        @pl.when(i == 0)
        def _():
            dg_ref[...] = jnp.zeros_like(dg_ref)
            db_ref[...] = jnp.zeros_like(db_ref)

        dg_ref[...] += jnp.sum(dy_ * xhat, axis=0, keepdims=True)
        db_ref[...] += jnp.sum(dy_, axis=0, keepdims=True)

    row = pl.BlockSpec((tm, d), lambda i: (i, 0))
    vec = pl.BlockSpec((1, d), lambda i: (0, 0))
    return pl.pallas_call(
        body, name=name, grid=(m // tm,), in_specs=[row, row, vec, row], out_specs=[row, row, vec, vec],
        out_shape=[jax.ShapeDtypeStruct((m, d), F32), jax.ShapeDtypeStruct((m, d), BF16),
                   jax.ShapeDtypeStruct((1, d), F32), jax.ShapeDtypeStruct((1, d), F32)],
        compiler_params=_params(("arbitrary",)),
    )(x, r, g.reshape(1, d), dy)


def _loss_head(y, t):
    m, d = y.shape
    tm = 256

    def body(y_ref, t_ref, dy_ref, l_ref):
        i = pl.program_id(0)
        e = y_ref[...] - t_ref[...]
        dy_ref[...] = e * (1.0 / d)

        @pl.when(i == 0)
        def _():
            l_ref[...] = jnp.zeros_like(l_ref)

        part = jnp.sum(jnp.sum(e * e, axis=-1, keepdims=True) * (0.5 / d), axis=0, keepdims=True)
        l_ref[...] += jnp.broadcast_to(part, l_ref.shape)

    row = pl.BlockSpec((tm, d), lambda i: (i, 0))
    dy, l = pl.pallas_call(
        body, name="loss_head", grid=(m // tm,), in_specs=[row, row],
        out_specs=[row, pl.BlockSpec((SUBLANES, LANES), lambda i: (0, 0))],
        out_shape=[jax.ShapeDtypeStruct((m, d), F32), jax.ShapeDtypeStruct((SUBLANES, LANES), F32)],
        compiler_params=_params(("arbitrary",)),
    )(y, t)
    return dy, l[0, 0]


def _conv_ext(ext, w, width):
    acc = None
    for k in range(width):
        s = width - 1 - k
        xs = ext if s == 0 else pltpu.roll(ext, s, 0)
        term = xs[SUBLANES:, :] * w[k:k + 1, :]
        acc = term if acc is None else acc + term
    return acc


def _ffn_act_fwd(hug, huv, wg, wv, bg, bv):
    m, fp = hug.shape
    ts, tc = 512, 512
    hb = ts // SUBLANES

    def body(g_ref, gh_ref, v_ref, vh_ref, wg_ref, wv_ref, bg_ref, bv_ref, o_ref):
        i = pl.program_id(1)
        keep = (i > 0).astype(F32)
        hg = _conv_ext(jnp.concatenate([gh_ref[...] * keep, g_ref[...]], axis=0), wg_ref[...], FFN_CONV) + bg_ref[...]
        hv = _conv_ext(jnp.concatenate([vh_ref[...] * keep, v_ref[...]], axis=0), wv_ref[...], FFN_CONV) + bv_ref[...]
        o_ref[...] = (_silu(hg) * hv).astype(BF16)

    tile = pl.BlockSpec((ts, tc), lambda j, i: (i, j))
    halo = pl.BlockSpec((SUBLANES, tc), lambda j, i: (jnp.maximum(i * hb - 1, 0), j))
    wsp = pl.BlockSpec((FFN_CONV, tc), lambda j, i: (0, j))
    bsp = pl.BlockSpec((1, tc), lambda j, i: (0, j))
    return pl.pallas_call(
        body, name="ffn_act_fwd", grid=(fp // tc, m // ts),
        in_specs=[tile, halo, tile, halo, wsp, wsp, bsp, bsp], out_specs=tile,
        out_shape=jax.ShapeDtypeStruct((m, fp), BF16),
        compiler_params=_params(("parallel", "parallel")),
    )(hug, hug, huv, huv, wg, wv, bg, bv)


def _ffn_act_bwd(hug, huv, wg, wv, bg, bv, dact):
    m, fp = hug.shape
    ts, tc = 512, 512
    hb = ts // SUBLANES
    ni = m // ts

    def body(g_ref, gp_ref, gn_ref, v_ref, vp_ref, vn_ref, wg_ref, wv_ref, bg_ref, bv_ref, da_ref, dan_ref,
             dg_ref, dv_ref, dwg_ref, dwv_ref, dbg_ref, dbv_ref):
        i = pl.program_id(1)
        first = (i > 0).astype(F32)
        last = (i < ni - 1).astype(F32)
        wg_, wv_ = wg_ref[...], wv_ref[...]
        ext_g = jnp.concatenate([gp_ref[...] * first, g_ref[...], gn_ref[...]], axis=0)
        ext_v = jnp.concatenate([vp_ref[...] * first, v_ref[...], vn_ref[...]], axis=0)
        hg = _conv_ext(ext_g, wg_, FFN_CONV) + bg_ref[...]
        hv = _conv_ext(ext_v, wv_, FFN_CONV) + bv_ref[...]
        da = jnp.concatenate([da_ref[...], dan_ref[...] * last], axis=0)
        sg = _sigmoid(hg)
        dhg = da * hv * (sg * (1.0 + hg * (1.0 - sg)))
        dhv = da * (hg * sg)
        n_ext = ts + SUBLANES

        def back(dh, w_, ext):
            dx = None
            dw_rows = []
            for k in range(FFN_CONV):
                s = FFN_CONV - 1 - k
                sh = dh if s == 0 else pltpu.roll(dh, n_ext - s, 0)
                term = sh[:ts, :] * w_[k:k + 1, :]
                dx = term if dx is None else dx + term
                xs = ext if s == 0 else pltpu.roll(ext, s, 0)
                dw_rows.append(jnp.sum(dh[:ts, :] * xs[SUBLANES:SUBLANES + ts, :], axis=0, keepdims=True))
            return dx, jnp.concatenate(dw_rows, axis=0), jnp.sum(dh[:ts, :], axis=0, keepdims=True)

        dxg, dwg, dbg = back(dhg, wg_, ext_g)
        dxv, dwv, dbv = back(dhv, wv_, ext_v)
        dg_ref[...] = dxg.astype(BF16)
        dv_ref[...] = dxv.astype(BF16)

        @pl.when(i == 0)
        def _():
            dwg_ref[...] = jnp.zeros_like(dwg_ref)
            dwv_ref[...] = jnp.zeros_like(dwv_ref)
            dbg_ref[...] = jnp.zeros_like(dbg_ref)
            dbv_ref[...] = jnp.zeros_like(dbv_ref)

        dwg_ref[...] += dwg
        dwv_ref[...] += dwv
        dbg_ref[...] += dbg
        dbv_ref[...] += dbv

    tile = pl.BlockSpec((ts, tc), lambda j, i: (i, j))
    prev = pl.BlockSpec((SUBLANES, tc), lambda j, i: (jnp.maximum(i * hb - 1, 0), j))
    nxt = pl.BlockSpec((SUBLANES, tc), lambda j, i: (jnp.minimum((i + 1) * hb, m // SUBLANES - 1), j))
    wsp = pl.BlockSpec((FFN_CONV, tc), lambda j, i: (0, j))
    bsp = pl.BlockSpec((1, tc), lambda j, i: (0, j))
    return pl.pallas_call(
        body, name="ffn_act_bwd", grid=(fp // tc, ni),
        in_specs=[tile, prev, nxt, tile, prev, nxt, wsp, wsp, bsp, bsp, tile, nxt],
        out_specs=[tile, tile, wsp, wsp, bsp, bsp],
        out_shape=[jax.ShapeDtypeStruct((m, fp), BF16), jax.ShapeDtypeStruct((m, fp), BF16),
                   jax.ShapeDtypeStruct((FFN_CONV, fp), F32), jax.ShapeDtypeStruct((FFN_CONV, fp), F32),
                   jax.ShapeDtypeStruct((1, fp), F32), jax.ShapeDtypeStruct((1, fp), F32)],
        compiler_params=_params(("parallel", "arbitrary")),
    )(hug, hug, hug, huv, huv, huv, wg, wv, bg, bv, dact, dact)


def _log1p_small(e):
    u = 1.0 + e
    safe = jnp.where(u == 1.0, 1.0, u - 1.0)
    return jnp.where(u == 1.0, e, jnp.log(u) * (e / safe))


def _log_sigmoid(x):
    return jnp.minimum(x, 0.0) - _log1p_small(jnp.exp(-jnp.abs(x)))


def _exact_tri_dot(tri, x):
    p1 = x.astype(BF16)
    r1 = x - p1.astype(F32)
    p2 = r1.astype(BF16)
    p3 = (r1 - p2.astype(F32)).astype(BF16)
    d = lambda p: lax.dot_general(tri, p, (((1,), (0,)), ((), ())), preferred_element_type=F32)
    return d(p1) + (d(p2) + d(p3))


def _fox_prefix_fwd(small):
    m, w = small.shape
    tb = 512
    nb = m // tb

    def body(f_ref, c_ref, carry_ref):
        i = pl.program_id(0)

        @pl.when(i == 0)
        def _():
            carry_ref[...] = jnp.zeros_like(carry_ref)

        lf = _log_sigmoid(f_ref[...])
        r = lax.broadcasted_iota(jnp.int32, (tb, tb), 0)
        c = lax.broadcasted_iota(jnp.int32, (tb, tb), 1)
        tri = (c <= r).astype(BF16)
        c_ref[...] = _exact_tri_dot(tri, lf) + carry_ref[0:1, :]
        carry_ref[0:1, :] += jnp.sum(lf, axis=0, keepdims=True)

    blk = pl.BlockSpec((tb, w), lambda i: (i, 0))
    return pl.pallas_call(
        body, name="fox_prefix_fwd", grid=(nb,), in_specs=[blk], out_specs=blk,
        out_shape=jax.ShapeDtypeStruct((m, w), F32), scratch_shapes=[pltpu.VMEM((SUBLANES, w), F32)],
        compiler_params=_params(("arbitrary",)),
    )(small)


def _fox_prefix_bwd(small, dc):
    m, w = small.shape
    tb = 512
    nb = m // tb

    def body(f_ref, dc_ref, df_ref, carry_ref):
        i = pl.program_id(0)

        @pl.when(i == 0)
        def _():
            carry_ref[...] = jnp.zeros_like(carry_ref)

        d = dc_ref[...]
        r = lax.broadcasted_iota(jnp.int32, (tb, tb), 0)
        c = lax.broadcasted_iota(jnp.int32, (tb, tb), 1)
        tri = (c >= r).astype(BF16)
        dlf = _exact_tri_dot(tri, d) + carry_ref[0:1, :]
        carry_ref[0:1, :] += jnp.sum(d, axis=0, keepdims=True)
        lane = lax.broadcasted_iota(jnp.int32, (tb, w), 1)
        df_ref[...] = jnp.where(lane < N_HEADS, dlf * _sigmoid(-f_ref[...]), 0.0)

    blk = pl.BlockSpec((tb, w), lambda i: (nb - 1 - i, 0))
    return pl.pallas_call(
        body, name="fox_prefix_bwd", grid=(nb,), in_specs=[blk, blk], out_specs=blk,
        out_shape=jax.ShapeDtypeStruct((m, w), F32), scratch_shapes=[pltpu.VMEM((SUBLANES, w), F32)],
        compiler_params=_params(("arbitrary",)),
    )(small, dc)


ATT_T = 512
NEG = -1e30


def _lane_pick(block, h):
    lane = lax.broadcasted_iota(jnp.int32, block.shape, 1)
    return jnp.sum(jnp.where(lane == h, block, 0.0), axis=1, keepdims=True)


def _att_scores(q, k, cq, ck):
    return _raw_dot(q, k, ((1,), (1,))) * (HEAD_DIM ** -0.5) + cq - ck


def _att_probs(s, lse, diag):
    p = jnp.exp(s - lse)
    if not diag:
        return p
    t = s.shape[0]
    return jnp.where(lax.broadcasted_iota(jnp.int32, (t, t), 0) >= lax.broadcasted_iota(jnp.int32, (t, t), 1), p, 0.0)


def _fox_fwd(q, k, v, c_col, c_row):
    m = q.shape[0]
    t = ATT_T
    nb = m // t

    def body(q_ref, k_ref, v_ref, cc_ref, cr_ref, o_ref, o32_ref, lse_ref):
        h, i = pl.program_id(0), pl.program_id(1)
        q = q_ref[...]
        cq = _lane_pick(cc_ref[...], h)

        def step(j, carry, diag):
            mx, l, acc = carry
            rows = pl.ds(pl.multiple_of(j * t, t), t)
            s = _att_scores(q, k_ref[rows, :], cq, cr_ref[h * nb + j])
            if diag:
                s = jnp.where(lax.broadcasted_iota(jnp.int32, (t, t), 0) >= lax.broadcasted_iota(jnp.int32, (t, t), 1), s, NEG)
            mn = jnp.maximum(mx, jnp.max(s, axis=1, keepdims=True))
            p = jnp.exp(s - mn)
            a = jnp.exp(mx - mn)
            return mn, a * l + jnp.sum(p, axis=1, keepdims=True), a * acc + _raw_dot(p, v_ref[rows, :], ((1,), (0,)))

        init = (jnp.full((t, 1), NEG, F32), jnp.zeros((t, 1), F32), jnp.zeros((t, HEAD_DIM), F32))
        mx, l, acc = step(i, lax.fori_loop(0, i, functools.partial(step, diag=False), init), True)
        o = acc / l
        o_ref[...] = o.astype(BF16)
        o32_ref[...] = o
        lse_ref[0] = jnp.broadcast_to(mx + jnp.log(l), (t, LANES))

    return pl.pallas_call(
        body, name="fox_fwd", grid=(N_HEADS, nb),
        in_specs=[pl.BlockSpec((t, HEAD_DIM), lambda h, i: (i, h)),
                  pl.BlockSpec((m, HEAD_DIM), lambda h, i: (0, h)),
                  pl.BlockSpec((m, HEAD_DIM), lambda h, i: (0, h)),
                  pl.BlockSpec((t, LANES), lambda h, i: (i, 0)),
                  pl.BlockSpec((N_HEADS * nb, 1, t), lambda h, i: (0, 0, 0))],
        out_specs=[pl.BlockSpec((t, HEAD_DIM), lambda h, i: (i, h)), pl.BlockSpec((t, HEAD_DIM), lambda h, i: (i, h)),
                   pl.BlockSpec((1, t, LANES), lambda h, i: (h, i, 0))],
        out_shape=[jax.ShapeDtypeStruct((m, WIDTH), BF16), jax.ShapeDtypeStruct((m, WIDTH), F32),
                   jax.ShapeDtypeStruct((N_HEADS, m, LANES), F32)],
        compiler_params=_params(("parallel", "parallel")),
    )(q, k, v, c_col, c_row)


def _fox_bwd_kv(q, k, v, o, do, lse, c_col, c_row):
    m = q.shape[0]
    t = ATT_T
    nb = m // t

    def body(q_ref, k_ref, v_ref, o_ref, do_ref, lse_ref, cc_ref, cr_ref, dk_ref, dv_ref, dc_ref):
        h, j = pl.program_id(0), pl.program_id(1)
        k, v = k_ref[...], v_ref[...]
        ck = cr_ref[h * nb + j]

        def step(i, carry, diag):
            dk, dv, dc = carry
            rows = pl.ds(pl.multiple_of(i * t, t), t)
            q, do_ = q_ref[rows, :], do_ref[rows, :]
            cq = _lane_pick(cc_ref[rows, :], h)
            lse_i = jnp.max(lse_ref[0, rows, :], axis=1, keepdims=True)
            dd = jnp.sum(do_.astype(F32) * o_ref[rows, :].astype(F32), axis=1, keepdims=True)
            p = _att_probs(_att_scores(q, k, cq, ck), lse_i, diag)
            dp = _raw_dot(do_, v, ((1,), (1,)))
            ds = p * (dp - dd)
            return (dk + _raw_dot(ds, q, ((0,), (0,))), dv + _raw_dot(p, do_, ((0,), (0,))),
                    dc - jnp.sum(ds, axis=0, keepdims=True))

        init = (jnp.zeros((t, HEAD_DIM), F32), jnp.zeros((t, HEAD_DIM), F32), jnp.zeros((1, t), F32))
        dk, dv, dc = lax.fori_loop(j + 1, nb, functools.partial(step, diag=False), step(j, init, True))
        dk_ref[...] = (dk * (HEAD_DIM ** -0.5)).astype(BF16)
        dv_ref[...] = dv.astype(BF16)
        dc_ref[0] = dc

    whole = lambda col: pl.BlockSpec((m, HEAD_DIM), col)
    return pl.pallas_call(
        body, name="fox_bwd_kv", grid=(N_HEADS, nb),
        in_specs=[whole(lambda h, j: (0, h)),
                  pl.BlockSpec((t, HEAD_DIM), lambda h, j: (j, h)),
                  pl.BlockSpec((t, HEAD_DIM), lambda h, j: (j, h)),
                  whole(lambda h, j: (0, h)), whole(lambda h, j: (0, h)),
                  pl.BlockSpec((1, m, LANES), lambda h, j: (h, 0, 0)),
                  pl.BlockSpec((m, LANES), lambda h, j: (0, 0)),
                  pl.BlockSpec((N_HEADS * nb, 1, t), lambda h, j: (0, 0, 0))],
        out_specs=[pl.BlockSpec((t, HEAD_DIM), lambda h, j: (j, h)),
                   pl.BlockSpec((t, HEAD_DIM), lambda h, j: (j, h)),
                   pl.BlockSpec((1, 1, t), lambda h, j: (h * nb + j, 0, 0))],
        out_shape=[jax.ShapeDtypeStruct((m, WIDTH), BF16), jax.ShapeDtypeStruct((m, WIDTH), BF16),
                   jax.ShapeDtypeStruct((N_HEADS * nb, 1, t), F32)],
        compiler_params=_params(("parallel", "parallel")),
    )(q, k, v, o, do, lse, c_col, c_row)


def _fox_bwd_q(q, k, v, o, do, lse, c_col, c_row):
    m = q.shape[0]
    t = ATT_T
    nb = m // t

    def body(q_ref, k_ref, v_ref, o_ref, do_ref, lse_ref, cc_ref, cr_ref, dq_ref, dcq_ref):
        h, i = pl.program_id(0), pl.program_id(1)
        q, do_ = q_ref[...], do_ref[...]
        cq = _lane_pick(cc_ref[...], h)
        lse_i = jnp.max(lse_ref[0], axis=1, keepdims=True)
        dd = jnp.sum(do_.astype(F32) * o_ref[...].astype(F32), axis=1, keepdims=True)

        def step(j, carry, diag):
            dq, dcq = carry
            rows = pl.ds(pl.multiple_of(j * t, t), t)
            k = k_ref[rows, :]
            p = _att_probs(_att_scores(q, k, cq, cr_ref[h * nb + j]), lse_i, diag)
            ds = p * (_raw_dot(do_, v_ref[rows, :], ((1,), (1,))) - dd)
            return dq + _raw_dot(ds, k, ((1,), (0,))), dcq + jnp.sum(ds, axis=1, keepdims=True)

        init = (jnp.zeros((t, HEAD_DIM), F32), jnp.zeros((t, 1), F32))
        dq, dcq = step(i, lax.fori_loop(0, i, functools.partial(step, diag=False), init), True)
        dq_ref[...] = (dq * (HEAD_DIM ** -0.5)).astype(BF16)
        dcq_ref[0] = jnp.broadcast_to(dcq, (t, LANES))

    tile = lambda col: pl.BlockSpec((t, HEAD_DIM), col)
    return pl.pallas_call(
        body, name="fox_bwd_q", grid=(N_HEADS, nb),
        in_specs=[tile(lambda h, i: (i, h)),
                  pl.BlockSpec((m, HEAD_DIM), lambda h, i: (0, h)),
                  pl.BlockSpec((m, HEAD_DIM), lambda h, i: (0, h)),
                  tile(lambda h, i: (i, h)), tile(lambda h, i: (i, h)),
                  pl.BlockSpec((1, t, LANES), lambda h, i: (h, i, 0)),
                  pl.BlockSpec((t, LANES), lambda h, i: (i, 0)),
                  pl.BlockSpec((N_HEADS * nb, 1, t), lambda h, i: (0, 0, 0))],
        out_specs=[tile(lambda h, i: (i, h)), pl.BlockSpec((1, t, LANES), lambda h, i: (h, i, 0))],
        out_shape=[jax.ShapeDtypeStruct((m, WIDTH), BF16), jax.ShapeDtypeStruct((N_HEADS, m, LANES), F32)],
        compiler_params=_params(("parallel", "parallel")),
    )(q, k, v, o, do, lse, c_col, c_row)


def _row_form(c_col, t):
    m = c_col.shape[0]
    return c_col[:, :N_HEADS].T.reshape(N_HEADS * (m // t), 1, t)


def _col_form(c_row, m):
    c = c_row.reshape(N_HEADS, m).T
    return jnp.pad(c, ((0, 0), (0, LANES - N_HEADS)))


SGU_T = 512


def _sgu_tile(u, v, lg, lb, w, bsb):
    r = lax.broadcasted_iota(jnp.int32, (SGU_SPAN, SGU_SPAN), 0) // CHUNK
    c = lax.broadcasted_iota(jnp.int32, (SGU_SPAN, SGU_SPAN), 1) // CHUNK
    wm = jnp.where(r >= c, w, 0.0)
    outs = []
    for n in range(u.shape[0] // SGU_SPAN):
        rows = slice(n * SGU_SPAN, (n + 1) * SGU_SPAN)
        vs = v[rows]
        mu = jnp.mean(vs, axis=-1, keepdims=True)
        d = vs - mu
        var = jnp.mean(d * d, axis=-1, keepdims=True)
        vg = d * lax.rsqrt(var + LN_EPS) * lg + lb
        outs.append(u[rows] * (_bdot(wm, vg) + bsb))
    return jnp.concatenate(outs, axis=0)


def _sgu_specs(m):
    t = SGU_T
    tile = lambda off: pl.BlockSpec((t, HEAD_DIM), lambda g, i: (i, off + g))
    vec = pl.BlockSpec((1, 1, HEAD_DIM), lambda g, i: (g, 0, 0))
    mat = pl.BlockSpec((1, SGU_SPAN, SGU_SPAN), lambda g, i: (g, 0, 0))
    return tile, vec, mat


def _sgu_operands(ln_g, ln_b, w, b):
    return (ln_g.reshape(N_HEADS, 1, HEAD_DIM), ln_b.reshape(N_HEADS, 1, HEAD_DIM), w,
            jnp.broadcast_to(b[:, :, None], (N_HEADS, SGU_SPAN, SGU_SPAN)))


def _sgu_fwd(u, v, ln_g, ln_b, w, b):
    m = u.shape[0]
    tile, vec, mat = _sgu_specs(m)

    def body(u_ref, v_ref, lg_ref, lb_ref, w_ref, b_ref, o_ref):
        o_ref[...] = _sgu_tile(u_ref[...], v_ref[...], lg_ref[0], lb_ref[0], w_ref[0], b_ref[0]).astype(BF16)

    return pl.pallas_call(
        body, name="sgu_fwd", grid=(N_HEADS, m // SGU_T),
        in_specs=[tile(0), tile(0), vec, vec, mat, mat], out_specs=tile(0),
        out_shape=jax.ShapeDtypeStruct((m, WIDTH), BF16),
        compiler_params=_params(("parallel", "parallel")),
    )(u, v, *_sgu_operands(ln_g, ln_b, w, b))


def _sgu_bwd(u, v, ln_g, ln_b, w, b, dy):
    m = u.shape[0]
    tile, vec, mat = _sgu_specs(m)

    def body(u_ref, v_ref, lg_ref, lb_ref, w_ref, b_ref, dy_ref, du_ref, dv_ref, dlg_ref, dlb_ref, dw_ref, db_ref):
        i = pl.program_id(1)
        _, vjp = jax.vjp(_sgu_tile, u_ref[...], v_ref[...], lg_ref[0], lb_ref[0], w_ref[0], b_ref[0])
        du, dv, dlg, dlb, dw, dbsb = vjp(dy_ref[...])
        du_ref[...] = du.astype(BF16)
        dv_ref[...] = dv.astype(BF16)

        @pl.when(i == 0)
        def _():
            dlg_ref[...] = jnp.zeros_like(dlg_ref)
            dlb_ref[...] = jnp.zeros_like(dlb_ref)
            dw_ref[...] = jnp.zeros_like(dw_ref)
            db_ref[...] = jnp.zeros_like(db_ref)

        dlg_ref[0] += dlg
        dlb_ref[0] += dlb
        dw_ref[0] += dw
        db_ref[0] += jnp.broadcast_to(jnp.sum(dbsb, axis=1, keepdims=True), (SGU_SPAN, SGU_SPAN))

    vshape = jax.ShapeDtypeStruct((N_HEADS, 1, HEAD_DIM), F32)
    mshape = jax.ShapeDtypeStruct((N_HEADS, SGU_SPAN, SGU_SPAN), F32)
    return pl.pallas_call(
        body, name="sgu_bwd", grid=(N_HEADS, m // SGU_T),
        in_specs=[tile(0), tile(0), vec, vec, mat, mat, tile(0)],
        out_specs=[tile(0), tile(0), vec, vec, mat, mat],
        out_shape=[jax.ShapeDtypeStruct((m, WIDTH), BF16), jax.ShapeDtypeStruct((m, WIDTH), BF16),
                   vshape, vshape, mshape, mshape],
        compiler_params=_params(("parallel", "arbitrary")),
    )(u, v, *_sgu_operands(ln_g, ln_b, w, b), dy)


def _merge_fwd(ys, ws, gates):
    m, d = gates[0].shape
    tm, tn = 512, _tile(d, 512)

    def body(ya, yb, yc, wa, wb, wc, ga, gb, gc, o_ref, za, zb, zc):
        acc = None
        for y_ref, w_ref, g_ref, z_ref in ((ya, wa, ga, za), (yb, wb, gb, zb), (yc, wc, gc, zc)):
            z = _raw_dot(y_ref[...], w_ref[...], ((1,), (0,)))
            z_ref[...] = z
            term = _sigmoid(g_ref[...]) * z
            acc = term if acc is None else acc + term
        o_ref[...] = acc.astype(BF16)

    ysp = pl.BlockSpec((tm, WIDTH), lambda i, j: (i, 0))
    wsp = pl.BlockSpec((WIDTH, tn), lambda i, j: (0, j))
    tsp = pl.BlockSpec((tm, tn), lambda i, j: (i, j))
    zs = jax.ShapeDtypeStruct((m, d), F32)
    return pl.pallas_call(
        body, name="merge_fwd", grid=(m // tm, d // tn),
        in_specs=[ysp] * 3 + [wsp] * 3 + [tsp] * 3, out_specs=[tsp] * 4,
        out_shape=[jax.ShapeDtypeStruct((m, d), BF16), zs, zs, zs],
        compiler_params=_params(("parallel", "parallel")),
    )(*ys, *ws, *gates)


def _merge_bwd(dmerged, zs, gates):
    m, d = dmerged.shape
    tm, tn = 512, _tile(d, 1024)

    def body(dm_ref, za, zb, zc, ga, gb, gc, dza, dzb, dzc, dga, dgb, dgc):
        dm = dm_ref[...]
        for z_ref, g_ref, dz_ref, dg_ref in ((za, ga, dza, dga), (zb, gb, dzb, dgb), (zc, gc, dzc, dgc)):
            s = _sigmoid(g_ref[...])
            dz_ref[...] = (dm * s).astype(BF16)
            dg_ref[...] = (dm * z_ref[...] * (s * (1.0 - s))).astype(BF16)

    tsp = pl.BlockSpec((tm, tn), lambda i, j: (i, j))
    bs = jax.ShapeDtypeStruct((m, d), BF16)
    return pl.pallas_call(
        body, name="merge_bwd", grid=(m // tm, d // tn),
        in_specs=[tsp] * 7, out_specs=[tsp] * 6, out_shape=[bs] * 6,
        compiler_params=_params(("parallel", "parallel")),
    )(dmerged, *zs, *gates)


GDN_T = 512
GDN_A_LANE = N_HEADS
GDN_B_LANE = 2 * N_HEADS


@functools.partial(jax.custom_vjp, nondiff_argnums=(2,))
def _conv_halo(ext, w, width):
    return _conv_ext(ext, w, width)


def _conv_halo_fwd(ext, w, width):
    return _conv_ext(ext, w, width), (ext, w)


def _conv_halo_bwd(width, res, dy):
    ext, w = res
    n = ext.shape[0]
    dyp = jnp.concatenate([jnp.zeros((SUBLANES, dy.shape[1]), dy.dtype), dy], axis=0)
    dext = None
    dw_rows = []
    for k in range(width):
        s = width - 1 - k
        sh = dyp if s == 0 else pltpu.roll(dyp, n - s, 0)
        term = sh * w[k:k + 1, :]
        dext = term if dext is None else dext + term
        xs = ext if s == 0 else pltpu.roll(ext, s, 0)
        dw_rows.append(jnp.sum(dy * xs[SUBLANES:, :], axis=0, keepdims=True))
    return dext, jnp.concatenate(dw_rows, axis=0)


_conv_halo.defvjp(_conv_halo_fwd, _conv_halo_bwd)


@jax.custom_jvp
def _softplus(x):
    return jnp.maximum(x, 0.0) + _log1p_small(jnp.exp(-jnp.abs(x)))


@_softplus.defjvp
def _softplus_jvp(primals, tangents):
    (x,), (t,) = primals, tangents
    return _softplus(x), t * _sigmoid(x)


def _tri_inv(a, ii, jj):
    eye = (ii == jj).astype(F32)
    same = (ii // 16) == (jj // 16)
    ad = jnp.where(same, a, 0.0)
    ao = a - ad
    b1 = -ad
    b2 = _dot3(b1, b1)
    b4 = _dot3(b2, b2)
    b8 = _dot3(b4, b4)
    dinv = _dot3(_dot3(_dot3(eye + b1, eye + b2), eye + b4), eye + b8)
    n1 = _dot3(dinv, ao)
    n2 = _dot3(n1, n1)
    return _dot3(_dot3(eye - n1, eye + n2), dinv)


def _gdn_chunk(s_in, eq, ek, ev, a_col, b_col, gate, wq, wk, wv, a_log, dt_bias, ng):
    q = _silu(_conv_halo(eq, wq, GDN_CONV))
    k = _silu(_conv_halo(ek, wk, GDN_CONV))
    v = _silu(_conv_halo(ev, wv, GDN_CONV))
    q = q * lax.rsqrt(jnp.sum(q * q, axis=-1, keepdims=True) + RMS_EPS) * (HEAD_DIM ** -0.5)
    k = k * lax.rsqrt(jnp.sum(k * k, axis=-1, keepdims=True) + RMS_EPS)
    g = -jnp.exp(a_log) * _softplus(a_col + dt_bias)
    beta = _sigmoid(b_col)
    ii = lax.broadcasted_iota(jnp.int32, (CHUNK, CHUNK), 0)
    jj = lax.broadcasted_iota(jnp.int32, (CHUNK, CHUNK), 1)
    gb = jnp.broadcast_to(g, (CHUNK, CHUNK))
    g_row = jnp.sum(jnp.where(ii == jj, gb, 0.0), axis=0, keepdims=True)
    gc_row = jnp.sum(jnp.where(ii <= jj, gb, 0.0), axis=0, keepdims=True)
    gc_col = jnp.sum(jnp.where(jj <= ii, jnp.broadcast_to(g_row, (CHUNK, CHUNK)), 0.0), axis=1, keepdims=True)
    g_last = jnp.sum(g, axis=0, keepdims=True)
    causal = ii >= jj
    decay = jnp.where(causal, jnp.exp(jnp.where(causal, gc_col - gc_row, 0.0)), 0.0)
    kb = k * beta
    a_kk = jnp.where(ii > jj, _bdot_nt(kb, k) * decay, 0.0)
    t_inv = _tri_inv(a_kk, ii, jj)
    u = _dot3(t_inv, v * beta)
    w = _dot3(t_inv, kb * jnp.exp(gc_col))
    qk = jnp.where(causal, _bdot_nt(q, k) * decay, 0.0)
    k_dec = k * jnp.exp(g_last - gc_col)
    q_dec = q * jnp.exp(gc_col)
    v_new = u - _bdot(w, s_in)
    o = _bdot(q_dec, s_in) + _bdot(qk, v_new)
    s_out = s_in * jnp.exp(g_last) + _bdot_tn(k_dec, v_new)
    o = o * lax.rsqrt(jnp.mean(o * o, axis=-1, keepdims=True) + RMS_EPS) * ng
    return s_out, o * _silu(gate)


GDN_HB = 4
GDN_W = GDN_HB * HEAD_DIM
GDN_LANES = tuple(slice(i * HEAD_DIM, (i + 1) * HEAD_DIM) for i in range(GDN_HB))


def _gdn_specs(m, rev):
    t = GDN_T
    ns = m // t
    hb = t // SUBLANES
    pos = (lambda s: ns - 1 - s) if rev else (lambda s: s)
    tile = pl.BlockSpec((t, GDN_W), lambda h, s: (pos(s), h))
    halo = pl.BlockSpec((SUBLANES, GDN_W), lambda h, s: (jnp.maximum(pos(s) * hb - 1, 0), h))
    small = pl.BlockSpec((t, LANES), lambda h, s: (pos(s), 0))
    wsp = pl.BlockSpec((GDN_CONV, GDN_W), lambda h, s: (0, h))
    sc = pl.BlockSpec((GDN_HB, 1, LANES), lambda h, s: (h, 0, 0))
    ngs = pl.BlockSpec((1, HEAD_DIM), lambda h, s: (0, 0))
    st = pl.BlockSpec((GDN_HB, t // CHUNK, HEAD_DIM, HEAD_DIM), lambda h, s: (h, pos(s), 0, 0))
    return tile, halo, small, wsp, sc, ngs, st


def _gdn_scalars(a_log, dt_bias):
    bc = lambda p: jnp.broadcast_to(p.reshape(N_HEADS, 1, 1), (N_HEADS, 1, LANES))
    return bc(a_log), bc(dt_bias)


def _gdn_fill_ext(xpad_ref, tiles, halos, keep):
    for sec in range(3):
        xpad_ref[sec, 0:SUBLANES, :] = halos[sec][...] * keep
        xpad_ref[sec, SUBLANES:, :] = tiles[sec][...]


def _gdn_fwd(xq, xk, xv, small, gate, wq, wk, wv, a_log, dt_bias, norm_g):
    m = xq.shape[0]
    t = GDN_T
    nc = t // CHUNK
    tile, halo, smallsp, wsp, sc, ngs, st = _gdn_specs(m, False)

    def body(q_ref, qh_ref, k_ref, kh_ref, v_ref, vh_ref, sm_ref, gate_ref, wq_ref, wk_ref, wv_ref,
             al_ref, dt_ref, ng_ref, y_ref, st_ref, xpad_ref, s_ref):
        hg, s = pl.program_id(0), pl.program_id(1)
        _gdn_fill_ext(xpad_ref, (q_ref, k_ref, v_ref), (qh_ref, kh_ref, vh_ref), (s > 0).astype(F32))

        @pl.when(s == 0)
        def _():
            s_ref[...] = jnp.zeros_like(s_ref)

        ng = ng_ref[...]
        par = [(wq_ref[:, ls], wk_ref[:, ls], wv_ref[:, ls], al_ref[i][:, 0:1], dt_ref[i][:, 0:1])
               for i, ls in enumerate(GDN_LANES)]

        def step(c, states):
            off = pl.multiple_of(c * CHUNK, CHUNK)
            ext = pl.ds(off, CHUNK + SUBLANES)
            sm = sm_ref[pl.ds(off, CHUNK), :]
            ins = [(xpad_ref[0, ext, ls], xpad_ref[1, ext, ls], xpad_ref[2, ext, ls],
                    _lane_pick(sm, GDN_A_LANE + hg * GDN_HB + i), _lane_pick(sm, GDN_B_LANE + hg * GDN_HB + i),
                    gate_ref[pl.ds(off, CHUNK), ls]) for i, ls in enumerate(GDN_LANES)]
            outs = [_gdn_chunk(states[i], *ins[i], *par[i], ng) for i in range(GDN_HB)]
            for i, ls in enumerate(GDN_LANES):
                st_ref[i, c] = states[i]
                y_ref[pl.ds(off, CHUNK), ls] = outs[i][1].astype(BF16)
            return tuple(o[0] for o in outs)

        states = lax.fori_loop(0, nc, step, tuple(s_ref[i] for i in range(GDN_HB)))
        for i in range(GDN_HB):
            s_ref[i] = states[i]

    return pl.pallas_call(
        body, name="gdn_fwd", grid=(N_HEADS // GDN_HB, m // t),
        in_specs=[tile, halo, tile, halo, tile, halo, smallsp, tile, wsp, wsp, wsp, sc, sc, ngs],
        out_specs=[tile, st],
        out_shape=[jax.ShapeDtypeStruct((m, WIDTH), BF16),
                   jax.ShapeDtypeStruct((N_HEADS, m // CHUNK, HEAD_DIM, HEAD_DIM), F32)],
        scratch_shapes=[pltpu.VMEM((3, t + SUBLANES, GDN_W), F32), pltpu.VMEM((GDN_HB, HEAD_DIM, HEAD_DIM), F32)],
        compiler_params=_params(("parallel", "arbitrary")),
    )(xq, xq, xk, xk, xv, xv, small, gate, wq, wk, wv, *_gdn_scalars(a_log, dt_bias), norm_g.reshape(1, HEAD_DIM))


def _gdn_bwd(xq, xk, xv, small, gate, wq, wk, wv, a_log, dt_bias, norm_g, states, dy):
    m = xq.shape[0]
    t = GDN_T
    nc = t // CHUNK
    tile, halo, smallsp, wsp, sc, ngs, st = _gdn_specs(m, True)

    def body(q_ref, qh_ref, k_ref, kh_ref, v_ref, vh_ref, sm_ref, gate_ref, wq_ref, wk_ref, wv_ref,
             al_ref, dt_ref, ng_ref, st_ref, dy_ref,
             dq_ref, dk_ref, dv_ref, dgate_ref, dab_ref, dwq_ref, dwk_ref, dwv_ref, dal_ref, ddt_ref, dng_ref,
             xpad_ref, dxpad_ref, ds_ref, carry_ref):
        hg, s = pl.program_id(0), pl.program_id(1)
        _gdn_fill_ext(xpad_ref, (q_ref, k_ref, v_ref), (qh_ref, kh_ref, vh_ref), (s < m // t - 1).astype(F32))

        @pl.when(s == 0)
        def _():
            ds_ref[...] = jnp.zeros_like(ds_ref)
            carry_ref[...] = jnp.zeros_like(carry_ref)
            for r in (dwq_ref, dwk_ref, dwv_ref, dal_ref, ddt_ref):
                r[...] = jnp.zeros_like(r)

        @pl.when((s == 0) & (hg == 0))
        def _():
            dng_ref[...] = jnp.zeros_like(dng_ref)

        dxpad_ref[:, 0:t, :] = jnp.zeros((3, t, GDN_W), F32)
        dxpad_ref[:, t:, :] = carry_ref[...]
        ng = ng_ref[...]
        par = [(wq_ref[:, ls], wk_ref[:, ls], wv_ref[:, ls], al_ref[i][:, 0:1], dt_ref[i][:, 0:1])
               for i, ls in enumerate(GDN_LANES)]
        lane = lax.broadcasted_iota(jnp.int32, (CHUNK, LANES), 1)

        def step(cc, carry):
            c = nc - 1 - cc
            off = pl.multiple_of(c * CHUNK, CHUNK)
            ext = pl.ds(off, CHUNK + SUBLANES)
            sm = sm_ref[pl.ds(off, CHUNK), :]
            ins = [(st_ref[i, c], xpad_ref[0, ext, ls], xpad_ref[1, ext, ls], xpad_ref[2, ext, ls],
                    _lane_pick(sm, GDN_A_LANE + hg * GDN_HB + i), _lane_pick(sm, GDN_B_LANE + hg * GDN_HB + i),
                    gate_ref[pl.ds(off, CHUNK), ls]) for i, ls in enumerate(GDN_LANES)]
            dys = [dy_ref[pl.ds(off, CHUNK), ls] for ls in GDN_LANES]
            gs = []
            for i in range(GDN_HB):
                _, vjp = jax.vjp(_gdn_chunk, *ins[i], *par[i], ng)
                gs.append(vjp((carry[i][0], dys[i])))
            for i, ls in enumerate(GDN_LANES):
                g = gs[i]
                for sec in range(3):
                    dxpad_ref[sec, ext, ls] += g[1 + sec]
                dab_ref[i, pl.ds(off, CHUNK), :] = jnp.where(lane == 0, g[4], jnp.where(lane == 1, g[5], 0.0))
                dgate_ref[pl.ds(off, CHUNK), ls] = g[6].astype(BF16)
            return tuple((gs[i][0], tuple(a + b for a, b in zip(carry[i][1], gs[i][7:]))) for i in range(GDN_HB))

        zero = (jnp.zeros((GDN_CONV, HEAD_DIM), F32),) * 3 + (jnp.zeros((1, 1), F32),) * 2 + (jnp.zeros((1, HEAD_DIM), F32),)
        done = lax.fori_loop(0, nc, step, tuple((ds_ref[i], zero) for i in range(GDN_HB)))
        carry_ref[...] = dxpad_ref[:, 0:SUBLANES, :]
        dq_ref[...] = dxpad_ref[0, SUBLANES:, :].astype(BF16)
        dk_ref[...] = dxpad_ref[1, SUBLANES:, :].astype(BF16)
        dv_ref[...] = dxpad_ref[2, SUBLANES:, :].astype(BF16)
        for i, ls in enumerate(GDN_LANES):
            d_state, acc = done[i]
            ds_ref[i] = d_state
            dwq_ref[:, ls] += acc[0]
            dwk_ref[:, ls] += acc[1]
            dwv_ref[:, ls] += acc[2]
            dal_ref[i] += jnp.broadcast_to(acc[3], (1, LANES))
            ddt_ref[i] += jnp.broadcast_to(acc[4], (1, LANES))
            dng_ref[...] += acc[5]

    bs = jax.ShapeDtypeStruct((m, WIDTH), BF16)
    ws = jax.ShapeDtypeStruct((GDN_CONV, WIDTH), F32)
    ss = jax.ShapeDtypeStruct((N_HEADS, 1, LANES), F32)
    ns = m // t
    dabsp = pl.BlockSpec((GDN_HB, t, LANES), lambda h, s: (h, ns - 1 - s, 0))
    return pl.pallas_call(
        body, name="gdn_bwd", grid=(N_HEADS // GDN_HB, ns),
        in_specs=[tile, halo, tile, halo, tile, halo, smallsp, tile, wsp, wsp, wsp, sc, sc, ngs, st, tile],
        out_specs=[tile, tile, tile, tile, dabsp, wsp, wsp, wsp, sc, sc, ngs],
        out_shape=[bs, bs, bs, bs, jax.ShapeDtypeStruct((N_HEADS, m, LANES), F32), ws, ws, ws, ss, ss,
                   jax.ShapeDtypeStruct((1, HEAD_DIM), F32)],
        scratch_shapes=[pltpu.VMEM((3, t + SUBLANES, GDN_W), F32), pltpu.VMEM((3, t + SUBLANES, GDN_W), F32),
                        pltpu.VMEM((GDN_HB, HEAD_DIM, HEAD_DIM), F32), pltpu.VMEM((3, SUBLANES, GDN_W), F32)],
        compiler_params=_params(("arbitrary", "arbitrary")),
    )(xq, xq, xk, xk, xv, xv, small, gate, wq, wk, wv, *_gdn_scalars(a_log, dt_bias), norm_g.reshape(1, HEAD_DIM),
      states, dy)


def _colsum(a, name):
    m, n = a.shape
    tm, tn = 512, _tile(n, 1408)

    def body(a_ref, o_ref):
        @pl.when(pl.program_id(1) == 0)
        def _():
            o_ref[...] = jnp.zeros_like(o_ref)

        o_ref[...] += jnp.sum(a_ref[...].astype(F32), axis=0, keepdims=True)

    return pl.pallas_call(
        body, name=name, grid=(n // tn, m // tm),
        in_specs=[pl.BlockSpec((tm, tn), lambda j, i: (i, j))], out_specs=pl.BlockSpec((1, tn), lambda j, i: (0, j)),
        out_shape=jax.ShapeDtypeStruct((1, n), F32),
        compiler_params=_params(("parallel", "arbitrary")),
    )(a)[0]


ANY = pl.BlockSpec(memory_space=pl.ANY)


def _mesh_pos():
    return lax.axis_index("x"), lax.axis_index("y"), lax.axis_index("c")


def _other_chips(px, py):
    return [(1 - px, py), (px, 1 - py), (1 - px, 1 - py)]


def _all_gather(xs, name):
    n = len(xs)

    def body(*refs):
        x_refs, out_refs = refs[:n], refs[n:2 * n]
        send_sems, recv_sems, local_sems = refs[2 * n:]
        px, py, pc = _mesh_pos()
        me, sibling = (px, py, pc), (px, py, 1 - pc)
        chips = _other_chips(px, py)

        def slot(a, qx, qy, qc):
            return out_refs[a].at[4 * qx + 2 * qy + qc]

        def copy(a, k, block, to, src=None):
            return pltpu.make_async_remote_copy(
                src_ref=slot(a, *block) if src is None else src, dst_ref=slot(a, *block),
                send_sem=send_sems.at[a, k], recv_sem=recv_sems.at[a, k], device_id=to, device_id_type=MESH)

        mine = [pltpu.make_async_copy(x_refs[a], slot(a, *me), local_sems.at[a]) for a in range(n)]
        for cp in mine:
            cp.start()
        first = []
        for a in range(n):
            first.append(copy(a, 0, me, sibling, src=x_refs[a]))
            first += [copy(a, 1 + j, me, (*chip, pc), src=x_refs[a]) for j, chip in enumerate(chips)]
        for cp in first:
            cp.start()
        passed = []
        for j, chip in enumerate(chips):
            for a in range(n):
                copy(a, 1 + j, (*chip, pc), me).wait_recv()
                passed.append(copy(a, 4 + j, (*chip, pc), sibling))
                passed[-1].start()
        for a in range(n):
            copy(a, 0, sibling, me).wait_recv()
            for j, chip in enumerate(chips):
                copy(a, 4 + j, (*chip, 1 - pc), me).wait_recv()
        for cp in first + passed:
            cp.wait_send()
        for cp in mine:
            cp.wait()

    return pl.pallas_call(
        body, name=name, out_shape=[jax.ShapeDtypeStruct((N_DEV,) + x.shape, x.dtype) for x in xs],
        in_specs=[ANY] * n, out_specs=[ANY] * n,
        scratch_shapes=[pltpu.SemaphoreType.DMA((n, 7)), pltpu.SemaphoreType.DMA((n, 7)), pltpu.SemaphoreType.DMA((n,))],
    )(*xs)


def _pair_exchange(gs, name):
    n = len(gs)

    def body(*refs):
        g_refs, r_refs = refs[:n], refs[n:2 * n]
        send_sems, recv_sems = refs[2 * n:]
        px, py, pc = _mesh_pos()
        copies = [pltpu.make_async_remote_copy(
            src_ref=g_refs[a].at[2 * j + (1 - pc)], dst_ref=r_refs[a].at[j], send_sem=send_sems.at[a, j],
            recv_sem=recv_sems.at[a, j], device_id=(px, py, 1 - pc), device_id_type=MESH)
            for a in range(n) for j in range(4)]
        for cp in copies:
            cp.start()
        for cp in copies:
            cp.wait()

    return pl.pallas_call(
        body, name=name, out_shape=[jax.ShapeDtypeStruct((4,) + g.shape[1:], g.dtype) for g in gs],
        in_specs=[ANY] * n, out_specs=[ANY] * n,
        scratch_shapes=[pltpu.SemaphoreType.DMA((n, 4)), pltpu.SemaphoreType.DMA((n, 4))],
    )(*gs)


def _chip_exchange(hs, name):
    n = len(hs)

    def body(*refs):
        h_refs, r_refs = refs[:n], refs[n:2 * n]
        send_sems, recv_sems = refs[2 * n:]
        px, py, pc = _mesh_pos()
        copies = [pltpu.make_async_remote_copy(
            src_ref=h_refs[a].at[2 * cx + cy], dst_ref=r_refs[a].at[k], send_sem=send_sems.at[a, k],
            recv_sem=recv_sems.at[a, k], device_id=(cx, cy, pc), device_id_type=MESH)
            for a in range(n) for k, (cx, cy) in enumerate(_other_chips(px, py))]
        for cp in copies:
            cp.start()
        for cp in copies:
            cp.wait()

    return pl.pallas_call(
        body, name=name, out_shape=[jax.ShapeDtypeStruct((3,) + h.shape[1:], h.dtype) for h in hs],
        in_specs=[ANY] * n, out_specs=[ANY] * n,
        scratch_shapes=[pltpu.SemaphoreType.DMA((n, 3)), pltpu.SemaphoreType.DMA((n, 3))],
    )(*hs)


def _select_dot(blocks, target, valid):
    src = jnp.concatenate(blocks, axis=1)
    n = src.shape[1]
    pos = lax.broadcasted_iota(jnp.int32, (n, LANES), 0)
    sel = ((pos == target) & valid).astype(src.dtype)
    return lax.dot_general(src, sel, (((1,), (0,)), ((), ())), preferred_element_type=F32)


def _shards_to_pieces(g, pieces, shard_w, name):
    _, nl, r, pw = g.shape
    nb = pw // LANES
    c0s, nvs, blks = [], [], []
    for start, real, padded in pieces:
        for u in range(padded // LANES):
            c0, nv = start + LANES * u, max(0, min(LANES, real - LANES * u))
            cp_lo = c0 + (pw - shard_w) * min(c0 // shard_w, N_DEV - 1)
            cp_hi = c0 + nv - 1 + (pw - shard_w) * min((c0 + nv - 1) // shard_w, N_DEV - 1)
            assert nv == 0 or cp_hi - (cp_lo // LANES) * LANES < 3 * LANES
            c0s.append(c0)
            nvs.append(nv)
            blks.append(cp_lo // LANES if nv else 0)
    table = jnp.array([c0s, nvs, blks], jnp.int32)
    tr = min(r, 2048)
    last = N_DEV * nb - 1

    def body(tab_ref, b0, b1, b2, o_ref):
        t = pl.program_id(2)
        c0, nv, blk0 = tab_ref[0, t], tab_ref[1, t], tab_ref[2, t]
        j = lax.broadcasted_iota(jnp.int32, (1, LANES), 1)
        c = c0 + j
        k = jnp.zeros_like(c)
        for q in range(1, N_DEV):
            k = k + (c >= q * shard_w).astype(jnp.int32)
        target = c + (pw - shard_w) * k - blk0 * LANES
        o_ref[0] = _select_dot([b0[0, 0], b1[0, 0], b2[0, 0]], target, j < nv).astype(o_ref.dtype)

    def src(d):
        def index(l, i, t, tab_ref):
            blk = jnp.minimum(tab_ref[2, t] + d, last)
            return (blk // nb, l, i, blk % nb)
        return pl.BlockSpec((1, 1, tr, LANES), index)

    return pl.pallas_call(
        body, name=name, out_shape=jax.ShapeDtypeStruct((nl, r, LANES * len(c0s)), g.dtype),
        grid_spec=pltpu.PrefetchScalarGridSpec(
            num_scalar_prefetch=1, grid=(nl, r // tr, len(c0s)), in_specs=[src(0), src(1), src(2)],
            out_specs=pl.BlockSpec((1, tr, LANES), lambda l, i, t, tab_ref: (l, i, t))),
        compiler_params=_params(("parallel", "parallel", "parallel")),
    )(table, g, g, g)


def _pieces_to_shards(d, segs, far_start, shard_w, pw, name):
    nl, r, s_cols = d.shape
    nb = pw // LANES
    shifts = [src - gs for gs, src in segs]

    def source(c):
        return c + [sh for (gs, _), sh in zip(segs, shifts) if c >= gs][-1]

    nvs, blks = [], []
    for k in range(N_DEV):
        for b in range(nb):
            c0, nv = k * shard_w + LANES * b, max(0, min(LANES, shard_w - LANES * b))
            near = [source(c) for c in range(c0, c0 + nv)]
            near = [s for s in near if far_start is None or s < far_start]
            blk0 = min(near) // LANES if near else 0
            assert not near or max(near) - blk0 * LANES < 3 * LANES
            nvs.append(nv)
            blks.append(blk0)
    table = jnp.array([nvs, blks], jnp.int32)
    tr = min(r, 2048)
    last = s_cols // LANES - 1
    far_blk = 0 if far_start is None else far_start // LANES

    def body(tab_ref, b0, b1, b2, bf, o_ref):
        t = pl.program_id(2)
        nv, blk0 = tab_ref[0, t], tab_ref[1, t]
        j = lax.broadcasted_iota(jnp.int32, (1, LANES), 1)
        c = (t // nb) * shard_w + (t % nb) * LANES + j
        s = c + shifts[0]
        for (gs, _), prev, sh in zip(segs[1:], shifts[:-1], shifts[1:]):
            s = s + (sh - prev) * (c >= gs).astype(jnp.int32)
        target = s - blk0 * LANES
        if far_start is not None:
            target = jnp.where(s >= far_start, 3 * LANES + s - far_start, target)
        o_ref[0, 0] = _select_dot([b0[0], b1[0], b2[0], bf[0]], target, j < nv).astype(o_ref.dtype)

    def src(dd):
        return pl.BlockSpec((1, tr, LANES), lambda l, i, t, tab_ref: (l, i, jnp.minimum(tab_ref[1, t] + dd, last)))

    return pl.pallas_call(
        body, name=name, out_shape=jax.ShapeDtypeStruct((N_DEV, nl, r, pw), d.dtype),
        grid_spec=pltpu.PrefetchScalarGridSpec(
            num_scalar_prefetch=1, grid=(nl, r // tr, N_DEV * nb),
            in_specs=[src(0), src(1), src(2), pl.BlockSpec((1, tr, LANES), lambda l, i, t, tab_ref: (l, i, far_blk))],
            out_specs=pl.BlockSpec((1, 1, tr, LANES), lambda l, i, t, tab_ref: (t // nb, l, i, t % nb))),
        compiler_params=_params(("parallel", "parallel", "parallel")),
    )(table, d, d, d, d)


def _cat_blocks(g, axis):
    _, nl, r, c = g.shape
    shape = (nl, N_DEV * r, c) if axis == 0 else (nl, r, N_DEV * c)

    def body(g_ref, o_ref):
        o_ref[0] = g_ref[0, 0]

    return pl.pallas_call(
        body, name="cat_blocks", grid=(nl, N_DEV), out_shape=jax.ShapeDtypeStruct(shape, g.dtype),
        in_specs=[pl.BlockSpec((1, 1, r, c), lambda l, k: (k, l, 0, 0))],
        out_specs=pl.BlockSpec((1, r, c), (lambda l, k: (l, k, 0)) if axis == 0 else (lambda l, k: (l, 0, k))),
        compiler_params=_params(("parallel", "parallel")),
    )(g)


def _split_blocks(d, axis):
    nl = d.shape[0]
    r, c = (d.shape[1] // N_DEV, d.shape[2]) if axis == 0 else (d.shape[1], d.shape[2] // N_DEV)

    def body(d_ref, o_ref):
        o_ref[0, 0] = d_ref[0]

    return pl.pallas_call(
        body, name="split_blocks", grid=(nl, N_DEV), out_shape=jax.ShapeDtypeStruct((N_DEV, nl, r, c), d.dtype),
        in_specs=[pl.BlockSpec((1, r, c), (lambda l, k: (l, k, 0)) if axis == 0 else (lambda l, k: (l, 0, k)))],
        out_specs=pl.BlockSpec((1, 1, r, c), lambda l, k: (k, l, 0, 0)),
        compiler_params=_params(("parallel", "parallel")),
    )(d)


def _row_tile(r, mult, pref=256):
    return max(d for d in range(mult, pref + 1, mult) if r % d == 0)


def _pair_sum(g, r1):
    _, r, c = g.shape
    tr = _row_tile(r, 16)

    def body(g_ref, r_ref, h_ref, own_ref):
        j = pl.program_id(1)
        px, py, pc = _mesh_pos()
        mine = jnp.where(pc == 0, g_ref[0, 0].astype(F32), g_ref[0, 1].astype(F32))
        val = (mine + r_ref[0].astype(F32)).astype(h_ref.dtype)
        h_ref[0] = val

        @pl.when(j == 2 * px + py)
        def _():
            own_ref[...] = val

    return pl.pallas_call(
        body, name="rs_pair_sum", grid=(r // tr, 4),
        in_specs=[pl.BlockSpec((1, 2, tr, c), lambda i, j: (j, 0, i, 0)), pl.BlockSpec((1, tr, c), lambda i, j: (j, i, 0))],
        out_specs=[pl.BlockSpec((1, tr, c), lambda i, j: (j, i, 0)), pl.BlockSpec((tr, c), lambda i, j: (i, 0))],
        out_shape=[jax.ShapeDtypeStruct((4, r, c), g.dtype), jax.ShapeDtypeStruct((r, c), g.dtype)],
        compiler_params=_params(("parallel", "arbitrary")),
    )(g.reshape(4, 2, r, c), r1)


def _adamw(parts, w, m, v, name, tr):
    r, c = w.shape
    n = len(parts)
    slots = [slot for _, slot in parts]

    def body(*refs):
        w_ref, m_ref, v_ref = refs[n:n + 3]
        g_ref, d_ref, nm_ref, nv_ref = refs[n + 3:]
        g = None
        for slot, ref in zip(slots, refs[:n]):
            t = (ref[...] if slot is None else ref[0]).astype(F32)
            g = t if g is None else g + t
        nm = ADAM_B1 * m_ref[...] + (1.0 - ADAM_B1) * g
        nv = ADAM_B2 * v_ref[...] + (1.0 - ADAM_B2) * (g * g)
        m_hat = nm / (1.0 - ADAM_B1 ** ADAM_STEP)
        v_hat = nv / (1.0 - ADAM_B2 ** ADAM_STEP)
        g_ref[...] = g
        d_ref[...] = -ADAM_LR * (m_hat / (jnp.sqrt(v_hat) + ADAM_EPS) + ADAM_WD * w_ref[...])
        nm_ref[...] = nm
        nv_ref[...] = nv

    flat = pl.BlockSpec((tr, c), lambda i: (i, 0))
    specs = [flat if slot is None else pl.BlockSpec((1, tr, c), functools.partial(lambda i, s: (s, i, 0), s=slot))
             for _, slot in parts]
    out = jax.ShapeDtypeStruct((r, c), F32)
    return pl.pallas_call(
        body, name=name, grid=(r // tr,), in_specs=specs + [flat] * 3, out_specs=[flat] * 4, out_shape=[out] * 4,
        compiler_params=_params(("parallel",)),
    )(*[a for a, _ in parts], w, m, v)


def _adamw_whole(parts, w, m, v, name):
    n_parts = parts.shape[0]
    shape = w.shape[1:]
    zeros = (0,) * len(shape)

    def body(p_ref, w_ref, m_ref, v_ref, g_ref, d_ref, nm_ref, nv_ref):
        g = p_ref[0]
        for k in range(1, n_parts):
            g = g + p_ref[k]
        nm = ADAM_B1 * m_ref[...] + (1.0 - ADAM_B1) * g
        nv = ADAM_B2 * v_ref[...] + (1.0 - ADAM_B2) * (g * g)
        m_hat = nm / (1.0 - ADAM_B1 ** ADAM_STEP)
        v_hat = nv / (1.0 - ADAM_B2 ** ADAM_STEP)
        g_ref[...] = g
        d_ref[...] = -ADAM_LR * (m_hat / (jnp.sqrt(v_hat) + ADAM_EPS) + ADAM_WD * w_ref[...])
        nm_ref[...] = nm
        nv_ref[...] = nv

    one = pl.BlockSpec((1,) + shape, lambda l: (l,) + zeros)
    out = jax.ShapeDtypeStruct(w.shape, F32)
    return pl.pallas_call(
        body, name=name, grid=(w.shape[0],),
        in_specs=[pl.BlockSpec((n_parts, 1) + shape, lambda l: (0, l) + zeros), one, one, one],
        out_specs=[one] * 4, out_shape=[out] * 4, compiler_params=_params(("parallel",)),
    )(parts, w, m, v)


def _sum_parts(parts, name):
    n_parts = parts.shape[0]
    shape = parts.shape[2:]
    zeros = (0,) * len(shape)

    def body(p_ref, o_ref):
        g = p_ref[0]
        for k in range(1, n_parts):
            g = g + p_ref[k]
        o_ref[...] = g

    return pl.pallas_call(
        body, name=name, grid=(parts.shape[1],),
        in_specs=[pl.BlockSpec((n_parts, 1) + shape, lambda l: (0, l) + zeros)],
        out_specs=pl.BlockSpec((1,) + shape, lambda l: (l,) + zeros),
        out_shape=jax.ShapeDtypeStruct(parts.shape[1:], F32), compiler_params=_params(("parallel",)),
    )(parts)


SMALL = ("b_in", "sgu_ln_g", "sgu_ln_b", "sgu_w", "sgu_b", "gdn_a_log", "gdn_dt_bias", "gdn_norm_g", "ln1_g", "ln1_b",
         "ffn_conv_b", "ln2_g", "ln2_b")
WEIGHT_ORDER = ("w_in", "b_in", "sgu_ln_g", "sgu_ln_b", "sgu_w", "sgu_b", "gdn_conv_w", "gdn_a_log", "gdn_dt_bias",
                "gdn_norm_g", "w_proj_a", "w_proj_b", "w_proj_c", "w_out", "ln1_g", "ln1_b", "ffn_w_up", "ffn_conv_w",
                "ffn_conv_b", "ffn_w_down", "ln2_g", "ln2_b")
IN_SHARD = N_IN // N_DEV
IN_SHARD_PAD = -(-IN_SHARD // LANES) * LANES
UP_SHARD = 2 * D_FF // N_DEV
UP_SHARD_PAD = -(-UP_SHARD // LANES) * LANES

O_FOX, O_FF, O_SGU, O_GDN, O_A, O_B, O_GATE, O_GATES = 0, 3072, 3080, 5128, 8200, 8208, 8216, 9240
IN_PIECES = (("fq", O_FOX, WIDTH, BF16), ("fk", O_FOX + WIDTH, WIDTH, BF16), ("fv", O_FOX + 2 * WIDTH, WIDTH, BF16),
             ("su", O_SGU, WIDTH, F32), ("sv", O_SGU + WIDTH, WIDTH, F32),
             ("gq", O_GDN, WIDTH, F32), ("gk", O_GDN + WIDTH, WIDTH, F32), ("gv", O_GDN + 2 * WIDTH, WIDTH, F32),
             ("gg", O_GATE, WIDTH, F32),
             ("ga", O_GATES, D_MODEL, F32), ("gb", O_GATES + D_MODEL, D_MODEL, F32), ("gc", O_GATES + 2 * D_MODEL, D_MODEL, F32))
N_SMALL_COLS = 3 * N_HEADS


def _split_in(w):
    out = {n: w[..., o:o + k] for n, o, k, _ in IN_PIECES}
    sm = jnp.concatenate([w[..., O_FF:O_FF + N_HEADS], w[..., O_A:O_A + 2 * N_HEADS]], axis=-1)
    out["sm"] = jnp.pad(sm, [(0, 0)] * (w.ndim - 1) + [(0, LANES - N_SMALL_COLS)])
    return out


def _pad_to(a, axis, n):
    pad = [(0, 0)] * a.ndim
    pad[axis] = (0, n - a.shape[axis])
    return jnp.pad(a, pad)


IN_MAIN = ((O_FOX, 3 * WIDTH, 3 * WIDTH), (O_SGU, O_A - O_SGU, O_A - O_SGU), (O_GATE, N_IN - O_GATE, N_IN - O_GATE))
N_MAIN = sum(p[2] for p in IN_MAIN)
N_RE = N_MAIN + LANES
IN_OFF = {"fq": 0, "fk": WIDTH, "fv": 2 * WIDTH, "su": 3 * WIDTH, "sv": 4 * WIDTH, "gq": 5 * WIDTH, "gk": 6 * WIDTH,
          "gv": 7 * WIDTH, "gg": 8 * WIDTH, "ga": 9 * WIDTH, "gb": 9 * WIDTH + D_MODEL, "gc": 9 * WIDTH + 2 * D_MODEL,
          "sm": N_MAIN}
IN_SEGS = ((0, 0), (O_FF, N_MAIN), (O_SGU, 3 * WIDTH), (O_A, N_MAIN + N_HEADS), (O_GATE, 8 * WIDTH))
UP_PIECES = ((0, D_FF, D_FF_PAD), (D_FF, D_FF, D_FF_PAD))
UP_SEGS = ((0, 0), (D_FF, D_FF_PAD))


def _shard_cols(g, start, n, shard_w):
    k, r = start // shard_w, start % shard_w
    assert r + n <= shard_w
    return g[k, :, :, r:r + n]


def _reorder_bias(v):
    return jnp.concatenate([v[:3 * WIDTH], v[N_MAIN:N_MAIN + N_HEADS], v[3 * WIDTH:8 * WIDTH],
                            v[N_MAIN + N_HEADS:N_MAIN + N_SMALL_COLS], v[8 * WIDTH:N_MAIN]])


def _layer_weights(wt, l, conv, small):
    w = {"re": wt["in"][l], "b_in": _split_in(small["b_in"][None, :])}
    w["pa"], w["pb"], w["pc"], w["out"] = wt["pa"][l], wt["pb"][l], wt["pc"][l], wt["out"][l]
    w["up"], w["down"] = wt["up"][l], wt["down"][l]
    gcw, fcw, fcb = conv["gdn_conv_w"], conv["ffn_conv_w"], small["ffn_conv_b"][None, :]
    w["gcq"], w["gck"], w["gcv"] = gcw[:, :WIDTH], gcw[:, WIDTH:2 * WIDTH], gcw[:, 2 * WIDTH:]
    w["fcg"], w["fcv"] = _pad_to(fcw[:, :D_FF], 1, D_FF_PAD), _pad_to(fcw[:, D_FF:], 1, D_FF_PAD)
    w["fbg"], w["fbv"] = _pad_to(fcb[:, :D_FF], 1, D_FF_PAD), _pad_to(fcb[:, D_FF:], 1, D_FF_PAD)
    for n in ("sgu_ln_g", "sgu_ln_b", "sgu_w", "sgu_b", "gdn_a_log", "gdn_dt_bias", "gdn_norm_g", "ln1_g", "ln1_b",
              "ln2_g", "ln2_b"):
        w[n] = small[n]
    return w


def _layer_fwd(x, xb, w, tag):
    s = {"x": x, "xb": xb}
    for n, _, k, dt in IN_PIECES + (("sm", 0, LANES, F32),):
        s[n] = _mm(xb, w["re"], mode="nn", name="in_" + n, bias=w["b_in"][n][0], out_dtype=dt, b_off=IN_OFF[n], b_len=k)
    s["c_col"] = _fox_prefix_fwd(s["sm"])
    s["c_row"] = _row_form(s["c_col"], ATT_T)
    s["ya"], s["ya32"], s["lse"] = _fox_fwd(s["fq"], s["fk"], s["fv"], s["c_col"], s["c_row"])
    s["yb"] = _sgu_fwd(s["su"], s["sv"], w["sgu_ln_g"], w["sgu_ln_b"], w["sgu_w"], w["sgu_b"])
    s["yc"], s["states"] = _gdn_fwd(s["gq"], s["gk"], s["gv"], s["sm"], s["gg"], w["gcq"], w["gck"], w["gcv"],
                                    w["gdn_a_log"], w["gdn_dt_bias"], w["gdn_norm_g"])
    s["merged"], s["za"], s["zb"], s["zc"] = _merge_fwd(
        [s["ya"], s["yb"], s["yc"]], [w["pa"], w["pb"], w["pc"]], [s["ga"], s["gb"], s["gc"]])
    s["mix"] = _mm(s["merged"], w["out"], mode="nn", name="out_proj")
    s["x1"], s["x1b"] = _ln_fwd(x, s["mix"], w["ln1_g"], w["ln1_b"], name="ln_fwd")
    s["hug"] = _mm(s["x1b"], w["up"], mode="nn", name="ffn_up_gate", b_off=0, b_len=D_FF_PAD)
    s["huv"] = _mm(s["x1b"], w["up"], mode="nn", name="ffn_up_val", b_off=D_FF_PAD, b_len=D_FF_PAD)
    s["act"] = _ffn_act_fwd(s["hug"], s["huv"], w["fcg"], w["fcv"], w["fbg"], w["fbv"])
    s["ffn"] = _mm(s["act"], w["down"], mode="nn", name="ffn_down")
    x2, x2b = _ln_fwd(s["x1"], s["ffn"], w["ln2_g"], w["ln2_b"], name="ln_fwd")
    return x2, x2b, s


def _layer_bwd(dx2, s, w):
    gb, gc, gs = {}, {}, {}
    ds2, ds2b, dg2, db2 = _ln_bwd(s["x1"], s["ffn"], w["ln2_g"], dx2, name="ln_bwd")
    gs["ln2_g"], gs["ln2_b"] = dg2[0], db2[0]
    gb["ffn_w_down"] = _mm(s["act"], ds2b, mode="tn", name="dw_down", out_dtype=BF16)[:D_FF]
    dact = _mm(ds2b, w["down"], mode="nt", name="dact")
    dhug, dhuv, dcwg, dcwv, dcbg, dcbv = _ffn_act_bwd(s["hug"], s["huv"], w["fcg"], w["fcv"], w["fbg"], w["fbv"], dact)
    gc["ffn_conv_w"] = jnp.concatenate([dcwg[:, :D_FF], dcwv[:, :D_FF]], axis=1)
    gs["ffn_conv_b"] = jnp.concatenate([dcbg[0, :D_FF], dcbv[0, :D_FF]])
    dhu = jnp.concatenate([dhug, dhuv], axis=1)
    gb["ffn_w_up"] = _mm(s["x1b"], dhu, mode="tn", name="dw_up", out_dtype=BF16)
    dx1 = _mm(dhu, w["up"], mode="nt", name="dx_up", add=ds2, add_scale=DEEPNORM_ALPHA)
    ds1, ds1b, dg1, db1 = _ln_bwd(s["x"], s["mix"], w["ln1_g"], dx1, name="ln_bwd")
    gs["ln1_g"], gs["ln1_b"] = dg1[0], db1[0]
    gb["w_out"] = _mm(s["merged"], ds1b, mode="tn", name="dw_out", out_dtype=BF16)
    dmerged = _mm(ds1b, w["out"], mode="nt", name="dmerged")
    dza, dzb, dzc, dga, dgb, dgc = _merge_bwd(dmerged, [s["za"], s["zb"], s["zc"]], [s["ga"], s["gb"], s["gc"]])
    gb["w_proj_a"] = _mm(s["ya"], dza, mode="tn", name="dw_proj", out_dtype=BF16)
    gb["w_proj_b"] = _mm(s["yb"], dzb, mode="tn", name="dw_proj", out_dtype=BF16)
    gb["w_proj_c"] = _mm(s["yc"], dzc, mode="tn", name="dw_proj", out_dtype=BF16)
    dya = _mm(dza, w["pa"], mode="nt", name="dy_proj_bf16", out_dtype=BF16)
    dyb = _mm(dzb, w["pb"], mode="nt", name="dy_proj")
    dyc = _mm(dzc, w["pc"], mode="nt", name="dy_proj")
    d = {"ga": dga, "gb": dgb, "gc": dgc}
    fox = (s["fq"], s["fk"], s["fv"], s["ya32"], dya, s["lse"], s["c_col"], s["c_row"])
    d["fk"], d["fv"], dc_row = _fox_bwd_kv(*fox)
    d["fq"], dc_q = _fox_bwd_q(*fox)
    dc_col = _col_form(dc_row, x_rows(s)) + jnp.pad(dc_q[:, :, 0].T, ((0, 0), (0, LANES - N_HEADS)))
    dsm = _fox_prefix_bwd(s["sm"], dc_col)
    d["su"], d["sv"], dlg, dlb, dsw, dsb = _sgu_bwd(s["su"], s["sv"], w["sgu_ln_g"], w["sgu_ln_b"], w["sgu_w"], w["sgu_b"], dyb)
    gs["sgu_ln_g"], gs["sgu_ln_b"], gs["sgu_w"], gs["sgu_b"] = dlg.reshape(-1), dlb.reshape(-1), dsw, dsb[:, :, 0]
    (d["gq"], d["gk"], d["gv"], d["gg"], dab, dwq, dwk, dwv, dal, ddt, dng) = _gdn_bwd(
        s["gq"], s["gk"], s["gv"], s["sm"], s["gg"], w["gcq"], w["gck"], w["gcv"],
        w["gdn_a_log"], w["gdn_dt_bias"], w["gdn_norm_g"], s["states"], dyc)
    gc["gdn_conv_w"] = jnp.concatenate([dwq, dwk, dwv], axis=1)
    gs["gdn_a_log"], gs["gdn_dt_bias"], gs["gdn_norm_g"] = dal[:, 0, 0], ddt[:, 0, 0], dng[0]
    dab_cols = jnp.concatenate([dab[:, :, 0].T, dab[:, :, 1].T], axis=1)
    d["sm"] = (dsm + jnp.pad(dab_cols, ((0, 0), (N_HEADS, LANES - N_SMALL_COLS)))).astype(BF16)
    dp = jnp.concatenate([d[n] for n in sorted(IN_OFF, key=IN_OFF.get)], axis=1)
    gb["w_in"] = _mm(s["xb"], dp, mode="tn", name="dw_in", out_dtype=BF16)
    gs["b_in"] = _reorder_bias(_colsum(dp, "db_in"))
    dx = _mm(dp, w["re"], mode="nt", name="dx_in", add=ds1, add_scale=DEEPNORM_ALPHA)
    return dx, gb, gc, gs


def x_rows(s):
    return s["x"].shape[0]


def _step(a):
    x = a["x"][0]
    kinds = ("grad_", "delta_", "new_m_", "new_v_")
    res = {}
    bf = lambda n: a[n].astype(BF16)
    pad_in = lambda t: _pad_to(t, 2, IN_SHARD_PAD)
    pad_up = lambda t: _pad_to(t, 2, UP_SHARD_PAD)
    g_in, g_up, g_pa, g_pb, g_pc, g_out, g_down, g_gc, g_fc = _all_gather(
        [pad_in(bf("w_in")), pad_up(bf("ffn_w_up")), bf("w_proj_a"), bf("w_proj_b"), bf("w_proj_c"), bf("w_out"),
         bf("ffn_w_down"), a["gdn_conv_w"], a["ffn_conv_w"]], "gather_weights")
    sm = jnp.concatenate([_shard_cols(g_in, O_FF, N_HEADS, IN_SHARD), _shard_cols(g_in, O_A, 2 * N_HEADS, IN_SHARD)], axis=-1)
    wt = {"in": jnp.concatenate([_shards_to_pieces(g_in, IN_MAIN, IN_SHARD, "in_to_pieces"), _pad_to(sm, 2, LANES)], axis=-1),
          "up": _shards_to_pieces(g_up, UP_PIECES, UP_SHARD, "up_to_pieces"),
          "pa": _cat_blocks(g_pa, 1), "pb": _cat_blocks(g_pb, 1), "pc": _cat_blocks(g_pc, 1),
          "out": _cat_blocks(g_out, 0), "down": _pad_to(_cat_blocks(g_down, 0), 1, D_FF_PAD)}
    conv_full = {"gdn_conv_w": g_gc.transpose(1, 2, 0, 3).reshape(DEPTH, GDN_CONV, 3 * WIDTH),
                 "ffn_conv_w": g_fc.transpose(1, 2, 0, 3).reshape(DEPTH, FFN_CONV, 2 * D_FF)}
    layers = [_layer_weights(wt, l, {n: t[l] for n, t in conv_full.items()}, {n: a[n][l] for n in SMALL})
              for l in range(DEPTH)]

    xb = x.astype(BF16)
    saved = []
    for l in range(DEPTH):
        x, xb, s = _layer_fwd(x, xb, layers[l], l)
        saved.append(s)
    dx, loss = _loss_head(x, a["loss_target"][0])
    res["loss"] = lax.psum(loss, ("x", "y", "c"))

    grads = [None] * DEPTH
    for l in reversed(range(DEPTH)):
        dx, gb, gc, gs = _layer_bwd(dx, saved[l], layers[l])
        grads[l] = {**gb, **gc, **gs}
    res["grad_x"] = dx[None]
    stacked = {n: jnp.stack([g[n] for g in grads]) for n in grads[0]}

    big = (("w_in", _pieces_to_shards(stacked["w_in"], IN_SEGS, N_MAIN, IN_SHARD, IN_SHARD_PAD, "in_to_shards"), pad_in),
           ("ffn_w_up", _pieces_to_shards(stacked["ffn_w_up"], UP_SEGS, None, UP_SHARD, UP_SHARD_PAD, "up_to_shards"), pad_up),
           ("w_proj_a", _split_blocks(stacked["w_proj_a"], 1), None), ("w_proj_b", _split_blocks(stacked["w_proj_b"], 1), None),
           ("w_proj_c", _split_blocks(stacked["w_proj_c"], 1), None), ("w_out", _split_blocks(stacked["w_out"], 0), None),
           ("ffn_w_down", _split_blocks(stacked["ffn_w_down"], 0), None))
    flat3 = lambda t: t.reshape(t.shape[0], -1, t.shape[-1])
    gs_ = [flat3(g) for _, g, _ in big]
    r1s = _pair_exchange(gs_, "rs_pair_exchange")
    sums = [_pair_sum(g, r1) for g, r1 in zip(gs_, r1s)]
    r2s = _chip_exchange([h for h, _ in sums], "rs_chip_exchange")
    for (n, _, pad), (_, own), r2 in zip(big, sums, r2s):
        prep = lambda t: (t if pad is None else pad(t)).reshape(own.shape)
        outs = _adamw([(own, None), (r2, 0), (r2, 1), (r2, 2)], prep(a[n]), prep(a["m_" + n]), prep(a["v_" + n]),
                      "adamw_" + n, _row_tile(own.shape[0], 16, max(16, (1 << 18) // own.shape[1])))
        for kind, o in zip(kinds, outs):
            o = o.reshape((DEPTH, -1, own.shape[1]))
            res[kind + n] = o if pad is None else o[:, :, :a[n].shape[2]]

    nd = lambda t: t.reshape(t.shape[0], 1, t.shape[1]) if t.ndim == 2 else t
    names = list(SMALL) + ["gdn_conv_w", "ffn_conv_w"]
    parts = dict(zip(names, _all_gather([nd(stacked[n]) for n in names], "gather_small_grads")))
    for n in SMALL:
        outs = _adamw_whole(parts[n], nd(a[n]), nd(a["m_" + n]), nd(a["v_" + n]), "adamw_" + n)
        for kind, o in zip(kinds, outs):
            res[kind + n] = o.reshape(a[n].shape)
    me = 4 * lax.axis_index("x") + 2 * lax.axis_index("y") + lax.axis_index("c")
    for n in ("gdn_conv_w", "ffn_conv_w"):
        width = a[n].shape[2]
        g_own = lax.dynamic_slice_in_dim(_sum_parts(parts[n], "sum_" + n), me * width, width, axis=2)
        outs = _adamw_whole(g_own[None], a[n], a["m_" + n], a["v_" + n], "adamw_" + n)
        for kind, o in zip(kinds, outs):
            res[kind + n] = o
    return res


INPUT_ORDER = (("x",) + WEIGHT_ORDER + ("loss_target",) + tuple("m_" + n for n in WEIGHT_ORDER)
               + tuple("v_" + n for n in WEIGHT_ORDER))
OUTPUT_ORDER = (("loss", "grad_x") + tuple(k + n for k in ("grad_", "delta_", "new_m_", "new_v_") for n in WEIGHT_ORDER))


def kernel(x, w_in, b_in, sgu_ln_g, sgu_ln_b, sgu_w, sgu_b, gdn_conv_w, gdn_a_log, gdn_dt_bias, gdn_norm_g, w_proj_a, w_proj_b, w_proj_c, w_out, ln1_g, ln1_b, ffn_w_up, ffn_conv_w, ffn_conv_b, ffn_w_down, ln2_g, ln2_b, loss_target, m_w_in, m_b_in, m_sgu_ln_g, m_sgu_ln_b, m_sgu_w, m_sgu_b, m_gdn_conv_w, m_gdn_a_log, m_gdn_dt_bias, m_gdn_norm_g, m_w_proj_a, m_w_proj_b, m_w_proj_c, m_w_out, m_ln1_g, m_ln1_b, m_ffn_w_up, m_ffn_conv_w, m_ffn_conv_b, m_ffn_w_down, m_ln2_g, m_ln2_b, v_w_in, v_b_in, v_sgu_ln_g, v_sgu_ln_b, v_sgu_w, v_sgu_b, v_gdn_conv_w, v_gdn_a_log, v_gdn_dt_bias, v_gdn_norm_g, v_w_proj_a, v_w_proj_b, v_w_proj_c, v_w_out, v_ln1_g, v_ln1_b, v_ffn_w_up, v_ffn_conv_w, v_ffn_conv_b, v_ffn_w_down, v_ln2_g, v_ln2_b):
    args = (x, w_in, b_in, sgu_ln_g, sgu_ln_b, sgu_w, sgu_b, gdn_conv_w, gdn_a_log, gdn_dt_bias, gdn_norm_g, w_proj_a, w_proj_b, w_proj_c, w_out, ln1_g, ln1_b, ffn_w_up, ffn_conv_w, ffn_conv_b, ffn_w_down, ln2_g, ln2_b, loss_target, m_w_in, m_b_in, m_sgu_ln_g, m_sgu_ln_b, m_sgu_w, m_sgu_b, m_gdn_conv_w, m_gdn_a_log, m_gdn_dt_bias, m_gdn_norm_g, m_w_proj_a, m_w_proj_b, m_w_proj_c, m_w_out, m_ln1_g, m_ln1_b, m_ffn_w_up, m_ffn_conv_w, m_ffn_conv_b, m_ffn_w_down, m_ln2_g, m_ln2_b, v_w_in, v_b_in, v_sgu_ln_g, v_sgu_ln_b, v_sgu_w, v_sgu_b, v_gdn_conv_w, v_gdn_a_log, v_gdn_dt_bias, v_gdn_norm_g, v_w_proj_a, v_w_proj_b, v_w_proj_c, v_w_out, v_ln1_g, v_ln1_b, v_ffn_w_up, v_ffn_conv_w, v_ffn_conv_b, v_ffn_w_down, v_ln2_g, v_ln2_b)
    res = _step(dict(zip(INPUT_ORDER, args)))
    return tuple(res[n] for n in OUTPUT_ORDER)
```

```python
import functools

import jax
import jax.numpy as jnp
from jax import lax
from jax.experimental import pallas as pl
from jax.experimental.pallas import tpu as pltpu

F32 = jnp.float32
BF16 = jnp.bfloat16

N_DEV = 8
DEPTH = 4
D_MODEL = 2048
HEAD_DIM = 128
N_HEADS = 8
WIDTH = N_HEADS * HEAD_DIM
CHUNK = 64
SGU_SPAN = 128
GDN_CONV = 4
FFN_CONV = 3
D_FF = 5504
D_FF_PAD = 5632
N_IN = 15384
DEEPNORM_ALPHA = (2 * DEPTH) ** 0.25
LN_EPS = 1e-5
RMS_EPS = 1e-6
ADAM_LR = 0.001
ADAM_B1 = 0.9
ADAM_B2 = 0.999
ADAM_EPS = 1e-08
ADAM_WD = 0.01
ADAM_STEP = 10

LANES = 128
SUBLANES = 8
VMEM_LIMIT = 56 * 1024 * 1024

MESH = pl.DeviceIdType.MESH


def _params(sem, vmem=VMEM_LIMIT):
    return pltpu.CompilerParams(dimension_semantics=sem, vmem_limit_bytes=vmem)


def _tile(n, pref):
    best = None
    for d in range(LANES, min(n, pref) + 1, LANES):
        if n % d == 0:
            best = d
    return n if best is None else best


def _sigmoid(x):
    return jax.nn.sigmoid(x)


def _silu(x):
    return x * jax.nn.sigmoid(x)


def _raw_dot(a, b, dims):
    return lax.dot_general(a.astype(BF16), b.astype(BF16), (dims, ((), ())), preferred_element_type=F32)


@jax.custom_vjp
def _bdot(a, b):
    return _raw_dot(a, b, ((1,), (0,)))


def _bdot_fwd(a, b):
    return _bdot(a, b), (a, b)


def _bdot_bwd(res, ct):
    a, b = res
    return _raw_dot(ct, b, ((1,), (1,))), _raw_dot(a, ct, ((0,), (0,)))


_bdot.defvjp(_bdot_fwd, _bdot_bwd)


@jax.custom_vjp
def _bdot_nt(a, b):
    return _raw_dot(a, b, ((1,), (1,)))


def _bdot_nt_fwd(a, b):
    return _bdot_nt(a, b), (a, b)


def _bdot_nt_bwd(res, ct):
    a, b = res
    return _raw_dot(ct, b, ((1,), (0,))), _raw_dot(ct, a, ((0,), (0,)))


_bdot_nt.defvjp(_bdot_nt_fwd, _bdot_nt_bwd)


@jax.custom_vjp
def _bdot_tn(a, b):
    return _raw_dot(a, b, ((0,), (0,)))


def _bdot_tn_fwd(a, b):
    return _bdot_tn(a, b), (a, b)


def _bdot_tn_bwd(res, ct):
    a, b = res
    return _raw_dot(b, ct, ((1,), (1,))), _raw_dot(a, ct, ((1,), (0,)))


_bdot_tn.defvjp(_bdot_tn_fwd, _bdot_tn_bwd)


def _split3(a):
    hi = a.astype(BF16)
    r = a - hi.astype(F32)
    lo = r.astype(BF16)
    return hi, lo


def _dot3_raw(a, b, dims):
    ah, al = _split3(a)
    bh, bl = _split3(b)
    d = functools.partial(lax.dot_general, dimension_numbers=(dims, ((), ())), preferred_element_type=F32)
    return d(ah, bh) + (d(ah, bl) + d(al, bh))


@jax.custom_vjp
def _dot3(a, b):
    return _dot3_raw(a, b, ((1,), (0,)))


def _dot3_fwd(a, b):
    return _dot3(a, b), (a, b)


def _dot3_bwd(res, ct):
    a, b = res
    return _dot3_raw(ct, b, ((1,), (1,))), _dot3_raw(a, ct, ((0,), (0,)))


_dot3.defvjp(_dot3_fwd, _dot3_bwd)


def _mm(a, b, *, mode, name, bias=None, add=None, add_scale=1.0, out_dtype=F32, b_off=0, b_len=None):
    if mode == "nn":
        (m, k), (k2, n) = a.shape, (b.shape[0], b_len or b.shape[1])
    elif mode == "nt":
        (m, k), (n, k2) = a.shape, (b.shape[0], b_len or b.shape[1])
    else:
        (k, m), (k2, n) = a.shape, b.shape
        assert b_off == 0 and b_len is None
    assert k == k2, (a.shape, b.shape, mode)
    tn = _tile(n, 1408)
    tm = _tile(m, 1024 if tn >= 1024 else 2048)
    tk = _tile(k, 512) if k % 512 == 0 else _tile(k, 1408)
    nk = k // tk
    on, ok = (b_off // tn, 0) if mode == "nn" else (0, b_off // tk)
    assert b_off == (on * tn if mode == "nn" else ok * tk), (b_off, tn, tk)
    dims = {"nn": ((1,), (0,)), "nt": ((1,), (1,)), "tn": ((0,), (0,))}[mode]
    a_spec = pl.BlockSpec((tk, tm), lambda i, j, kk: (kk, i)) if mode == "tn" else pl.BlockSpec((tm, tk), lambda i, j, kk: (i, kk))
    b_spec = (pl.BlockSpec((tn, tk), lambda i, j, kk: (j, kk + ok)) if mode == "nt"
              else pl.BlockSpec((tk, tn), lambda i, j, kk: (kk, j + on)))
    in_specs = [a_spec, b_spec]
    operands = [a, b]
    if bias is not None:
        in_specs.append(pl.BlockSpec((1, tn), lambda i, j, kk: (0, j)))
        operands.append(bias.reshape(1, n))
    if add is not None:
        in_specs.append(pl.BlockSpec((tm, tn), lambda i, j, kk: (i, j)))
        operands.append(add)
    has_bias, has_add = bias is not None, add is not None

    def body(*refs):
        a_ref, b_ref = refs[0], refs[1]
        pos = 2
        bias_ref = add_ref = None
        if has_bias:
            bias_ref = refs[pos]
            pos += 1
        if has_add:
            add_ref = refs[pos]
            pos += 1
        o_ref, acc_ref = refs[pos], refs[pos + 1]
        kk = pl.program_id(2)

        @pl.when(kk == 0)
        def _():
            acc_ref[...] = jnp.zeros_like(acc_ref)

        acc_ref[...] += _raw_dot(a_ref[...], b_ref[...], dims)

        @pl.when(kk == nk - 1)
        def _():
            r = acc_ref[...]
            if has_bias:
                r = r + bias_ref[...]
            if has_add:
                r = r + add_scale * add_ref[...].astype(F32)
            o_ref[...] = r.astype(out_dtype)

    return pl.pallas_call(
        body, name=name, grid=(m // tm, n // tn, nk),
        in_specs=in_specs, out_specs=pl.BlockSpec((tm, tn), lambda i, j, kk: (i, j)),
        out_shape=jax.ShapeDtypeStruct((m, n), out_dtype),
        scratch_shapes=[pltpu.VMEM((tm, tn), F32)],
        compiler_params=_params(("parallel", "parallel", "arbitrary")),
    )(*operands)


def _ln_fwd(x, r, g, b, *, name):
    m, d = x.shape
    tm = 256

    def body(x_ref, r_ref, g_ref, b_ref, y_ref, yb_ref):
        s = DEEPNORM_ALPHA * x_ref[...] + r_ref[...]
        mu = jnp.mean(s, axis=-1, keepdims=True)
        c = s - mu
        var = jnp.mean(c * c, axis=-1, keepdims=True)
        y = c * lax.rsqrt(var + LN_EPS) * g_ref[...] + b_ref[...]
        y_ref[...] = y
        yb_ref[...] = y.astype(BF16)

    row = pl.BlockSpec((tm, d), lambda i: (i, 0))
    vec = pl.BlockSpec((1, d), lambda i: (0, 0))
    return pl.pallas_call(
        body, name=name, grid=(m // tm,), in_specs=[row, row, vec, vec], out_specs=[row, row],
        out_shape=[jax.ShapeDtypeStruct((m, d), F32), jax.ShapeDtypeStruct((m, d), BF16)],
        compiler_params=_params(("parallel",)),
    )(x, r, g.reshape(1, d), b.reshape(1, d))


def _ln_bwd(x, r, g, dy, *, name):
    m, d = x.shape
    tm = 256

    def body(x_ref, r_ref, g_ref, dy_ref, ds_ref, dsb_ref, dg_ref, db_ref):
        i = pl.program_id(0)
        s = DEEPNORM_ALPHA * x_ref[...] + r_ref[...]
        mu = jnp.mean(s, axis=-1, keepdims=True)
        c = s - mu
        var = jnp.mean(c * c, axis=-1, keepdims=True)
        rstd = lax.rsqrt(var + LN_EPS)
        xhat = c * rstd
        dy_ = dy_ref[...]
        dxhat = dy_ * g_ref[...]
        m1 = jnp.mean(dxhat, axis=-1, keepdims=True)
        m2 = jnp.mean(dxhat * xhat, axis=-1, keepdims=True)
        ds = rstd * (dxhat - m1 - xhat * m2)
        ds_ref[...] = ds
        dsb_ref[...] = ds.astype(BF16)

        @pl.when(i == 0)
        def _():
            dg_ref[...] = jnp.zeros_like(dg_ref)
            db_ref[...] = jnp.zeros_like(db_ref)

        dg_ref[...] += jnp.sum(dy_ * xhat, axis=0, keepdims=True)
        db_ref[...] += jnp.sum(dy_, axis=0, keepdims=True)

    row = pl.BlockSpec((tm, d), lambda i: (i, 0))
    vec = pl.BlockSpec((1, d), lambda i: (0, 0))
    return pl.pallas_call(
        body, name=name, grid=(m // tm,), in_specs=[row, row, vec, row], out_specs=[row, row, vec, vec],
        out_shape=[jax.ShapeDtypeStruct((m, d), F32), jax.ShapeDtypeStruct((m, d), BF16),
                   jax.ShapeDtypeStruct((1, d), F32), jax.ShapeDtypeStruct((1, d), F32)],
        compiler_params=_params(("arbitrary",)),
    )(x, r, g.reshape(1, d), dy)


def _loss_head(y, t):
    m, d = y.shape
    tm = 256

    def body(y_ref, t_ref, dy_ref, l_ref):
        i = pl.program_id(0)
        e = y_ref[...] - t_ref[...]
        dy_ref[...] = e * (1.0 / d)

        @pl.when(i == 0)
        def _():
            l_ref[...] = jnp.zeros_like(l_ref)

        part = jnp.sum(jnp.sum(e * e, axis=-1, keepdims=True) * (0.5 / d), axis=0, keepdims=True)
        l_ref[...] += jnp.broadcast_to(part, l_ref.shape)

    row = pl.BlockSpec((tm, d), lambda i: (i, 0))
    dy, l = pl.pallas_call(
        body, name="loss_head", grid=(m // tm,), in_specs=[row, row],
        out_specs=[row, pl.BlockSpec((SUBLANES, LANES), lambda i: (0, 0))],
        out_shape=[jax.ShapeDtypeStruct((m, d), F32), jax.ShapeDtypeStruct((SUBLANES, LANES), F32)],
        compiler_params=_params(("arbitrary",)),
    )(y, t)
    return dy, l[0, 0]


def _conv_ext(ext, w, width):
    acc = None
    for k in range(width):
        s = width - 1 - k
        xs = ext if s == 0 else pltpu.roll(ext, s, 0)
        term = xs[SUBLANES:, :] * w[k:k + 1, :]
        acc = term if acc is None else acc + term
    return acc


def _ffn_act_fwd(hug, huv, wg, wv, bg, bv):
    m, fp = hug.shape
    ts, tc = 512, 512
    hb = ts // SUBLANES

    def body(g_ref, gh_ref, v_ref, vh_ref, wg_ref, wv_ref, bg_ref, bv_ref, o_ref):
        i = pl.program_id(1)
        keep = (i > 0).astype(F32)
        hg = _conv_ext(jnp.concatenate([gh_ref[...] * keep, g_ref[...]], axis=0), wg_ref[...], FFN_CONV) + bg_ref[...]
        hv = _conv_ext(jnp.concatenate([vh_ref[...] * keep, v_ref[...]], axis=0), wv_ref[...], FFN_CONV) + bv_ref[...]
        o_ref[...] = (_silu(hg) * hv).astype(BF16)

    tile = pl.BlockSpec((ts, tc), lambda j, i: (i, j))
    halo = pl.BlockSpec((SUBLANES, tc), lambda j, i: (jnp.maximum(i * hb - 1, 0), j))
    wsp = pl.BlockSpec((FFN_CONV, tc), lambda j, i: (0, j))
    bsp = pl.BlockSpec((1, tc), lambda j, i: (0, j))
    return pl.pallas_call(
        body, name="ffn_act_fwd", grid=(fp // tc, m // ts),
        in_specs=[tile, halo, tile, halo, wsp, wsp, bsp, bsp], out_specs=tile,
        out_shape=jax.ShapeDtypeStruct((m, fp), BF16),
        compiler_params=_params(("parallel", "parallel")),
    )(hug, hug, huv, huv, wg, wv, bg, bv)


def _ffn_act_bwd(hug, huv, wg, wv, bg, bv, dact):
    m, fp = hug.shape
    ts, tc = 512, 512
    hb = ts // SUBLANES
    ni = m // ts

    def body(g_ref, gp_ref, gn_ref, v_ref, vp_ref, vn_ref, wg_ref, wv_ref, bg_ref, bv_ref, da_ref, dan_ref,
             dg_ref, dv_ref, dwg_ref, dwv_ref, dbg_ref, dbv_ref):
        i = pl.program_id(1)
        first = (i > 0).astype(F32)
        last = (i < ni - 1).astype(F32)
        wg_, wv_ = wg_ref[...], wv_ref[...]
        ext_g = jnp.concatenate([gp_ref[...] * first, g_ref[...], gn_ref[...]], axis=0)
        ext_v = jnp.concatenate([vp_ref[...] * first, v_ref[...], vn_ref[...]], axis=0)
        hg = _conv_ext(ext_g, wg_, FFN_CONV) + bg_ref[...]
        hv = _conv_ext(ext_v, wv_, FFN_CONV) + bv_ref[...]
        da = jnp.concatenate([da_ref[...], dan_ref[...] * last], axis=0)
        sg = _sigmoid(hg)
        dhg = da * hv * (sg * (1.0 + hg * (1.0 - sg)))
        dhv = da * (hg * sg)
        n_ext = ts + SUBLANES

        def back(dh, w_, ext):
            dx = None
            dw_rows = []
            for k in range(FFN_CONV):
                s = FFN_CONV - 1 - k
                sh = dh if s == 0 else pltpu.roll(dh, n_ext - s, 0)
                term = sh[:ts, :] * w_[k:k + 1, :]
                dx = term if dx is None else dx + term
                xs = ext if s == 0 else pltpu.roll(ext, s, 0)
                dw_rows.append(jnp.sum(dh[:ts, :] * xs[SUBLANES:SUBLANES + ts, :], axis=0, keepdims=True))
            return dx, jnp.concatenate(dw_rows, axis=0), jnp.sum(dh[:ts, :], axis=0, keepdims=True)

        dxg, dwg, dbg = back(dhg, wg_, ext_g)
        dxv, dwv, dbv = back(dhv, wv_, ext_v)
        dg_ref[...] = dxg.astype(BF16)
        dv_ref[...] = dxv.astype(BF16)

        @pl.when(i == 0)
        def _():
            dwg_ref[...] = jnp.zeros_like(dwg_ref)
            dwv_ref[...] = jnp.zeros_like(dwv_ref)
            dbg_ref[...] = jnp.zeros_like(dbg_ref)
            dbv_ref[...] = jnp.zeros_like(dbv_ref)

        dwg_ref[...] += dwg
        dwv_ref[...] += dwv
        dbg_ref[...] += dbg
        dbv_ref[...] += dbv

    tile = pl.BlockSpec((ts, tc), lambda j, i: (i, j))
    prev = pl.BlockSpec((SUBLANES, tc), lambda j, i: (jnp.maximum(i * hb - 1, 0), j))
    nxt = pl.BlockSpec((SUBLANES, tc), lambda j, i: (jnp.minimum((i + 1) * hb, m // SUBLANES - 1), j))
    wsp = pl.BlockSpec((FFN_CONV, tc), lambda j, i: (0, j))
    bsp = pl.BlockSpec((1, tc), lambda j, i: (0, j))
    return pl.pallas_call(
        body, name="ffn_act_bwd", grid=(fp // tc, ni),
        in_specs=[tile, prev, nxt, tile, prev, nxt, wsp, wsp, bsp, bsp, tile, nxt],
        out_specs=[tile, tile, wsp, wsp, bsp, bsp],
        out_shape=[jax.ShapeDtypeStruct((m, fp), BF16), jax.ShapeDtypeStruct((m, fp), BF16),
                   jax.ShapeDtypeStruct((FFN_CONV, fp), F32), jax.ShapeDtypeStruct((FFN_CONV, fp), F32),
                   jax.ShapeDtypeStruct((1, fp), F32), jax.ShapeDtypeStruct((1, fp), F32)],
        compiler_params=_params(("parallel", "arbitrary")),
    )(hug, hug, hug, huv, huv, huv, wg, wv, bg, bv, dact, dact)


def _log1p_small(e):
    u = 1.0 + e
    safe = jnp.where(u == 1.0, 1.0, u - 1.0)
    return jnp.where(u == 1.0, e, jnp.log(u) * (e / safe))


def _log_sigmoid(x):
    return jnp.minimum(x, 0.0) - _log1p_small(jnp.exp(-jnp.abs(x)))


def _exact_tri_dot(tri, x):
    p1 = x.astype(BF16)
    r1 = x - p1.astype(F32)
    p2 = r1.astype(BF16)
    p3 = (r1 - p2.astype(F32)).astype(BF16)
    d = lambda p: lax.dot_general(tri, p, (((1,), (0,)), ((), ())), preferred_element_type=F32)
    return d(p1) + (d(p2) + d(p3))


def _fox_prefix_fwd(small):
    m, w = small.shape
    tb = 512
    nb = m // tb

    def body(f_ref, c_ref, carry_ref):
        i = pl.program_id(0)

        @pl.when(i == 0)
        def _():
            carry_ref[...] = jnp.zeros_like(carry_ref)

        lf = _log_sigmoid(f_ref[...])
        r = lax.broadcasted_iota(jnp.int32, (tb, tb), 0)
        c = lax.broadcasted_iota(jnp.int32, (tb, tb), 1)
        tri = (c <= r).astype(BF16)
        c_ref[...] = _exact_tri_dot(tri, lf) + carry_ref[0:1, :]
        carry_ref[0:1, :] += jnp.sum(lf, axis=0, keepdims=True)

    blk = pl.BlockSpec((tb, w), lambda i: (i, 0))
    return pl.pallas_call(
        body, name="fox_prefix_fwd", grid=(nb,), in_specs=[blk], out_specs=blk,
        out_shape=jax.ShapeDtypeStruct((m, w), F32), scratch_shapes=[pltpu.VMEM((SUBLANES, w), F32)],
        compiler_params=_params(("arbitrary",)),
    )(small)


def _fox_prefix_bwd(small, dc):
    m, w = small.shape
    tb = 512
    nb = m // tb

    def body(f_ref, dc_ref, df_ref, carry_ref):
        i = pl.program_id(0)

        @pl.when(i == 0)
        def _():
            carry_ref[...] = jnp.zeros_like(carry_ref)

        d = dc_ref[...]
        r = lax.broadcasted_iota(jnp.int32, (tb, tb), 0)
        c = lax.broadcasted_iota(jnp.int32, (tb, tb), 1)
        tri = (c >= r).astype(BF16)
        dlf = _exact_tri_dot(tri, d) + carry_ref[0:1, :]
        carry_ref[0:1, :] += jnp.sum(d, axis=0, keepdims=True)
        lane = lax.broadcasted_iota(jnp.int32, (tb, w), 1)
        df_ref[...] = jnp.where(lane < N_HEADS, dlf * _sigmoid(-f_ref[...]), 0.0)

    blk = pl.BlockSpec((tb, w), lambda i: (nb - 1 - i, 0))
    return pl.pallas_call(
        body, name="fox_prefix_bwd", grid=(nb,), in_specs=[blk, blk], out_specs=blk,
        out_shape=jax.ShapeDtypeStruct((m, w), F32), scratch_shapes=[pltpu.VMEM((SUBLANES, w), F32)],
        compiler_params=_params(("arbitrary",)),
    )(small, dc)


ATT_T = 512
NEG = -1e30


def _lane_pick(block, h):
    lane = lax.broadcasted_iota(jnp.int32, block.shape, 1)
    return jnp.sum(jnp.where(lane == h, block, 0.0), axis=1, keepdims=True)


def _att_scores(q, k, cq, ck):
    return _raw_dot(q, k, ((1,), (1,))) * (HEAD_DIM ** -0.5) + cq - ck


def _att_probs(s, lse, diag):
    p = jnp.exp(s - lse)
    if not diag:
        return p
    t = s.shape[0]
    return jnp.where(lax.broadcasted_iota(jnp.int32, (t, t), 0) >= lax.broadcasted_iota(jnp.int32, (t, t), 1), p, 0.0)


def _fox_fwd(q, k, v, c_col, c_row):
    m = q.shape[0]
    t = ATT_T
    nb = m // t

    def body(q_ref, k_ref, v_ref, cc_ref, cr_ref, o_ref, o32_ref, lse_ref):
        h, i = pl.program_id(0), pl.program_id(1)
        q = q_ref[...]
        cq = _lane_pick(cc_ref[...], h)

        def step(j, carry, diag):
            mx, l, acc = carry
            rows = pl.ds(pl.multiple_of(j * t, t), t)
            s = _att_scores(q, k_ref[rows, :], cq, cr_ref[h * nb + j])
            if diag:
                s = jnp.where(lax.broadcasted_iota(jnp.int32, (t, t), 0) >= lax.broadcasted_iota(jnp.int32, (t, t), 1), s, NEG)
            mn = jnp.maximum(mx, jnp.max(s, axis=1, keepdims=True))
            p = jnp.exp(s - mn)
            a = jnp.exp(mx - mn)
            return mn, a * l + jnp.sum(p, axis=1, keepdims=True), a * acc + _raw_dot(p, v_ref[rows, :], ((1,), (0,)))

        init = (jnp.full((t, 1), NEG, F32), jnp.zeros((t, 1), F32), jnp.zeros((t, HEAD_DIM), F32))
        mx, l, acc = step(i, lax.fori_loop(0, i, functools.partial(step, diag=False), init), True)
        o = acc / l
        o_ref[...] = o.astype(BF16)
        o32_ref[...] = o
        lse_ref[0] = jnp.broadcast_to(mx + jnp.log(l), (t, LANES))

    return pl.pallas_call(
        body, name="fox_fwd", grid=(N_HEADS, nb),
        in_specs=[pl.BlockSpec((t, HEAD_DIM), lambda h, i: (i, h)),
                  pl.BlockSpec((m, HEAD_DIM), lambda h, i: (0, h)),
                  pl.BlockSpec((m, HEAD_DIM), lambda h, i: (0, h)),
                  pl.BlockSpec((t, LANES), lambda h, i: (i, 0)),
                  pl.BlockSpec((N_HEADS * nb, 1, t), lambda h, i: (0, 0, 0))],
        out_specs=[pl.BlockSpec((t, HEAD_DIM), lambda h, i: (i, h)), pl.BlockSpec((t, HEAD_DIM), lambda h, i: (i, h)),
                   pl.BlockSpec((1, t, LANES), lambda h, i: (h, i, 0))],
        out_shape=[jax.ShapeDtypeStruct((m, WIDTH), BF16), jax.ShapeDtypeStruct((m, WIDTH), F32),
                   jax.ShapeDtypeStruct((N_HEADS, m, LANES), F32)],
        compiler_params=_params(("parallel", "parallel")),
    )(q, k, v, c_col, c_row)


def _fox_bwd_kv(q, k, v, o, do, lse, c_col, c_row):
    m = q.shape[0]
    t = ATT_T
    nb = m // t

    def body(q_ref, k_ref, v_ref, o_ref, do_ref, lse_ref, cc_ref, cr_ref, dk_ref, dv_ref, dc_ref):
        h, j = pl.program_id(0), pl.program_id(1)
        k, v = k_ref[...], v_ref[...]
        ck = cr_ref[h * nb + j]

        def step(i, carry, diag):
            dk, dv, dc = carry
            rows = pl.ds(pl.multiple_of(i * t, t), t)
            q, do_ = q_ref[rows, :], do_ref[rows, :]
            cq = _lane_pick(cc_ref[rows, :], h)
            lse_i = jnp.max(lse_ref[0, rows, :], axis=1, keepdims=True)
            dd = jnp.sum(do_.astype(F32) * o_ref[rows, :].astype(F32), axis=1, keepdims=True)
            p = _att_probs(_att_scores(q, k, cq, ck), lse_i, diag)
            dp = _raw_dot(do_, v, ((1,), (1,)))
            ds = p * (dp - dd)
            return (dk + _raw_dot(ds, q, ((0,), (0,))), dv + _raw_dot(p, do_, ((0,), (0,))),
                    dc - jnp.sum(ds, axis=0, keepdims=True))

        init = (jnp.zeros((t, HEAD_DIM), F32), jnp.zeros((t, HEAD_DIM), F32), jnp.zeros((1, t), F32))
        dk, dv, dc = lax.fori_loop(j + 1, nb, functools.partial(step, diag=False), step(j, init, True))
        dk_ref[...] = (dk * (HEAD_DIM ** -0.5)).astype(BF16)
        dv_ref[...] = dv.astype(BF16)
        dc_ref[0] = dc

    whole = lambda col: pl.BlockSpec((m, HEAD_DIM), col)
    return pl.pallas_call(
        body, name="fox_bwd_kv", grid=(N_HEADS, nb),
        in_specs=[whole(lambda h, j: (0, h)),
                  pl.BlockSpec((t, HEAD_DIM), lambda h, j: (j, h)),
                  pl.BlockSpec((t, HEAD_DIM), lambda h, j: (j, h)),
                  whole(lambda h, j: (0, h)), whole(lambda h, j: (0, h)),
                  pl.BlockSpec((1, m, LANES), lambda h, j: (h, 0, 0)),
                  pl.BlockSpec((m, LANES), lambda h, j: (0, 0)),
                  pl.BlockSpec((N_HEADS * nb, 1, t), lambda h, j: (0, 0, 0))],
        out_specs=[pl.BlockSpec((t, HEAD_DIM), lambda h, j: (j, h)),
                   pl.BlockSpec((t, HEAD_DIM), lambda h, j: (j, h)),
                   pl.BlockSpec((1, 1, t), lambda h, j: (h * nb + j, 0, 0))],
        out_shape=[jax.ShapeDtypeStruct((m, WIDTH), BF16), jax.ShapeDtypeStruct((m, WIDTH), BF16),
                   jax.ShapeDtypeStruct((N_HEADS * nb, 1, t), F32)],
        compiler_params=_params(("parallel", "parallel")),
    )(q, k, v, o, do, lse, c_col, c_row)


def _fox_bwd_q(q, k, v, o, do, lse, c_col, c_row):
    m = q.shape[0]
    t = ATT_T
    nb = m // t

    def body(q_ref, k_ref, v_ref, o_ref, do_ref, lse_ref, cc_ref, cr_ref, dq_ref, dcq_ref):
        h, i = pl.program_id(0), pl.program_id(1)
        q, do_ = q_ref[...], do_ref[...]
        cq = _lane_pick(cc_ref[...], h)
        lse_i = jnp.max(lse_ref[0], axis=1, keepdims=True)
        dd = jnp.sum(do_.astype(F32) * o_ref[...].astype(F32), axis=1, keepdims=True)

        def step(j, carry, diag):
            dq, dcq = carry
            rows = pl.ds(pl.multiple_of(j * t, t), t)
            k = k_ref[rows, :]
            p = _att_probs(_att_scores(q, k, cq, cr_ref[h * nb + j]), lse_i, diag)
            ds = p * (_raw_dot(do_, v_ref[rows, :], ((1,), (1,))) - dd)
            return dq + _raw_dot(ds, k, ((1,), (0,))), dcq + jnp.sum(ds, axis=1, keepdims=True)

        init = (jnp.zeros((t, HEAD_DIM), F32), jnp.zeros((t, 1), F32))
        dq, dcq = step(i, lax.fori_loop(0, i, functools.partial(step, diag=False), init), True)
        dq_ref[...] = (dq * (HEAD_DIM ** -0.5)).astype(BF16)
        dcq_ref[0] = jnp.broadcast_to(dcq, (t, LANES))

    tile = lambda col: pl.BlockSpec((t, HEAD_DIM), col)
    return pl.pallas_call(
        body, name="fox_bwd_q", grid=(N_HEADS, nb),
        in_specs=[tile(lambda h, i: (i, h)),
                  pl.BlockSpec((m, HEAD_DIM), lambda h, i: (0, h)),
                  pl.BlockSpec((m, HEAD_DIM), lambda h, i: (0, h)),
                  tile(lambda h, i: (i, h)), tile(lambda h, i: (i, h)),
                  pl.BlockSpec((1, t, LANES), lambda h, i: (h, i, 0)),
                  pl.BlockSpec((t, LANES), lambda h, i: (i, 0)),
                  pl.BlockSpec((N_HEADS * nb, 1, t), lambda h, i: (0, 0, 0))],
        out_specs=[tile(lambda h, i: (i, h)), pl.BlockSpec((1, t, LANES), lambda h, i: (h, i, 0))],
        out_shape=[jax.ShapeDtypeStruct((m, WIDTH), BF16), jax.ShapeDtypeStruct((N_HEADS, m, LANES), F32)],
        compiler_params=_params(("parallel", "parallel")),
    )(q, k, v, o, do, lse, c_col, c_row)


def _row_form(c_col, t):
    m = c_col.shape[0]
    return c_col[:, :N_HEADS].T.reshape(N_HEADS * (m // t), 1, t)


def _col_form(c_row, m):
    c = c_row.reshape(N_HEADS, m).T
    return jnp.pad(c, ((0, 0), (0, LANES - N_HEADS)))


SGU_T = 512


def _sgu_tile(u, v, lg, lb, w, bsb):
    r = lax.broadcasted_iota(jnp.int32, (SGU_SPAN, SGU_SPAN), 0) // CHUNK
    c = lax.broadcasted_iota(jnp.int32, (SGU_SPAN, SGU_SPAN), 1) // CHUNK
    wm = jnp.where(r >= c, w, 0.0)
    outs = []
    for n in range(u.shape[0] // SGU_SPAN):
        rows = slice(n * SGU_SPAN, (n + 1) * SGU_SPAN)
        vs = v[rows]
        mu = jnp.mean(vs, axis=-1, keepdims=True)
        d = vs - mu
        var = jnp.mean(d * d, axis=-1, keepdims=True)
        vg = d * lax.rsqrt(var + LN_EPS) * lg + lb
        outs.append(u[rows] * (_bdot(wm, vg) + bsb))
    return jnp.concatenate(outs, axis=0)


def _sgu_specs(m):
    t = SGU_T
    tile = lambda off: pl.BlockSpec((t, HEAD_DIM), lambda g, i: (i, off + g))
    vec = pl.BlockSpec((1, 1, HEAD_DIM), lambda g, i: (g, 0, 0))
    mat = pl.BlockSpec((1, SGU_SPAN, SGU_SPAN), lambda g, i: (g, 0, 0))
    return tile, vec, mat


def _sgu_operands(ln_g, ln_b, w, b):
    return (ln_g.reshape(N_HEADS, 1, HEAD_DIM), ln_b.reshape(N_HEADS, 1, HEAD_DIM), w,
            jnp.broadcast_to(b[:, :, None], (N_HEADS, SGU_SPAN, SGU_SPAN)))


def _sgu_fwd(u, v, ln_g, ln_b, w, b):
    m = u.shape[0]
    tile, vec, mat = _sgu_specs(m)

    def body(u_ref, v_ref, lg_ref, lb_ref, w_ref, b_ref, o_ref):
        o_ref[...] = _sgu_tile(u_ref[...], v_ref[...], lg_ref[0], lb_ref[0], w_ref[0], b_ref[0]).astype(BF16)

    return pl.pallas_call(
        body, name="sgu_fwd", grid=(N_HEADS, m // SGU_T),
        in_specs=[tile(0), tile(0), vec, vec, mat, mat], out_specs=tile(0),
        out_shape=jax.ShapeDtypeStruct((m, WIDTH), BF16),
        compiler_params=_params(("parallel", "parallel")),
    )(u, v, *_sgu_operands(ln_g, ln_b, w, b))


def _sgu_bwd(u, v, ln_g, ln_b, w, b, dy):
    m = u.shape[0]
    tile, vec, mat = _sgu_specs(m)

    def body(u_ref, v_ref, lg_ref, lb_ref, w_ref, b_ref, dy_ref, du_ref, dv_ref, dlg_ref, dlb_ref, dw_ref, db_ref):
        i = pl.program_id(1)
        _, vjp = jax.vjp(_sgu_tile, u_ref[...], v_ref[...], lg_ref[0], lb_ref[0], w_ref[0], b_ref[0])
        du, dv, dlg, dlb, dw, dbsb = vjp(dy_ref[...])
        du_ref[...] = du.astype(BF16)
        dv_ref[...] = dv.astype(BF16)

        @pl.when(i == 0)
        def _():
            dlg_ref[...] = jnp.zeros_like(dlg_ref)
            dlb_ref[...] = jnp.zeros_like(dlb_ref)
            dw_ref[...] = jnp.zeros_like(dw_ref)
            db_ref[...] = jnp.zeros_like(db_ref)

        dlg_ref[0] += dlg
        dlb_ref[0] += dlb
        dw_ref[0] += dw
        db_ref[0] += jnp.broadcast_to(jnp.sum(dbsb, axis=1, keepdims=True), (SGU_SPAN, SGU_SPAN))

    vshape = jax.ShapeDtypeStruct((N_HEADS, 1, HEAD_DIM), F32)
    mshape = jax.ShapeDtypeStruct((N_HEADS, SGU_SPAN, SGU_SPAN), F32)
    return pl.pallas_call(
        body, name="sgu_bwd", grid=(N_HEADS, m // SGU_T),
        in_specs=[tile(0), tile(0), vec, vec, mat, mat, tile(0)],
        out_specs=[tile(0), tile(0), vec, vec, mat, mat],
        out_shape=[jax.ShapeDtypeStruct((m, WIDTH), BF16), jax.ShapeDtypeStruct((m, WIDTH), BF16),
                   vshape, vshape, mshape, mshape],
        compiler_params=_params(("parallel", "arbitrary")),
    )(u, v, *_sgu_operands(ln_g, ln_b, w, b), dy)


def _merge_fwd(ys, ws, gates):
    m, d = gates[0].shape
    tm, tn = 512, _tile(d, 512)

    def body(ya, yb, yc, wa, wb, wc, ga, gb, gc, o_ref, za, zb, zc):
        acc = None
        for y_ref, w_ref, g_ref, z_ref in ((ya, wa, ga, za), (yb, wb, gb, zb), (yc, wc, gc, zc)):
            z = _raw_dot(y_ref[...], w_ref[...], ((1,), (0,)))
            z_ref[...] = z
            term = _sigmoid(g_ref[...]) * z
            acc = term if acc is None else acc + term
        o_ref[...] = acc.astype(BF16)

    ysp = pl.BlockSpec((tm, WIDTH), lambda i, j: (i, 0))
    wsp = pl.BlockSpec((WIDTH, tn), lambda i, j: (0, j))
    tsp = pl.BlockSpec((tm, tn), lambda i, j: (i, j))
    zs = jax.ShapeDtypeStruct((m, d), F32)
    return pl.pallas_call(
        body, name="merge_fwd", grid=(m // tm, d // tn),
        in_specs=[ysp] * 3 + [wsp] * 3 + [tsp] * 3, out_specs=[tsp] * 4,
        out_shape=[jax.ShapeDtypeStruct((m, d), BF16), zs, zs, zs],
        compiler_params=_params(("parallel", "parallel")),
    )(*ys, *ws, *gates)


def _merge_bwd(dmerged, zs, gates):
    m, d = dmerged.shape
    tm, tn = 512, _tile(d, 1024)

    def body(dm_ref, za, zb, zc, ga, gb, gc, dza, dzb, dzc, dga, dgb, dgc):
        dm = dm_ref[...]
        for z_ref, g_ref, dz_ref, dg_ref in ((za, ga, dza, dga), (zb, gb, dzb, dgb), (zc, gc, dzc, dgc)):
            s = _sigmoid(g_ref[...])
            dz_ref[...] = (dm * s).astype(BF16)
            dg_ref[...] = (dm * z_ref[...] * (s * (1.0 - s))).astype(BF16)

    tsp = pl.BlockSpec((tm, tn), lambda i, j: (i, j))
    bs = jax.ShapeDtypeStruct((m, d), BF16)
    return pl.pallas_call(
        body, name="merge_bwd", grid=(m // tm, d // tn),
        in_specs=[tsp] * 7, out_specs=[tsp] * 6, out_shape=[bs] * 6,
        compiler_params=_params(("parallel", "parallel")),
    )(dmerged, *zs, *gates)


GDN_T = 512
GDN_A_LANE = N_HEADS
GDN_B_LANE = 2 * N_HEADS


@functools.partial(jax.custom_vjp, nondiff_argnums=(2,))
def _conv_halo(ext, w, width):
    return _conv_ext(ext, w, width)


def _conv_halo_fwd(ext, w, width):
    return _conv_ext(ext, w, width), (ext, w)


def _conv_halo_bwd(width, res, dy):
    ext, w = res
    n = ext.shape[0]
    dyp = jnp.concatenate([jnp.zeros((SUBLANES, dy.shape[1]), dy.dtype), dy], axis=0)
    dext = None
    dw_rows = []
    for k in range(width):
        s = width - 1 - k
        sh = dyp if s == 0 else pltpu.roll(dyp, n - s, 0)
        term = sh * w[k:k + 1, :]
        dext = term if dext is None else dext + term
        xs = ext if s == 0 else pltpu.roll(ext, s, 0)
        dw_rows.append(jnp.sum(dy * xs[SUBLANES:, :], axis=0, keepdims=True))
    return dext, jnp.concatenate(dw_rows, axis=0)


_conv_halo.defvjp(_conv_halo_fwd, _conv_halo_bwd)


@jax.custom_jvp
def _softplus(x):
    return jnp.maximum(x, 0.0) + _log1p_small(jnp.exp(-jnp.abs(x)))


@_softplus.defjvp
def _softplus_jvp(primals, tangents):
    (x,), (t,) = primals, tangents
    return _softplus(x), t * _sigmoid(x)


def _tri_inv(a, ii, jj):
    eye = (ii == jj).astype(F32)
    same = (ii // 16) == (jj // 16)
    ad = jnp.where(same, a, 0.0)
    ao = a - ad
    b1 = -ad
    b2 = _dot3(b1, b1)
    b4 = _dot3(b2, b2)
    b8 = _dot3(b4, b4)
    dinv = _dot3(_dot3(_dot3(eye + b1, eye + b2), eye + b4), eye + b8)
    n1 = _dot3(dinv, ao)
    n2 = _dot3(n1, n1)
    return _dot3(_dot3(eye - n1, eye + n2), dinv)


def _gdn_chunk(s_in, eq, ek, ev, a_col, b_col, gate, wq, wk, wv, a_log, dt_bias, ng):
    q = _silu(_conv_halo(eq, wq, GDN_CONV))
    k = _silu(_conv_halo(ek, wk, GDN_CONV))
    v = _silu(_conv_halo(ev, wv, GDN_CONV))
    q = q * lax.rsqrt(jnp.sum(q * q, axis=-1, keepdims=True) + RMS_EPS) * (HEAD_DIM ** -0.5)
    k = k * lax.rsqrt(jnp.sum(k * k, axis=-1, keepdims=True) + RMS_EPS)
    g = -jnp.exp(a_log) * _softplus(a_col + dt_bias)
    beta = _sigmoid(b_col)
    ii = lax.broadcasted_iota(jnp.int32, (CHUNK, CHUNK), 0)
    jj = lax.broadcasted_iota(jnp.int32, (CHUNK, CHUNK), 1)
    gb = jnp.broadcast_to(g, (CHUNK, CHUNK))
    g_row = jnp.sum(jnp.where(ii == jj, gb, 0.0), axis=0, keepdims=True)
    gc_row = jnp.sum(jnp.where(ii <= jj, gb, 0.0), axis=0, keepdims=True)
    gc_col = jnp.sum(jnp.where(jj <= ii, jnp.broadcast_to(g_row, (CHUNK, CHUNK)), 0.0), axis=1, keepdims=True)
    g_last = jnp.sum(g, axis=0, keepdims=True)
    causal = ii >= jj
    decay = jnp.where(causal, jnp.exp(jnp.where(causal, gc_col - gc_row, 0.0)), 0.0)
    kb = k * beta
    a_kk = jnp.where(ii > jj, _bdot_nt(kb, k) * decay, 0.0)
    t_inv = _tri_inv(a_kk, ii, jj)
    u = _dot3(t_inv, v * beta)
    w = _dot3(t_inv, kb * jnp.exp(gc_col))
    qk = jnp.where(causal, _bdot_nt(q, k) * decay, 0.0)
    k_dec = k * jnp.exp(g_last - gc_col)
    q_dec = q * jnp.exp(gc_col)
    v_new = u - _bdot(w, s_in)
    o = _bdot(q_dec, s_in) + _bdot(qk, v_new)
    s_out = s_in * jnp.exp(g_last) + _bdot_tn(k_dec, v_new)
    o = o * lax.rsqrt(jnp.mean(o * o, axis=-1, keepdims=True) + RMS_EPS) * ng
    return s_out, o * _silu(gate)


GDN_HB = 4
GDN_W = GDN_HB * HEAD_DIM
GDN_LANES = tuple(slice(i * HEAD_DIM, (i + 1) * HEAD_DIM) for i in range(GDN_HB))


def _gdn_specs(m, rev):
    t = GDN_T
    ns = m // t
    hb = t // SUBLANES
    pos = (lambda s: ns - 1 - s) if rev else (lambda s: s)
    tile = pl.BlockSpec((t, GDN_W), lambda h, s: (pos(s), h))
    halo = pl.BlockSpec((SUBLANES, GDN_W), lambda h, s: (jnp.maximum(pos(s) * hb - 1, 0), h))
    small = pl.BlockSpec((t, LANES), lambda h, s: (pos(s), 0))
    wsp = pl.BlockSpec((GDN_CONV, GDN_W), lambda h, s: (0, h))
    sc = pl.BlockSpec((GDN_HB, 1, LANES), lambda h, s: (h, 0, 0))
    ngs = pl.BlockSpec((1, HEAD_DIM), lambda h, s: (0, 0))
    st = pl.BlockSpec((GDN_HB, t // CHUNK, HEAD_DIM, HEAD_DIM), lambda h, s: (h, pos(s), 0, 0))
    return tile, halo, small, wsp, sc, ngs, st


def _gdn_scalars(a_log, dt_bias):
    bc = lambda p: jnp.broadcast_to(p.reshape(N_HEADS, 1, 1), (N_HEADS, 1, LANES))
    return bc(a_log), bc(dt_bias)


def _gdn_fill_ext(xpad_ref, tiles, halos, keep):
    for sec in range(3):
        xpad_ref[sec, 0:SUBLANES, :] = halos[sec][...] * keep
        xpad_ref[sec, SUBLANES:, :] = tiles[sec][...]


def _gdn_fwd(xq, xk, xv, small, gate, wq, wk, wv, a_log, dt_bias, norm_g):
    m = xq.shape[0]
    t = GDN_T
    nc = t // CHUNK
    tile, halo, smallsp, wsp, sc, ngs, st = _gdn_specs(m, False)

    def body(q_ref, qh_ref, k_ref, kh_ref, v_ref, vh_ref, sm_ref, gate_ref, wq_ref, wk_ref, wv_ref,
             al_ref, dt_ref, ng_ref, y_ref, st_ref, xpad_ref, s_ref):
        hg, s = pl.program_id(0), pl.program_id(1)
        _gdn_fill_ext(xpad_ref, (q_ref, k_ref, v_ref), (qh_ref, kh_ref, vh_ref), (s > 0).astype(F32))

        @pl.when(s == 0)
        def _():
            s_ref[...] = jnp.zeros_like(s_ref)

        ng = ng_ref[...]
        par = [(wq_ref[:, ls], wk_ref[:, ls], wv_ref[:, ls], al_ref[i][:, 0:1], dt_ref[i][:, 0:1])
               for i, ls in enumerate(GDN_LANES)]

        def step(c, states):
            off = pl.multiple_of(c * CHUNK, CHUNK)
            ext = pl.ds(off, CHUNK + SUBLANES)
            sm = sm_ref[pl.ds(off, CHUNK), :]
            ins = [(xpad_ref[0, ext, ls], xpad_ref[1, ext, ls], xpad_ref[2, ext, ls],
                    _lane_pick(sm, GDN_A_LANE + hg * GDN_HB + i), _lane_pick(sm, GDN_B_LANE + hg * GDN_HB + i),
                    gate_ref[pl.ds(off, CHUNK), ls]) for i, ls in enumerate(GDN_LANES)]
            outs = [_gdn_chunk(states[i], *ins[i], *par[i], ng) for i in range(GDN_HB)]
            for i, ls in enumerate(GDN_LANES):
                st_ref[i, c] = states[i]
                y_ref[pl.ds(off, CHUNK), ls] = outs[i][1].astype(BF16)
            return tuple(o[0] for o in outs)

        states = lax.fori_loop(0, nc, step, tuple(s_ref[i] for i in range(GDN_HB)))
        for i in range(GDN_HB):
            s_ref[i] = states[i]

    return pl.pallas_call(
        body, name="gdn_fwd", grid=(N_HEADS // GDN_HB, m // t),
        in_specs=[tile, halo, tile, halo, tile, halo, smallsp, tile, wsp, wsp, wsp, sc, sc, ngs],
        out_specs=[tile, st],
        out_shape=[jax.ShapeDtypeStruct((m, WIDTH), BF16),
                   jax.ShapeDtypeStruct((N_HEADS, m // CHUNK, HEAD_DIM, HEAD_DIM), F32)],
        scratch_shapes=[pltpu.VMEM((3, t + SUBLANES, GDN_W), F32), pltpu.VMEM((GDN_HB, HEAD_DIM, HEAD_DIM), F32)],
        compiler_params=_params(("parallel", "arbitrary")),
    )(xq, xq, xk, xk, xv, xv, small, gate, wq, wk, wv, *_gdn_scalars(a_log, dt_bias), norm_g.reshape(1, HEAD_DIM))


def _gdn_bwd(xq, xk, xv, small, gate, wq, wk, wv, a_log, dt_bias, norm_g, states, dy):
    m = xq.shape[0]
    t = GDN_T
    nc = t // CHUNK
    tile, halo, smallsp, wsp, sc, ngs, st = _gdn_specs(m, True)

    def body(q_ref, qh_ref, k_ref, kh_ref, v_ref, vh_ref, sm_ref, gate_ref, wq_ref, wk_ref, wv_ref,
             al_ref, dt_ref, ng_ref, st_ref, dy_ref,
             dq_ref, dk_ref, dv_ref, dgate_ref, dab_ref, dwq_ref, dwk_ref, dwv_ref, dal_ref, ddt_ref, dng_ref,
             xpad_ref, dxpad_ref, ds_ref, carry_ref):
        hg, s = pl.program_id(0), pl.program_id(1)
        _gdn_fill_ext(xpad_ref, (q_ref, k_ref, v_ref), (qh_ref, kh_ref, vh_ref), (s < m // t - 1).astype(F32))

        @pl.when(s == 0)
        def _():
            ds_ref[...] = jnp.zeros_like(ds_ref)
            carry_ref[...] = jnp.zeros_like(carry_ref)
            for r in (dwq_ref, dwk_ref, dwv_ref, dal_ref, ddt_ref):
                r[...] = jnp.zeros_like(r)

        @pl.when((s == 0) & (hg == 0))
        def _():
            dng_ref[...] = jnp.zeros_like(dng_ref)

        dxpad_ref[:, 0:t, :] = jnp.zeros((3, t, GDN_W), F32)
        dxpad_ref[:, t:, :] = carry_ref[...]
        ng = ng_ref[...]
        par = [(wq_ref[:, ls], wk_ref[:, ls], wv_ref[:, ls], al_ref[i][:, 0:1], dt_ref[i][:, 0:1])
               for i, ls in enumerate(GDN_LANES)]
        lane = lax.broadcasted_iota(jnp.int32, (CHUNK, LANES), 1)

        def step(cc, carry):
            c = nc - 1 - cc
            off = pl.multiple_of(c * CHUNK, CHUNK)
            ext = pl.ds(off, CHUNK + SUBLANES)
            sm = sm_ref[pl.ds(off, CHUNK), :]
            ins = [(st_ref[i, c], xpad_ref[0, ext, ls], xpad_ref[1, ext, ls], xpad_ref[2, ext, ls],
                    _lane_pick(sm, GDN_A_LANE + hg * GDN_HB + i), _lane_pick(sm, GDN_B_LANE + hg * GDN_HB + i),
                    gate_ref[pl.ds(off, CHUNK), ls]) for i, ls in enumerate(GDN_LANES)]
            dys = [dy_ref[pl.ds(off, CHUNK), ls] for ls in GDN_LANES]
            gs = []
            for i in range(GDN_HB):
                _, vjp = jax.vjp(_gdn_chunk, *ins[i], *par[i], ng)
                gs.append(vjp((carry[i][0], dys[i])))
            for i, ls in enumerate(GDN_LANES):
                g = gs[i]
                for sec in range(3):
                    dxpad_ref[sec, ext, ls] += g[1 + sec]
                dab_ref[i, pl.ds(off, CHUNK), :] = jnp.where(lane == 0, g[4], jnp.where(lane == 1, g[5], 0.0))
                dgate_ref[pl.ds(off, CHUNK), ls] = g[6].astype(BF16)
            return tuple((gs[i][0], tuple(a + b for a, b in zip(carry[i][1], gs[i][7:]))) for i in range(GDN_HB))

        zero = (jnp.zeros((GDN_CONV, HEAD_DIM), F32),) * 3 + (jnp.zeros((1, 1), F32),) * 2 + (jnp.zeros((1, HEAD_DIM), F32),)
        done = lax.fori_loop(0, nc, step, tuple((ds_ref[i], zero) for i in range(GDN_HB)))
        carry_ref[...] = dxpad_ref[:, 0:SUBLANES, :]
        dq_ref[...] = dxpad_ref[0, SUBLANES:, :].astype(BF16)
        dk_ref[...] = dxpad_ref[1, SUBLANES:, :].astype(BF16)
        dv_ref[...] = dxpad_ref[2, SUBLANES:, :].astype(BF16)
        for i, ls in enumerate(GDN_LANES):
            d_state, acc = done[i]
            ds_ref[i] = d_state
            dwq_ref[:, ls] += acc[0]
            dwk_ref[:, ls] += acc[1]
            dwv_ref[:, ls] += acc[2]
            dal_ref[i] += jnp.broadcast_to(acc[3], (1, LANES))
            ddt_ref[i] += jnp.broadcast_to(acc[4], (1, LANES))
            dng_ref[...] += acc[5]

    bs = jax.ShapeDtypeStruct((m, WIDTH), BF16)
    ws = jax.ShapeDtypeStruct((GDN_CONV, WIDTH), F32)
    ss = jax.ShapeDtypeStruct((N_HEADS, 1, LANES), F32)
    ns = m // t
    dabsp = pl.BlockSpec((GDN_HB, t, LANES), lambda h, s: (h, ns - 1 - s, 0))
    return pl.pallas_call(
        body, name="gdn_bwd", grid=(N_HEADS // GDN_HB, ns),
        in_specs=[tile, halo, tile, halo, tile, halo, smallsp, tile, wsp, wsp, wsp, sc, sc, ngs, st, tile],
        out_specs=[tile, tile, tile, tile, dabsp, wsp, wsp, wsp, sc, sc, ngs],
        out_shape=[bs, bs, bs, bs, jax.ShapeDtypeStruct((N_HEADS, m, LANES), F32), ws, ws, ws, ss, ss,
                   jax.ShapeDtypeStruct((1, HEAD_DIM), F32)],
        scratch_shapes=[pltpu.VMEM((3, t + SUBLANES, GDN_W), F32), pltpu.VMEM((3, t + SUBLANES, GDN_W), F32),
                        pltpu.VMEM((GDN_HB, HEAD_DIM, HEAD_DIM), F32), pltpu.VMEM((3, SUBLANES, GDN_W), F32)],
        compiler_params=_params(("arbitrary", "arbitrary")),
    )(xq, xq, xk, xk, xv, xv, small, gate, wq, wk, wv, *_gdn_scalars(a_log, dt_bias), norm_g.reshape(1, HEAD_DIM),
      states, dy)


def _colsum(a, name):
    m, n = a.shape
    tm, tn = 512, _tile(n, 1408)

    def body(a_ref, o_ref):
        @pl.when(pl.program_id(1) == 0)
        def _():
            o_ref[...] = jnp.zeros_like(o_ref)

        o_ref[...] += jnp.sum(a_ref[...].astype(F32), axis=0, keepdims=True)

    return pl.pallas_call(
        body, name=name, grid=(n // tn, m // tm),
        in_specs=[pl.BlockSpec((tm, tn), lambda j, i: (i, j))], out_specs=pl.BlockSpec((1, tn), lambda j, i: (0, j)),
        out_shape=jax.ShapeDtypeStruct((1, n), F32),
        compiler_params=_params(("parallel", "arbitrary")),
    )(a)[0]


ANY = pl.BlockSpec(memory_space=pl.ANY)


def _mesh_pos():
    return lax.axis_index("x"), lax.axis_index("y"), lax.axis_index("c")


def _other_chips(px, py):
    return [(1 - px, py), (px, 1 - py), (1 - px, 1 - py)]


def _all_gather(xs, name):
    n = len(xs)

    def body(*refs):
        x_refs, out_refs = refs[:n], refs[n:2 * n]
        send_sems, recv_sems, local_sems = refs[2 * n:]
        px, py, pc = _mesh_pos()
        me, sibling = (px, py, pc), (px, py, 1 - pc)
        chips = _other_chips(px, py)

        def slot(a, qx, qy, qc):
            return out_refs[a].at[4 * qx + 2 * qy + qc]

        def copy(a, k, block, to, src=None):
            return pltpu.make_async_remote_copy(
                src_ref=slot(a, *block) if src is None else src, dst_ref=slot(a, *block),
                send_sem=send_sems.at[a, k], recv_sem=recv_sems.at[a, k], device_id=to, device_id_type=MESH)

        mine = [pltpu.make_async_copy(x_refs[a], slot(a, *me), local_sems.at[a]) for a in range(n)]
        for cp in mine:
            cp.start()
        first = []
        for a in range(n):
            first.append(copy(a, 0, me, sibling, src=x_refs[a]))
            first += [copy(a, 1 + j, me, (*chip, pc), src=x_refs[a]) for j, chip in enumerate(chips)]
        for cp in first:
            cp.start()
        passed = []
        for j, chip in enumerate(chips):
            for a in range(n):
                copy(a, 1 + j, (*chip, pc), me).wait_recv()
                passed.append(copy(a, 4 + j, (*chip, pc), sibling))
                passed[-1].start()
        for a in range(n):
            copy(a, 0, sibling, me).wait_recv()
            for j, chip in enumerate(chips):
                copy(a, 4 + j, (*chip, 1 - pc), me).wait_recv()
        for cp in first + passed:
            cp.wait_send()
        for cp in mine:
            cp.wait()

    return pl.pallas_call(
        body, name=name, out_shape=[jax.ShapeDtypeStruct((N_DEV,) + x.shape, x.dtype) for x in xs],
        in_specs=[ANY] * n, out_specs=[ANY] * n,
        scratch_shapes=[pltpu.SemaphoreType.DMA((n, 7)), pltpu.SemaphoreType.DMA((n, 7)), pltpu.SemaphoreType.DMA((n,))],
    )(*xs)


def _pair_exchange(gs, name):
    n = len(gs)

    def body(*refs):
        g_refs, r_refs = refs[:n], refs[n:2 * n]
        send_sems, recv_sems = refs[2 * n:]
        px, py, pc = _mesh_pos()
        copies = [pltpu.make_async_remote_copy(
            src_ref=g_refs[a].at[2 * j + (1 - pc)], dst_ref=r_refs[a].at[j], send_sem=send_sems.at[a, j],
            recv_sem=recv_sems.at[a, j], device_id=(px, py, 1 - pc), device_id_type=MESH)
            for a in range(n) for j in range(4)]
        for cp in copies:
            cp.start()
        for cp in copies:
            cp.wait()

    return pl.pallas_call(
        body, name=name, out_shape=[jax.ShapeDtypeStruct((4,) + g.shape[1:], g.dtype) for g in gs],
        in_specs=[ANY] * n, out_specs=[ANY] * n,
        scratch_shapes=[pltpu.SemaphoreType.DMA((n, 4)), pltpu.SemaphoreType.DMA((n, 4))],
    )(*gs)


def _chip_exchange(hs, name):
    n = len(hs)

    def body(*refs):
        h_refs, r_refs = refs[:n], refs[n:2 * n]
        send_sems, recv_sems = refs[2 * n:]
        px, py, pc = _mesh_pos()
        copies = [pltpu.make_async_remote_copy(
            src_ref=h_refs[a].at[2 * cx + cy], dst_ref=r_refs[a].at[k], send_sem=send_sems.at[a, k],
            recv_sem=recv_sems.at[a, k], device_id=(cx, cy, pc), device_id_type=MESH)
            for a in range(n) for k, (cx, cy) in enumerate(_other_chips(px, py))]
        for cp in copies:
            cp.start()
        for cp in copies:
            cp.wait()

    return pl.pallas_call(
        body, name=name, out_shape=[jax.ShapeDtypeStruct((3,) + h.shape[1:], h.dtype) for h in hs],
        in_specs=[ANY] * n, out_specs=[ANY] * n,
        scratch_shapes=[pltpu.SemaphoreType.DMA((n, 3)), pltpu.SemaphoreType.DMA((n, 3))],
    )(*hs)


SEM = pl.BlockSpec(memory_space=pltpu.SEMAPHORE)
HBM = pl.BlockSpec(memory_space=pltpu.HBM)
N_PEERS = N_DEV - 1


def _peer(px, py, pc, r):
    return ((1 - px) if r & 4 else px, (1 - py) if r & 2 else py, (1 - pc) if r & 1 else pc)


def _spread_copies(x_refs, land_refs, send_sems, recv_sems, mine):
    px, py, pc = _mesh_pos()
    out = []
    for a, (x_ref, land_ref) in enumerate(zip(x_refs, land_refs)):
        for r in range(1, N_DEV):
            qx, qy, qc = _peer(px, py, pc, r)
            slot = 4 * px + 2 * py + pc if mine else 4 * qx + 2 * qy + qc
            out.append(pltpu.make_async_remote_copy(
                src_ref=x_ref, dst_ref=land_ref.at[slot], send_sem=send_sems.at[a * N_PEERS + r - 1],
                recv_sem=recv_sems.at[a * N_PEERS + r - 1],
                device_id=(qx, qy, qc), device_id_type=MESH))
    return out


def _spread_start(xs, name, collective_id):
    n = len(xs)

    def body(*refs):
        x_refs, land_refs, send_sems, recv_sems, token = refs[:n], refs[n:2 * n], refs[2 * n], refs[2 * n + 1], refs[-1]
        px, py, pc = _mesh_pos()
        barrier = pltpu.get_barrier_semaphore()
        for r in range(1, N_DEV):
            pl.semaphore_signal(barrier, inc=1, device_id=_peer(px, py, pc, r), device_id_type=MESH)
        pl.semaphore_wait(barrier, N_PEERS)
        for cp in _spread_copies(x_refs, land_refs, send_sems, recv_sems, True):
            cp.start()
        token[...] = jnp.zeros_like(token)

    hbm = lambda t: pltpu.with_memory_space_constraint(t, pltpu.HBM)
    lands = [lax.empty((N_DEV,) + x.shape, x.dtype) for x in xs]
    sems = pltpu.SemaphoreType.DMA((n * N_PEERS,))
    outs = pl.pallas_call(
        body, name=name,
        out_shape=(sems, sems, *[pltpu.HBM(t.shape, t.dtype) for t in list(xs) + lands],
                   jax.ShapeDtypeStruct((SUBLANES, LANES), F32)),
        in_specs=[HBM] * (2 * n), out_specs=(SEM, SEM, *[HBM] * (2 * n), pl.BlockSpec(memory_space=pltpu.VMEM)),
        input_output_aliases={i: 2 + i for i in range(2 * n)},
        compiler_params=pltpu.CompilerParams(has_side_effects=pltpu.SideEffectType.DATAFLOW_SIDE_EFFECTING,
                                             collective_id=collective_id),
    )(*[hbm(t) for t in list(xs) + lands])
    return outs[0], outs[1], outs[2:2 + n], outs[2 + n:2 + 2 * n], outs[-1]


def _spread_wait(send_sems, recv_sems, xs, lands, after, name):
    n = len(xs)

    def body(*refs):
        for cp in _spread_copies(refs[:n], refs[n:2 * n], refs[2 * n], refs[2 * n + 1], False):
            cp.wait_send()
            cp.wait_recv()

    outs = pl.pallas_call(
        body, name=name, out_shape=tuple(pltpu.HBM(t.shape, t.dtype) for t in list(xs) + list(lands)),
        in_specs=[HBM] * (2 * n) + [SEM, SEM, ANY], out_specs=tuple([HBM] * (2 * n)),
        input_output_aliases={i: i for i in range(2 * n)},
        compiler_params=pltpu.CompilerParams(has_side_effects=pltpu.SideEffectType.DATAFLOW_SIDE_EFFECTING),
    )(*xs, *lands, send_sems, recv_sems, after)
    return outs[n:]


def _select_dot(blocks, target, valid):
    src = jnp.concatenate(blocks, axis=1)
    n = src.shape[1]
    pos = lax.broadcasted_iota(jnp.int32, (n, LANES), 0)
    sel = ((pos == target) & valid).astype(src.dtype)
    return lax.dot_general(src, sel, (((1,), (0,)), ((), ())), preferred_element_type=F32)


def _shards_to_pieces(g, pieces, shard_w, name):
    _, nl, r, pw = g.shape
    nb = pw // LANES
    c0s, nvs, blks = [], [], []
    for start, real, padded in pieces:
        for u in range(padded // LANES):
            c0, nv = start + LANES * u, max(0, min(LANES, real - LANES * u))
            cp_lo = c0 + (pw - shard_w) * min(c0 // shard_w, N_DEV - 1)
            cp_hi = c0 + nv - 1 + (pw - shard_w) * min((c0 + nv - 1) // shard_w, N_DEV - 1)
            assert nv == 0 or cp_hi - (cp_lo // LANES) * LANES < 3 * LANES
            c0s.append(c0)
            nvs.append(nv)
            blks.append(cp_lo // LANES if nv else 0)
    table = jnp.array([c0s, nvs, blks], jnp.int32)
    tr = min(r, 2048)
    last = N_DEV * nb - 1

    def body(tab_ref, b0, b1, b2, o_ref):
        t = pl.program_id(2)
        c0, nv, blk0 = tab_ref[0, t], tab_ref[1, t], tab_ref[2, t]
        j = lax.broadcasted_iota(jnp.int32, (1, LANES), 1)
        c = c0 + j
        k = jnp.zeros_like(c)
        for q in range(1, N_DEV):
            k = k + (c >= q * shard_w).astype(jnp.int32)
        target = c + (pw - shard_w) * k - blk0 * LANES
        o_ref[0] = _select_dot([b0[0, 0], b1[0, 0], b2[0, 0]], target, j < nv).astype(o_ref.dtype)

    def src(d):
        def index(l, i, t, tab_ref):
            blk = jnp.minimum(tab_ref[2, t] + d, last)
            return (blk // nb, l, i, blk % nb)
        return pl.BlockSpec((1, 1, tr, LANES), index)

    return pl.pallas_call(
        body, name=name, out_shape=jax.ShapeDtypeStruct((nl, r, LANES * len(c0s)), g.dtype),
        grid_spec=pltpu.PrefetchScalarGridSpec(
            num_scalar_prefetch=1, grid=(nl, r // tr, len(c0s)), in_specs=[src(0), src(1), src(2)],
            out_specs=pl.BlockSpec((1, tr, LANES), lambda l, i, t, tab_ref: (l, i, t))),
        compiler_params=_params(("parallel", "parallel", "parallel")),
    )(table, g, g, g)


def _pieces_to_shards(d, segs, far_start, shard_w, pw, name):
    nl, r, s_cols = d.shape
    nb = pw // LANES
    shifts = [src - gs for gs, src in segs]

    def source(c):
        return c + [sh for (gs, _), sh in zip(segs, shifts) if c >= gs][-1]

    nvs, blks = [], []
    for k in range(N_DEV):
        for b in range(nb):
            c0, nv = k * shard_w + LANES * b, max(0, min(LANES, shard_w - LANES * b))
            near = [source(c) for c in range(c0, c0 + nv)]
            near = [s for s in near if far_start is None or s < far_start]
            blk0 = min(near) // LANES if near else 0
            assert not near or max(near) - blk0 * LANES < 3 * LANES
            nvs.append(nv)
            blks.append(blk0)
    table = jnp.array([nvs, blks], jnp.int32)
    tr = min(r, 2048)
    last = s_cols // LANES - 1
    far_blk = 0 if far_start is None else far_start // LANES

    def body(tab_ref, b0, b1, b2, bf, o_ref):
        t = pl.program_id(2)
        nv, blk0 = tab_ref[0, t], tab_ref[1, t]
        j = lax.broadcasted_iota(jnp.int32, (1, LANES), 1)
        c = (t // nb) * shard_w + (t % nb) * LANES + j
        s = c + shifts[0]
        for (gs, _), prev, sh in zip(segs[1:], shifts[:-1], shifts[1:]):
            s = s + (sh - prev) * (c >= gs).astype(jnp.int32)
        target = s - blk0 * LANES
        if far_start is not None:
            target = jnp.where(s >= far_start, 3 * LANES + s - far_start, target)
        o_ref[0, 0] = _select_dot([b0[0], b1[0], b2[0], bf[0]], target, j < nv).astype(o_ref.dtype)

    def src(dd):
        return pl.BlockSpec((1, tr, LANES), lambda l, i, t, tab_ref: (l, i, jnp.minimum(tab_ref[1, t] + dd, last)))

    return pl.pallas_call(
        body, name=name, out_shape=jax.ShapeDtypeStruct((N_DEV, nl, r, pw), d.dtype),
        grid_spec=pltpu.PrefetchScalarGridSpec(
            num_scalar_prefetch=1, grid=(nl, r // tr, N_DEV * nb),
            in_specs=[src(0), src(1), src(2), pl.BlockSpec((1, tr, LANES), lambda l, i, t, tab_ref: (l, i, far_blk))],
            out_specs=pl.BlockSpec((1, 1, tr, LANES), lambda l, i, t, tab_ref: (t // nb, l, i, t % nb))),
        compiler_params=_params(("parallel", "parallel", "parallel")),
    )(table, d, d, d, d)


def _cat_blocks(g, axis):
    _, nl, r, c = g.shape
    shape = (nl, N_DEV * r, c) if axis == 0 else (nl, r, N_DEV * c)

    def body(g_ref, o_ref):
        o_ref[0] = g_ref[0, 0]

    return pl.pallas_call(
        body, name="cat_blocks", grid=(nl, N_DEV), out_shape=jax.ShapeDtypeStruct(shape, g.dtype),
        in_specs=[pl.BlockSpec((1, 1, r, c), lambda l, k: (k, l, 0, 0))],
        out_specs=pl.BlockSpec((1, r, c), (lambda l, k: (l, k, 0)) if axis == 0 else (lambda l, k: (l, 0, k))),
        compiler_params=_params(("parallel", "parallel")),
    )(g)


def _split_blocks(d, axis):
    nl = d.shape[0]
    r, c = (d.shape[1] // N_DEV, d.shape[2]) if axis == 0 else (d.shape[1], d.shape[2] // N_DEV)

    def body(d_ref, o_ref):
        o_ref[0, 0] = d_ref[0]

    return pl.pallas_call(
        body, name="split_blocks", grid=(nl, N_DEV), out_shape=jax.ShapeDtypeStruct((N_DEV, nl, r, c), d.dtype),
        in_specs=[pl.BlockSpec((1, r, c), (lambda l, k: (l, k, 0)) if axis == 0 else (lambda l, k: (l, 0, k)))],
        out_specs=pl.BlockSpec((1, 1, r, c), lambda l, k: (k, l, 0, 0)),
        compiler_params=_params(("parallel", "parallel")),
    )(d)


def _row_tile(r, mult, pref=256):
    return max(d for d in range(mult, pref + 1, mult) if r % d == 0)


def _pair_sum(g, r1):
    _, r, c = g.shape
    tr = _row_tile(r, 16)

    def body(g_ref, r_ref, h_ref, own_ref):
        j = pl.program_id(1)
        px, py, pc = _mesh_pos()
        mine = jnp.where(pc == 0, g_ref[0, 0].astype(F32), g_ref[0, 1].astype(F32))
        val = (mine + r_ref[0].astype(F32)).astype(h_ref.dtype)
        h_ref[0] = val

        @pl.when(j == 2 * px + py)
        def _():
            own_ref[...] = val

    return pl.pallas_call(
        body, name="rs_pair_sum", grid=(r // tr, 4),
        in_specs=[pl.BlockSpec((1, 2, tr, c), lambda i, j: (j, 0, i, 0)), pl.BlockSpec((1, tr, c), lambda i, j: (j, i, 0))],
        out_specs=[pl.BlockSpec((1, tr, c), lambda i, j: (j, i, 0)), pl.BlockSpec((tr, c), lambda i, j: (i, 0))],
        out_shape=[jax.ShapeDtypeStruct((4, r, c), g.dtype), jax.ShapeDtypeStruct((r, c), g.dtype)],
        compiler_params=_params(("parallel", "arbitrary")),
    )(g.reshape(4, 2, r, c), r1)


def _adamw(parts, w, m, v, name, tr):
    r, c = w.shape
    n = len(parts)
    slots = [slot for _, slot in parts]

    def body(*refs):
        w_ref, m_ref, v_ref = refs[n:n + 3]
        g_ref, d_ref, nm_ref, nv_ref = refs[n + 3:]
        g = None
        for slot, ref in zip(slots, refs[:n]):
            t = (ref[...] if slot is None else ref[0]).astype(F32)
            g = t if g is None else g + t
        nm = ADAM_B1 * m_ref[...] + (1.0 - ADAM_B1) * g
        nv = ADAM_B2 * v_ref[...] + (1.0 - ADAM_B2) * (g * g)
        m_hat = nm / (1.0 - ADAM_B1 ** ADAM_STEP)
        v_hat = nv / (1.0 - ADAM_B2 ** ADAM_STEP)
        g_ref[...] = g
        d_ref[...] = -ADAM_LR * (m_hat / (jnp.sqrt(v_hat) + ADAM_EPS) + ADAM_WD * w_ref[...])
        nm_ref[...] = nm
        nv_ref[...] = nv

    flat = pl.BlockSpec((tr, c), lambda i: (i, 0))
    specs = [flat if slot is None else pl.BlockSpec((1, tr, c), functools.partial(lambda i, s: (s, i, 0), s=slot))
             for _, slot in parts]
    out = jax.ShapeDtypeStruct((r, c), F32)
    return pl.pallas_call(
        body, name=name, grid=(r // tr,), in_specs=specs + [flat] * 3, out_specs=[flat] * 4, out_shape=[out] * 4,
        compiler_params=_params(("parallel",)),
    )(*[a for a, _ in parts], w, m, v)


def _adamw_whole(parts, w, m, v, name):
    n_parts = parts.shape[0]
    shape = w.shape[1:]
    zeros = (0,) * len(shape)

    def body(p_ref, w_ref, m_ref, v_ref, g_ref, d_ref, nm_ref, nv_ref):
        g = p_ref[0]
        for k in range(1, n_parts):
            g = g + p_ref[k]
        nm = ADAM_B1 * m_ref[...] + (1.0 - ADAM_B1) * g
        nv = ADAM_B2 * v_ref[...] + (1.0 - ADAM_B2) * (g * g)
        m_hat = nm / (1.0 - ADAM_B1 ** ADAM_STEP)
        v_hat = nv / (1.0 - ADAM_B2 ** ADAM_STEP)
        g_ref[...] = g
        d_ref[...] = -ADAM_LR * (m_hat / (jnp.sqrt(v_hat) + ADAM_EPS) + ADAM_WD * w_ref[...])
        nm_ref[...] = nm
        nv_ref[...] = nv

    one = pl.BlockSpec((1,) + shape, lambda l: (l,) + zeros)
    out = jax.ShapeDtypeStruct(w.shape, F32)
    return pl.pallas_call(
        body, name=name, grid=(w.shape[0],),
        in_specs=[pl.BlockSpec((n_parts, 1) + shape, lambda l: (0, l) + zeros), one, one, one],
        out_specs=[one] * 4, out_shape=[out] * 4, compiler_params=_params(("parallel",)),
    )(parts, w, m, v)


def _sum_parts(parts, name):
    n_parts = parts.shape[0]
    shape = parts.shape[2:]
    zeros = (0,) * len(shape)

    def body(p_ref, o_ref):
        g = p_ref[0]
        for k in range(1, n_parts):
            g = g + p_ref[k]
        o_ref[...] = g

    return pl.pallas_call(
        body, name=name, grid=(parts.shape[1],),
        in_specs=[pl.BlockSpec((n_parts, 1) + shape, lambda l: (0, l) + zeros)],
        out_specs=pl.BlockSpec((1,) + shape, lambda l: (l,) + zeros),
        out_shape=jax.ShapeDtypeStruct(parts.shape[1:], F32), compiler_params=_params(("parallel",)),
    )(parts)


SMALL = ("b_in", "sgu_ln_g", "sgu_ln_b", "sgu_w", "sgu_b", "gdn_a_log", "gdn_dt_bias", "gdn_norm_g", "ln1_g", "ln1_b",
         "ffn_conv_b", "ln2_g", "ln2_b")
WEIGHT_ORDER = ("w_in", "b_in", "sgu_ln_g", "sgu_ln_b", "sgu_w", "sgu_b", "gdn_conv_w", "gdn_a_log", "gdn_dt_bias",
                "gdn_norm_g", "w_proj_a", "w_proj_b", "w_proj_c", "w_out", "ln1_g", "ln1_b", "ffn_w_up", "ffn_conv_w",
                "ffn_conv_b", "ffn_w_down", "ln2_g", "ln2_b")
IN_SHARD = N_IN // N_DEV
IN_SHARD_PAD = -(-IN_SHARD // LANES) * LANES
UP_SHARD = 2 * D_FF // N_DEV
UP_SHARD_PAD = -(-UP_SHARD // LANES) * LANES

O_FOX, O_FF, O_SGU, O_GDN, O_A, O_B, O_GATE, O_GATES = 0, 3072, 3080, 5128, 8200, 8208, 8216, 9240
IN_PIECES = (("fq", O_FOX, WIDTH, BF16), ("fk", O_FOX + WIDTH, WIDTH, BF16), ("fv", O_FOX + 2 * WIDTH, WIDTH, BF16),
             ("su", O_SGU, WIDTH, F32), ("sv", O_SGU + WIDTH, WIDTH, F32),
             ("gq", O_GDN, WIDTH, F32), ("gk", O_GDN + WIDTH, WIDTH, F32), ("gv", O_GDN + 2 * WIDTH, WIDTH, F32),
             ("gg", O_GATE, WIDTH, F32),
             ("ga", O_GATES, D_MODEL, F32), ("gb", O_GATES + D_MODEL, D_MODEL, F32), ("gc", O_GATES + 2 * D_MODEL, D_MODEL, F32))
N_SMALL_COLS = 3 * N_HEADS


def _split_in(w):
    out = {n: w[..., o:o + k] for n, o, k, _ in IN_PIECES}
    sm = jnp.concatenate([w[..., O_FF:O_FF + N_HEADS], w[..., O_A:O_A + 2 * N_HEADS]], axis=-1)
    out["sm"] = jnp.pad(sm, [(0, 0)] * (w.ndim - 1) + [(0, LANES - N_SMALL_COLS)])
    return out


def _pad_to(a, axis, n):
    pad = [(0, 0)] * a.ndim
    pad[axis] = (0, n - a.shape[axis])
    return jnp.pad(a, pad)


IN_MAIN = ((O_FOX, 3 * WIDTH, 3 * WIDTH), (O_SGU, O_A - O_SGU, O_A - O_SGU), (O_GATE, N_IN - O_GATE, N_IN - O_GATE))
N_MAIN = sum(p[2] for p in IN_MAIN)
N_RE = N_MAIN + LANES
IN_OFF = {"fq": 0, "fk": WIDTH, "fv": 2 * WIDTH, "su": 3 * WIDTH, "sv": 4 * WIDTH, "gq": 5 * WIDTH, "gk": 6 * WIDTH,
          "gv": 7 * WIDTH, "gg": 8 * WIDTH, "ga": 9 * WIDTH, "gb": 9 * WIDTH + D_MODEL, "gc": 9 * WIDTH + 2 * D_MODEL,
          "sm": N_MAIN}
IN_SEGS = ((0, 0), (O_FF, N_MAIN), (O_SGU, 3 * WIDTH), (O_A, N_MAIN + N_HEADS), (O_GATE, 8 * WIDTH))
UP_PIECES = ((0, D_FF, D_FF_PAD), (D_FF, D_FF, D_FF_PAD))
UP_SEGS = ((0, 0), (D_FF, D_FF_PAD))


def _shard_cols(g, start, n, shard_w):
    k, r = start // shard_w, start % shard_w
    assert r + n <= shard_w
    return g[k, :, :, r:r + n]


def _reorder_bias(v):
    return jnp.concatenate([v[:3 * WIDTH], v[N_MAIN:N_MAIN + N_HEADS], v[3 * WIDTH:8 * WIDTH],
                            v[N_MAIN + N_HEADS:N_MAIN + N_SMALL_COLS], v[8 * WIDTH:N_MAIN]])


def _layer_weights(wt, l, conv, small):
    w = {"re": wt["in"][l], "b_in": _split_in(small["b_in"][None, :])}
    w["pa"], w["pb"], w["pc"], w["out"] = wt["pa"][l], wt["pb"][l], wt["pc"][l], wt["out"][l]
    w["up"], w["down"] = wt["up"][l], wt["down"][l]
    gcw, fcw, fcb = conv["gdn_conv_w"], conv["ffn_conv_w"], small["ffn_conv_b"][None, :]
    w["gcq"], w["gck"], w["gcv"] = gcw[:, :WIDTH], gcw[:, WIDTH:2 * WIDTH], gcw[:, 2 * WIDTH:]
    w["fcg"], w["fcv"] = _pad_to(fcw[:, :D_FF], 1, D_FF_PAD), _pad_to(fcw[:, D_FF:], 1, D_FF_PAD)
    w["fbg"], w["fbv"] = _pad_to(fcb[:, :D_FF], 1, D_FF_PAD), _pad_to(fcb[:, D_FF:], 1, D_FF_PAD)
    for n in ("sgu_ln_g", "sgu_ln_b", "sgu_w", "sgu_b", "gdn_a_log", "gdn_dt_bias", "gdn_norm_g", "ln1_g", "ln1_b",
              "ln2_g", "ln2_b"):
        w[n] = small[n]
    return w


def _layer_fwd(x, xb, w, tag):
    s = {"x": x, "xb": xb}
    for n, _, k, dt in IN_PIECES + (("sm", 0, LANES, F32),):
        s[n] = _mm(xb, w["re"], mode="nn", name="in_" + n, bias=w["b_in"][n][0], out_dtype=dt, b_off=IN_OFF[n], b_len=k)
    s["c_col"] = _fox_prefix_fwd(s["sm"])
    s["c_row"] = _row_form(s["c_col"], ATT_T)
    s["ya"], s["ya32"], s["lse"] = _fox_fwd(s["fq"], s["fk"], s["fv"], s["c_col"], s["c_row"])
    s["yb"] = _sgu_fwd(s["su"], s["sv"], w["sgu_ln_g"], w["sgu_ln_b"], w["sgu_w"], w["sgu_b"])
    s["yc"], s["states"] = _gdn_fwd(s["gq"], s["gk"], s["gv"], s["sm"], s["gg"], w["gcq"], w["gck"], w["gcv"],
                                    w["gdn_a_log"], w["gdn_dt_bias"], w["gdn_norm_g"])
    s["merged"], s["za"], s["zb"], s["zc"] = _merge_fwd(
        [s["ya"], s["yb"], s["yc"]], [w["pa"], w["pb"], w["pc"]], [s["ga"], s["gb"], s["gc"]])
    s["mix"] = _mm(s["merged"], w["out"], mode="nn", name="out_proj")
    s["x1"], s["x1b"] = _ln_fwd(x, s["mix"], w["ln1_g"], w["ln1_b"], name="ln_fwd")
    s["hug"] = _mm(s["x1b"], w["up"], mode="nn", name="ffn_up_gate", b_off=0, b_len=D_FF_PAD)
    s["huv"] = _mm(s["x1b"], w["up"], mode="nn", name="ffn_up_val", b_off=D_FF_PAD, b_len=D_FF_PAD)
    s["act"] = _ffn_act_fwd(s["hug"], s["huv"], w["fcg"], w["fcv"], w["fbg"], w["fbv"])
    s["ffn"] = _mm(s["act"], w["down"], mode="nn", name="ffn_down")
    x2, x2b = _ln_fwd(s["x1"], s["ffn"], w["ln2_g"], w["ln2_b"], name="ln_fwd")
    return x2, x2b, s


def _layer_bwd(dx2, s, w):
    gb, gc, gs = {}, {}, {}
    ds2, ds2b, dg2, db2 = _ln_bwd(s["x1"], s["ffn"], w["ln2_g"], dx2, name="ln_bwd")
    gs["ln2_g"], gs["ln2_b"] = dg2[0], db2[0]
    gb["ffn_w_down"] = _mm(s["act"], ds2b, mode="tn", name="dw_down", out_dtype=BF16)[:D_FF]
    dact = _mm(ds2b, w["down"], mode="nt", name="dact")
    dhug, dhuv, dcwg, dcwv, dcbg, dcbv = _ffn_act_bwd(s["hug"], s["huv"], w["fcg"], w["fcv"], w["fbg"], w["fbv"], dact)
    gc["ffn_conv_w"] = jnp.concatenate([dcwg[:, :D_FF], dcwv[:, :D_FF]], axis=1)
    gs["ffn_conv_b"] = jnp.concatenate([dcbg[0, :D_FF], dcbv[0, :D_FF]])
    dhu = jnp.concatenate([dhug, dhuv], axis=1)
    gb["ffn_w_up"] = _mm(s["x1b"], dhu, mode="tn", name="dw_up", out_dtype=BF16)
    dx1 = _mm(dhu, w["up"], mode="nt", name="dx_up", add=ds2, add_scale=DEEPNORM_ALPHA)
    ds1, ds1b, dg1, db1 = _ln_bwd(s["x"], s["mix"], w["ln1_g"], dx1, name="ln_bwd")
    gs["ln1_g"], gs["ln1_b"] = dg1[0], db1[0]
    gb["w_out"] = _mm(s["merged"], ds1b, mode="tn", name="dw_out", out_dtype=BF16)
    dmerged = _mm(ds1b, w["out"], mode="nt", name="dmerged")
    dza, dzb, dzc, dga, dgb, dgc = _merge_bwd(dmerged, [s["za"], s["zb"], s["zc"]], [s["ga"], s["gb"], s["gc"]])
    gb["w_proj_a"] = _mm(s["ya"], dza, mode="tn", name="dw_proj", out_dtype=BF16)
    gb["w_proj_b"] = _mm(s["yb"], dzb, mode="tn", name="dw_proj", out_dtype=BF16)
    gb["w_proj_c"] = _mm(s["yc"], dzc, mode="tn", name="dw_proj", out_dtype=BF16)
    dya = _mm(dza, w["pa"], mode="nt", name="dy_proj_bf16", out_dtype=BF16)
    dyb = _mm(dzb, w["pb"], mode="nt", name="dy_proj")
    dyc = _mm(dzc, w["pc"], mode="nt", name="dy_proj")
    d = {"ga": dga, "gb": dgb, "gc": dgc}
    fox = (s["fq"], s["fk"], s["fv"], s["ya32"], dya, s["lse"], s["c_col"], s["c_row"])
    d["fk"], d["fv"], dc_row = _fox_bwd_kv(*fox)
    d["fq"], dc_q = _fox_bwd_q(*fox)
    dc_col = _col_form(dc_row, x_rows(s)) + jnp.pad(dc_q[:, :, 0].T, ((0, 0), (0, LANES - N_HEADS)))
    dsm = _fox_prefix_bwd(s["sm"], dc_col)
    d["su"], d["sv"], dlg, dlb, dsw, dsb = _sgu_bwd(s["su"], s["sv"], w["sgu_ln_g"], w["sgu_ln_b"], w["sgu_w"], w["sgu_b"], dyb)
    gs["sgu_ln_g"], gs["sgu_ln_b"], gs["sgu_w"], gs["sgu_b"] = dlg.reshape(-1), dlb.reshape(-1), dsw, dsb[:, :, 0]
    (d["gq"], d["gk"], d["gv"], d["gg"], dab, dwq, dwk, dwv, dal, ddt, dng) = _gdn_bwd(
        s["gq"], s["gk"], s["gv"], s["sm"], s["gg"], w["gcq"], w["gck"], w["gcv"],
        w["gdn_a_log"], w["gdn_dt_bias"], w["gdn_norm_g"], s["states"], dyc)
    gc["gdn_conv_w"] = jnp.concatenate([dwq, dwk, dwv], axis=1)
    gs["gdn_a_log"], gs["gdn_dt_bias"], gs["gdn_norm_g"] = dal[:, 0, 0], ddt[:, 0, 0], dng[0]
    dab_cols = jnp.concatenate([dab[:, :, 0].T, dab[:, :, 1].T], axis=1)
    d["sm"] = (dsm + jnp.pad(dab_cols, ((0, 0), (N_HEADS, LANES - N_SMALL_COLS)))).astype(BF16)
    dp = jnp.concatenate([d[n] for n in sorted(IN_OFF, key=IN_OFF.get)], axis=1)
    gb["w_in"] = _mm(s["xb"], dp, mode="tn", name="dw_in", out_dtype=BF16)
    gs["b_in"] = _reorder_bias(_colsum(dp, "db_in"))
    dx = _mm(dp, w["re"], mode="nt", name="dx_in", add=ds1, add_scale=DEEPNORM_ALPHA)
    return dx, gb, gc, gs


def x_rows(s):
    return s["x"].shape[0]


def _step(a):
    x = a["x"][0]
    kinds = ("grad_", "delta_", "new_m_", "new_v_")
    res = {}
    bf = lambda n: a[n].astype(BF16)
    pad_in = lambda t: _pad_to(t, 2, IN_SHARD_PAD)
    pad_up = lambda t: _pad_to(t, 2, UP_SHARD_PAD)
    shards = [pad_in(bf("w_in")), pad_up(bf("ffn_w_up")), bf("w_proj_a"), bf("w_proj_b"), bf("w_proj_c"), bf("w_out"),
              bf("ffn_w_down"), a["gdn_conv_w"], a["ffn_conv_w"]]

    def layer_weights(g, n_layers, first, after=0.0):
        g_in, g_up, g_pa, g_pb, g_pc, g_out, g_down, g_gc, g_fc = g
        sm = jnp.concatenate([_shard_cols(g_in, O_FF, N_HEADS, IN_SHARD), _shard_cols(g_in, O_A, 2 * N_HEADS, IN_SHARD)], axis=-1)
        wt = {"in": jnp.concatenate([_shards_to_pieces(g_in, IN_MAIN, IN_SHARD, "in_to_pieces"), _pad_to(sm, 2, LANES)], axis=-1),
              "up": _shards_to_pieces(g_up, UP_PIECES, UP_SHARD, "up_to_pieces"),
              "pa": _cat_blocks(g_pa, 1), "pb": _cat_blocks(g_pb, 1), "pc": _cat_blocks(g_pc, 1),
              "out": _cat_blocks(g_out, 0), "down": _pad_to(_cat_blocks(g_down, 0), 1, D_FF_PAD)}
        conv_full = {"gdn_conv_w": g_gc.transpose(1, 2, 0, 3).reshape(n_layers, GDN_CONV, 3 * WIDTH),
                     "ffn_conv_w": g_fc.transpose(1, 2, 0, 3).reshape(n_layers, FFN_CONV, 2 * D_FF)}
        small = lambda l: {n: a[n][first + l] + after if n == "b_in" else a[n][first + l] for n in SMALL}
        return [_layer_weights(wt, l, {n: t[l] for n, t in conv_full.items()}, small(l)) for l in range(n_layers)]

    g_first = _all_gather([t[:1] for t in shards], "gather_weights")
    groups = [(lo, hi) for lo, hi in ((1, 2), (2, DEPTH)) if lo < hi <= DEPTH]
    flying = {}
    after = g_first[-1]
    for i, (lo, hi) in enumerate(groups):
        part = [t[lo:hi] for t in shards]
        part[-1], _ = lax.optimization_barrier((part[-1], after))
        flying[lo] = (hi,) + _spread_start(part, "gather_start_%d" % i, 7 + i)
        after = flying[lo][-1]
    layers = layer_weights(g_first, 1, 0, after[0, 0])

    xb = x.astype(BF16)
    saved = []
    me = 4 * lax.axis_index("x") + 2 * lax.axis_index("y") + lax.axis_index("c")
    for l in range(DEPTH):
        if l in flying:
            hi, send_sems, recv_sems, part, lands, _ = flying[l]
            lands = _spread_wait(send_sems, recv_sems, part, lands, x, "gather_wait_%d" % l)
            lands = [lax.dynamic_update_index_in_dim(land, t[None], me, 0) for land, t in zip(lands, part)]
            layers += layer_weights(lands, hi - l, l)
        x, xb, s = _layer_fwd(x, xb, layers[l], l)
        saved.append(s)
    dx, loss = _loss_head(x, a["loss_target"][0])
    res["loss"] = lax.psum(loss, ("x", "y", "c"))

    grads = [None] * DEPTH
    for l in reversed(range(DEPTH)):
        dx, gb, gc, gs = _layer_bwd(dx, saved[l], layers[l])
        grads[l] = {**gb, **gc, **gs}
    res["grad_x"] = dx[None]
    stacked = {n: jnp.stack([g[n] for g in grads]) for n in grads[0]}

    big = (("w_in", _pieces_to_shards(stacked["w_in"], IN_SEGS, N_MAIN, IN_SHARD, IN_SHARD_PAD, "in_to_shards"), pad_in),
           ("ffn_w_up", _pieces_to_shards(stacked["ffn_w_up"], UP_SEGS, None, UP_SHARD, UP_SHARD_PAD, "up_to_shards"), pad_up),
           ("w_proj_a", _split_blocks(stacked["w_proj_a"], 1), None), ("w_proj_b", _split_blocks(stacked["w_proj_b"], 1), None),
           ("w_proj_c", _split_blocks(stacked["w_proj_c"], 1), None), ("w_out", _split_blocks(stacked["w_out"], 0), None),
           ("ffn_w_down", _split_blocks(stacked["ffn_w_down"], 0), None))
    flat3 = lambda t: t.reshape(t.shape[0], -1, t.shape[-1])
    gs_ = [flat3(g) for _, g, _ in big]
    r1s = _pair_exchange(gs_, "rs_pair_exchange")
    sums = [_pair_sum(g, r1) for g, r1 in zip(gs_, r1s)]
    r2s = _chip_exchange([h for h, _ in sums], "rs_chip_exchange")
    for (n, _, pad), (_, own), r2 in zip(big, sums, r2s):
        prep = lambda t: (t if pad is None else pad(t)).reshape(own.shape)
        outs = _adamw([(own, None), (r2, 0), (r2, 1), (r2, 2)], prep(a[n]), prep(a["m_" + n]), prep(a["v_" + n]),
                      "adamw_" + n, _row_tile(own.shape[0], 16, max(16, (1 << 18) // own.shape[1])))
        for kind, o in zip(kinds, outs):
            o = o.reshape((DEPTH, -1, own.shape[1]))
            res[kind + n] = o if pad is None else o[:, :, :a[n].shape[2]]

    nd = lambda t: t.reshape(t.shape[0], 1, t.shape[1]) if t.ndim == 2 else t
    names = list(SMALL) + ["gdn_conv_w", "ffn_conv_w"]
    parts = dict(zip(names, _all_gather([nd(stacked[n]) for n in names], "gather_small_grads")))
    for n in SMALL:
        outs = _adamw_whole(parts[n], nd(a[n]), nd(a["m_" + n]), nd(a["v_" + n]), "adamw_" + n)
        for kind, o in zip(kinds, outs):
            res[kind + n] = o.reshape(a[n].shape)
    me = 4 * lax.axis_index("x") + 2 * lax.axis_index("y") + lax.axis_index("c")
    for n in ("gdn_conv_w", "ffn_conv_w"):
        width = a[n].shape[2]
        g_own = lax.dynamic_slice_in_dim(_sum_parts(parts[n], "sum_" + n), me * width, width, axis=2)
        outs = _adamw_whole(g_own[None], a[n], a["m_" + n], a["v_" + n], "adamw_" + n)
        for kind, o in zip(kinds, outs):
            res[kind + n] = o
    return res


INPUT_ORDER = (("x",) + WEIGHT_ORDER + ("loss_target",) + tuple("m_" + n for n in WEIGHT_ORDER)
               + tuple("v_" + n for n in WEIGHT_ORDER))
OUTPUT_ORDER = (("loss", "grad_x") + tuple(k + n for k in ("grad_", "delta_", "new_m_", "new_v_") for n in WEIGHT_ORDER))


def kernel(x, w_in, b_in, sgu_ln_g, sgu_ln_b, sgu_w, sgu_b, gdn_conv_w, gdn_a_log, gdn_dt_bias, gdn_norm_g, w_proj_a, w_proj_b, w_proj_c, w_out, ln1_g, ln1_b, ffn_w_up, ffn_conv_w, ffn_conv_b, ffn_w_down, ln2_g, ln2_b, loss_target, m_w_in, m_b_in, m_sgu_ln_g, m_sgu_ln_b, m_sgu_w, m_sgu_b, m_gdn_conv_w, m_gdn_a_log, m_gdn_dt_bias, m_gdn_norm_g, m_w_proj_a, m_w_proj_b, m_w_proj_c, m_w_out, m_ln1_g, m_ln1_b, m_ffn_w_up, m_ffn_conv_w, m_ffn_conv_b, m_ffn_w_down, m_ln2_g, m_ln2_b, v_w_in, v_b_in, v_sgu_ln_g, v_sgu_ln_b, v_sgu_w, v_sgu_b, v_gdn_conv_w, v_gdn_a_log, v_gdn_dt_bias, v_gdn_norm_g, v_w_proj_a, v_w_proj_b, v_w_proj_c, v_w_out, v_ln1_g, v_ln1_b, v_ffn_w_up, v_ffn_conv_w, v_ffn_conv_b, v_ffn_w_down, v_ln2_g, v_ln2_b):
    args = (x, w_in, b_in, sgu_ln_g, sgu_ln_b, sgu_w, sgu_b, gdn_conv_w, gdn_a_log, gdn_dt_bias, gdn_norm_g, w_proj_a, w_proj_b, w_proj_c, w_out, ln1_g, ln1_b, ffn_w_up, ffn_conv_w, ffn_conv_b, ffn_w_down, ln2_g, ln2_b, loss_target, m_w_in, m_b_in, m_sgu_ln_g, m_sgu_ln_b, m_sgu_w, m_sgu_b, m_gdn_conv_w, m_gdn_a_log, m_gdn_dt_bias, m_gdn_norm_g, m_w_proj_a, m_w_proj_b, m_w_proj_c, m_w_out, m_ln1_g, m_ln1_b, m_ffn_w_up, m_ffn_conv_w, m_ffn_conv_b, m_ffn_w_down, m_ln2_g, m_ln2_b, v_w_in, v_b_in, v_sgu_ln_g, v_sgu_ln_b, v_sgu_w, v_sgu_b, v_gdn_conv_w, v_gdn_a_log, v_gdn_dt_bias, v_gdn_norm_g, v_w_proj_a, v_w_proj_b, v_w_proj_c, v_w_out, v_ln1_g, v_ln1_b, v_ffn_w_up, v_ffn_conv_w, v_ffn_conv_b, v_ffn_w_down, v_ln2_g, v_ln2_b)
    res = _step(dict(zip(INPUT_ORDER, args)))
    return tuple(res[n] for n in OUTPUT_ORDER)
```

```python
import functools

import jax
import jax.numpy as jnp
from jax import lax
from jax.experimental import pallas as pl
from jax.experimental.pallas import tpu as pltpu

F32 = jnp.float32
BF16 = jnp.bfloat16

N_DEV = 8
DEPTH = 4
D_MODEL = 2048
HEAD_DIM = 128
N_HEADS = 8
WIDTH = N_HEADS * HEAD_DIM
CHUNK = 64
SGU_SPAN = 128
GDN_CONV = 4
FFN_CONV = 3
D_FF = 5504
D_FF_PAD = 5632
N_IN = 15384
DEEPNORM_ALPHA = (2 * DEPTH) ** 0.25
LN_EPS = 1e-5
RMS_EPS = 1e-6
ADAM_LR = 0.001
ADAM_B1 = 0.9
ADAM_B2 = 0.999
ADAM_EPS = 1e-08
ADAM_WD = 0.01
ADAM_STEP = 10

LANES = 128
SUBLANES = 8
VMEM_LIMIT = 56 * 1024 * 1024

MESH = pl.DeviceIdType.MESH


def _params(sem, vmem=VMEM_LIMIT):
    return pltpu.CompilerParams(dimension_semantics=sem, vmem_limit_bytes=vmem)


def _tile(n, pref):
    best = None
    for d in range(LANES, min(n, pref) + 1, LANES):
        if n % d == 0:
            best = d
    return n if best is None else best


def _sigmoid(x):
    return jax.nn.sigmoid(x)


def _silu(x):
    return x * jax.nn.sigmoid(x)


def _raw_dot(a, b, dims):
    return lax.dot_general(a.astype(BF16), b.astype(BF16), (dims, ((), ())), preferred_element_type=F32)


@jax.custom_vjp
def _bdot(a, b):
    return _raw_dot(a, b, ((1,), (0,)))


def _bdot_fwd(a, b):
    return _bdot(a, b), (a, b)


def _bdot_bwd(res, ct):
    a, b = res
    return _raw_dot(ct, b, ((1,), (1,))), _raw_dot(a, ct, ((0,), (0,)))


_bdot.defvjp(_bdot_fwd, _bdot_bwd)


@jax.custom_vjp
def _bdot_nt(a, b):
    return _raw_dot(a, b, ((1,), (1,)))


def _bdot_nt_fwd(a, b):
    return _bdot_nt(a, b), (a, b)


def _bdot_nt_bwd(res, ct):
    a, b = res
    return _raw_dot(ct, b, ((1,), (0,))), _raw_dot(ct, a, ((0,), (0,)))


_bdot_nt.defvjp(_bdot_nt_fwd, _bdot_nt_bwd)


@jax.custom_vjp
def _bdot_tn(a, b):
    return _raw_dot(a, b, ((0,), (0,)))


def _bdot_tn_fwd(a, b):
    return _bdot_tn(a, b), (a, b)


def _bdot_tn_bwd(res, ct):
    a, b = res
    return _raw_dot(b, ct, ((1,), (1,))), _raw_dot(a, ct, ((1,), (0,)))


_bdot_tn.defvjp(_bdot_tn_fwd, _bdot_tn_bwd)


def _split3(a):
    hi = a.astype(BF16)
    r = a - hi.astype(F32)
    lo = r.astype(BF16)
    return hi, lo


def _dot3_raw(a, b, dims):
    ah, al = _split3(a)
    bh, bl = _split3(b)
    d = functools.partial(lax.dot_general, dimension_numbers=(dims, ((), ())), preferred_element_type=F32)
    return d(ah, bh) + (d(ah, bl) + d(al, bh))


@jax.custom_vjp
def _dot3(a, b):
    return _dot3_raw(a, b, ((1,), (0,)))


def _dot3_fwd(a, b):
    return _dot3(a, b), (a, b)


def _dot3_bwd(res, ct):
    a, b = res
    return _dot3_raw(ct, b, ((1,), (1,))), _dot3_raw(a, ct, ((0,), (0,)))


_dot3.defvjp(_dot3_fwd, _dot3_bwd)


def _mm(a, b, *, mode, name, bias=None, add=None, add_scale=1.0, out_dtype=F32, b_off=0, b_len=None):
    if mode == "nn":
        (m, k), (k2, n) = a.shape, (b.shape[0], b_len or b.shape[1])
    elif mode == "nt":
        (m, k), (n, k2) = a.shape, (b.shape[0], b_len or b.shape[1])
    else:
        (k, m), (k2, n) = a.shape, b.shape
        assert b_off == 0 and b_len is None
    assert k == k2, (a.shape, b.shape, mode)
    tn = _tile(n, 1408)
    tm = _tile(m, 1024 if tn >= 1024 else 2048)
    tk = _tile(k, 512) if k % 512 == 0 else _tile(k, 1408)
    nk = k // tk
    on, ok = (b_off // tn, 0) if mode == "nn" else (0, b_off // tk)
    assert b_off == (on * tn if mode == "nn" else ok * tk), (b_off, tn, tk)
    dims = {"nn": ((1,), (0,)), "nt": ((1,), (1,)), "tn": ((0,), (0,))}[mode]
    a_spec = pl.BlockSpec((tk, tm), lambda i, j, kk: (kk, i)) if mode == "tn" else pl.BlockSpec((tm, tk), lambda i, j, kk: (i, kk))
    b_spec = (pl.BlockSpec((tn, tk), lambda i, j, kk: (j, kk + ok)) if mode == "nt"
              else pl.BlockSpec((tk, tn), lambda i, j, kk: (kk, j + on)))
    in_specs = [a_spec, b_spec]
    operands = [a, b]
    if bias is not None:
        in_specs.append(pl.BlockSpec((1, tn), lambda i, j, kk: (0, j)))
        operands.append(bias.reshape(1, n))
    if add is not None:
        in_specs.append(pl.BlockSpec((tm, tn), lambda i, j, kk: (i, j)))
        operands.append(add)
    has_bias, has_add = bias is not None, add is not None

    def body(*refs):
        a_ref, b_ref = refs[0], refs[1]
        pos = 2
        bias_ref = add_ref = None
        if has_bias:
            bias_ref = refs[pos]
            pos += 1
        if has_add:
            add_ref = refs[pos]
            pos += 1
        o_ref, acc_ref = refs[pos], refs[pos + 1]
        kk = pl.program_id(2)

        @pl.when(kk == 0)
        def _():
            acc_ref[...] = jnp.zeros_like(acc_ref)

        acc_ref[...] += _raw_dot(a_ref[...], b_ref[...], dims)

        @pl.when(kk == nk - 1)
        def _():
            r = acc_ref[...]
            if has_bias:
                r = r + bias_ref[...]
            if has_add:
                r = r + add_scale * add_ref[...].astype(F32)
            o_ref[...] = r.astype(out_dtype)

    return pl.pallas_call(
        body, name=name, grid=(m // tm, n // tn, nk),
        in_specs=in_specs, out_specs=pl.BlockSpec((tm, tn), lambda i, j, kk: (i, j)),
        out_shape=jax.ShapeDtypeStruct((m, n), out_dtype),
        scratch_shapes=[pltpu.VMEM((tm, tn), F32)],
        compiler_params=_params(("parallel", "parallel", "arbitrary")),
    )(*operands)


def _ln_fwd(x, r, g, b, *, name):
    m, d = x.shape
    tm = 256

    def body(x_ref, r_ref, g_ref, b_ref, y_ref, yb_ref):
        s = DEEPNORM_ALPHA * x_ref[...] + r_ref[...]
        mu = jnp.mean(s, axis=-1, keepdims=True)
        c = s - mu
        var = jnp.mean(c * c, axis=-1, keepdims=True)
        y = c * lax.rsqrt(var + LN_EPS) * g_ref[...] + b_ref[...]
        y_ref[...] = y
        yb_ref[...] = y.astype(BF16)

    row = pl.BlockSpec((tm, d), lambda i: (i, 0))
    vec = pl.BlockSpec((1, d), lambda i: (0, 0))
    return pl.pallas_call(
        body, name=name, grid=(m // tm,), in_specs=[row, row, vec, vec], out_specs=[row, row],
        out_shape=[jax.ShapeDtypeStruct((m, d), F32), jax.ShapeDtypeStruct((m, d), BF16)],
        compiler_params=_params(("parallel",)),
    )(x, r, g.reshape(1, d), b.reshape(1, d))


def _ln_bwd(x, r, g, dy, *, name):
    m, d = x.shape
    tm = 256

    def body(x_ref, r_ref, g_ref, dy_ref, ds_ref, dsb_ref, dg_ref, db_ref):
        i = pl.program_id(0)
        s = DEEPNORM_ALPHA * x_ref[...] + r_ref[...]
        mu = jnp.mean(s, axis=-1, keepdims=True)
        c = s - mu
        var = jnp.mean(c * c, axis=-1, keepdims=True)
        rstd = lax.rsqrt(var + LN_EPS)
        xhat = c * rstd
        dy_ = dy_ref[...]
        dxhat = dy_ * g_ref[...]
        m1 = jnp.mean(dxhat, axis=-1, keepdims=True)
        m2 = jnp.mean(dxhat * xhat, axis=-1, keepdims=True)
        ds = rstd * (dxhat - m1 - xhat * m2)
        ds_ref[...] = ds
        dsb_ref[...] = ds.astype(BF16)

        @pl.when(i == 0)
        def _():
            dg_ref[...] = jnp.zeros_like(dg_ref)
            db_ref[...] = jnp.zeros_like(db_ref)

        dg_ref[...] += jnp.sum(dy_ * xhat, axis=0, keepdims=True)
        db_ref[...] += jnp.sum(dy_, axis=0, keepdims=True)

    row = pl.BlockSpec((tm, d), lambda i: (i, 0))
    vec = pl.BlockSpec((1, d), lambda i: (0, 0))
    return pl.pallas_call(
        body, name=name, grid=(m // tm,), in_specs=[row, row, vec, row], out_specs=[row, row, vec, vec],
        out_shape=[jax.ShapeDtypeStruct((m, d), F32), jax.ShapeDtypeStruct((m, d), BF16),
                   jax.ShapeDtypeStruct((1, d), F32), jax.ShapeDtypeStruct((1, d), F32)],
        compiler_params=_params(("arbitrary",)),
    )(x, r, g.reshape(1, d), dy)


def _loss_head(y, t):
    m, d = y.shape
    tm = 256

    def body(y_ref, t_ref, dy_ref, l_ref):
        i = pl.program_id(0)
        e = y_ref[...] - t_ref[...]
        dy_ref[...] = e * (1.0 / d)

        @pl.when(i == 0)
        def _():
            l_ref[...] = jnp.zeros_like(l_ref)

        part = jnp.sum(jnp.sum(e * e, axis=-1, keepdims=True) * (0.5 / d), axis=0, keepdims=True)
        l_ref[...] += jnp.broadcast_to(part, l_ref.shape)

    row = pl.BlockSpec((tm, d), lambda i: (i, 0))
    dy, l = pl.pallas_call(
        body, name="loss_head", grid=(m // tm,), in_specs=[row, row],
        out_specs=[row, pl.BlockSpec((SUBLANES, LANES), lambda i: (0, 0))],
        out_shape=[jax.ShapeDtypeStruct((m, d), F32), jax.ShapeDtypeStruct((SUBLANES, LANES), F32)],
        compiler_params=_params(("arbitrary",)),
    )(y, t)
    return dy, l[0, 0]


def _conv_ext(ext, w, width):
    acc = None
    for k in range(width):
        s = width - 1 - k
        xs = ext if s == 0 else pltpu.roll(ext, s, 0)
        term = xs[SUBLANES:, :] * w[k:k + 1, :]
        acc = term if acc is None else acc + term
    return acc


def _ffn_act_fwd(hug, huv, wg, wv, bg, bv):
    m, fp = hug.shape
    ts, tc = 512, 512
    hb = ts // SUBLANES

    def body(g_ref, gh_ref, v_ref, vh_ref, wg_ref, wv_ref, bg_ref, bv_ref, o_ref):
        i = pl.program_id(1)
        keep = (i > 0).astype(F32)
        hg = _conv_ext(jnp.concatenate([gh_ref[...] * keep, g_ref[...]], axis=0), wg_ref[...], FFN_CONV) + bg_ref[...]
        hv = _conv_ext(jnp.concatenate([vh_ref[...] * keep, v_ref[...]], axis=0), wv_ref[...], FFN_CONV) + bv_ref[...]
        o_ref[...] = (_silu(hg) * hv).astype(BF16)

    tile = pl.BlockSpec((ts, tc), lambda j, i: (i, j))
    halo = pl.BlockSpec((SUBLANES, tc), lambda j, i: (jnp.maximum(i * hb - 1, 0), j))
    wsp = pl.BlockSpec((FFN_CONV, tc), lambda j, i: (0, j))
    bsp = pl.BlockSpec((1, tc), lambda j, i: (0, j))
    return pl.pallas_call(
        body, name="ffn_act_fwd", grid=(fp // tc, m // ts),
        in_specs=[tile, halo, tile, halo, wsp, wsp, bsp, bsp], out_specs=tile,
        out_shape=jax.ShapeDtypeStruct((m, fp), BF16),
        compiler_params=_params(("parallel", "parallel")),
    )(hug, hug, huv, huv, wg, wv, bg, bv)


def _ffn_act_bwd(hug, huv, wg, wv, bg, bv, dact):
    m, fp = hug.shape
    ts, tc = 512, 512
    hb = ts // SUBLANES
    ni = m // ts

    def body(g_ref, gp_ref, gn_ref, v_ref, vp_ref, vn_ref, wg_ref, wv_ref, bg_ref, bv_ref, da_ref, dan_ref,
             dg_ref, dv_ref, dwg_ref, dwv_ref, dbg_ref, dbv_ref):
        i = pl.program_id(1)
        first = (i > 0).astype(F32)
        last = (i < ni - 1).astype(F32)
        wg_, wv_ = wg_ref[...], wv_ref[...]
        ext_g = jnp.concatenate([gp_ref[...] * first, g_ref[...], gn_ref[...]], axis=0)
        ext_v = jnp.concatenate([vp_ref[...] * first, v_ref[...], vn_ref[...]], axis=0)
        hg = _conv_ext(ext_g, wg_, FFN_CONV) + bg_ref[...]
        hv = _conv_ext(ext_v, wv_, FFN_CONV) + bv_ref[...]
        da = jnp.concatenate([da_ref[...], dan_ref[...] * last], axis=0)
        sg = _sigmoid(hg)
        dhg = da * hv * (sg * (1.0 + hg * (1.0 - sg)))
        dhv = da * (hg * sg)
        n_ext = ts + SUBLANES

        def back(dh, w_, ext):
            dx = None
            dw_rows = []
            for k in range(FFN_CONV):
                s = FFN_CONV - 1 - k
                sh = dh if s == 0 else pltpu.roll(dh, n_ext - s, 0)
                term = sh[:ts, :] * w_[k:k + 1, :]
                dx = term if dx is None else dx + term
                xs = ext if s == 0 else pltpu.roll(ext, s, 0)
                dw_rows.append(jnp.sum(dh[:ts, :] * xs[SUBLANES:SUBLANES + ts, :], axis=0, keepdims=True))
            return dx, jnp.concatenate(dw_rows, axis=0), jnp.sum(dh[:ts, :], axis=0, keepdims=True)

        dxg, dwg, dbg = back(dhg, wg_, ext_g)
        dxv, dwv, dbv = back(dhv, wv_, ext_v)
        dg_ref[...] = dxg.astype(BF16)
        dv_ref[...] = dxv.astype(BF16)

        @pl.when(i == 0)
        def _():
            dwg_ref[...] = jnp.zeros_like(dwg_ref)
            dwv_ref[...] = jnp.zeros_like(dwv_ref)
            dbg_ref[...] = jnp.zeros_like(dbg_ref)
            dbv_ref[...] = jnp.zeros_like(dbv_ref)

        dwg_ref[...] += dwg
        dwv_ref[...] += dwv
        dbg_ref[...] += dbg
        dbv_ref[...] += dbv

    tile = pl.BlockSpec((ts, tc), lambda j, i: (i, j))
    prev = pl.BlockSpec((SUBLANES, tc), lambda j, i: (jnp.maximum(i * hb - 1, 0), j))
    nxt = pl.BlockSpec((SUBLANES, tc), lambda j, i: (jnp.minimum((i + 1) * hb, m // SUBLANES - 1), j))
    wsp = pl.BlockSpec((FFN_CONV, tc), lambda j, i: (0, j))
    bsp = pl.BlockSpec((1, tc), lambda j, i: (0, j))
    return pl.pallas_call(
        body, name="ffn_act_bwd", grid=(fp // tc, ni),
        in_specs=[tile, prev, nxt, tile, prev, nxt, wsp, wsp, bsp, bsp, tile, nxt],
        out_specs=[tile, tile, wsp, wsp, bsp, bsp],
        out_shape=[jax.ShapeDtypeStruct((m, fp), BF16), jax.ShapeDtypeStruct((m, fp), BF16),
                   jax.ShapeDtypeStruct((FFN_CONV, fp), F32), jax.ShapeDtypeStruct((FFN_CONV, fp), F32),
                   jax.ShapeDtypeStruct((1, fp), F32), jax.ShapeDtypeStruct((1, fp), F32)],
        compiler_params=_params(("parallel", "arbitrary")),
    )(hug, hug, hug, huv, huv, huv, wg, wv, bg, bv, dact, dact)


def _log1p_small(e):
    u = 1.0 + e
    safe = jnp.where(u == 1.0, 1.0, u - 1.0)
    return jnp.where(u == 1.0, e, jnp.log(u) * (e / safe))


def _log_sigmoid(x):
    return jnp.minimum(x, 0.0) - _log1p_small(jnp.exp(-jnp.abs(x)))


def _exact_tri_dot(tri, x):
    p1 = x.astype(BF16)
    r1 = x - p1.astype(F32)
    p2 = r1.astype(BF16)
    p3 = (r1 - p2.astype(F32)).astype(BF16)
    d = lambda p: lax.dot_general(tri, p, (((1,), (0,)), ((), ())), preferred_element_type=F32)
    return d(p1) + (d(p2) + d(p3))


def _fox_prefix_fwd(small):
    m, w = small.shape
    tb = 512
    nb = m // tb

    def body(f_ref, c_ref, carry_ref):
        i = pl.program_id(0)

        @pl.when(i == 0)
        def _():
            carry_ref[...] = jnp.zeros_like(carry_ref)

        lf = _log_sigmoid(f_ref[...])
        r = lax.broadcasted_iota(jnp.int32, (tb, tb), 0)
        c = lax.broadcasted_iota(jnp.int32, (tb, tb), 1)
        tri = (c <= r).astype(BF16)
        c_ref[...] = _exact_tri_dot(tri, lf) + carry_ref[0:1, :]
        carry_ref[0:1, :] += jnp.sum(lf, axis=0, keepdims=True)

    blk = pl.BlockSpec((tb, w), lambda i: (i, 0))
    return pl.pallas_call(
        body, name="fox_prefix_fwd", grid=(nb,), in_specs=[blk], out_specs=blk,
        out_shape=jax.ShapeDtypeStruct((m, w), F32), scratch_shapes=[pltpu.VMEM((SUBLANES, w), F32)],
        compiler_params=_params(("arbitrary",)),
    )(small)


def _fox_prefix_bwd(small, dc):
    m, w = small.shape
    tb = 512
    nb = m // tb

    def body(f_ref, dc_ref, df_ref, carry_ref):
        i = pl.program_id(0)

        @pl.when(i == 0)
        def _():
            carry_ref[...] = jnp.zeros_like(carry_ref)

        d = dc_ref[...]
        r = lax.broadcasted_iota(jnp.int32, (tb, tb), 0)
        c = lax.broadcasted_iota(jnp.int32, (tb, tb), 1)
        tri = (c >= r).astype(BF16)
        dlf = _exact_tri_dot(tri, d) + carry_ref[0:1, :]
        carry_ref[0:1, :] += jnp.sum(d, axis=0, keepdims=True)
        lane = lax.broadcasted_iota(jnp.int32, (tb, w), 1)
        df_ref[...] = jnp.where(lane < N_HEADS, dlf * _sigmoid(-f_ref[...]), 0.0)

    blk = pl.BlockSpec((tb, w), lambda i: (nb - 1 - i, 0))
    return pl.pallas_call(
        body, name="fox_prefix_bwd", grid=(nb,), in_specs=[blk, blk], out_specs=blk,
        out_shape=jax.ShapeDtypeStruct((m, w), F32), scratch_shapes=[pltpu.VMEM((SUBLANES, w), F32)],
        compiler_params=_params(("arbitrary",)),
    )(small, dc)


ATT_T = 512
NEG = -1e30


def _lane_pick(block, h):
    lane = lax.broadcasted_iota(jnp.int32, block.shape, 1)
    return jnp.sum(jnp.where(lane == h, block, 0.0), axis=1, keepdims=True)


def _att_scores(q, k, cq, ck):
    return _raw_dot(q, k, ((1,), (1,))) * (HEAD_DIM ** -0.5) + cq - ck


def _att_probs(s, lse, diag):
    p = jnp.exp(s - lse)
    if not diag:
        return p
    t = s.shape[0]
    return jnp.where(lax.broadcasted_iota(jnp.int32, (t, t), 0) >= lax.broadcasted_iota(jnp.int32, (t, t), 1), p, 0.0)


def _fox_fwd(q, k, v, c_col, c_row):
    m = q.shape[0]
    t = ATT_T
    nb = m // t

    def body(q_ref, k_ref, v_ref, cc_ref, cr_ref, o_ref, o32_ref, lse_ref):
        h, i = pl.program_id(0), pl.program_id(1)
        q = q_ref[...]
        cq = _lane_pick(cc_ref[...], h)

        def step(j, carry, diag):
            mx, l, acc = carry
            rows = pl.ds(pl.multiple_of(j * t, t), t)
            s = _att_scores(q, k_ref[rows, :], cq, cr_ref[h * nb + j])
            if diag:
                s = jnp.where(lax.broadcasted_iota(jnp.int32, (t, t), 0) >= lax.broadcasted_iota(jnp.int32, (t, t), 1), s, NEG)
            mn = jnp.maximum(mx, jnp.max(s, axis=1, keepdims=True))
            p = jnp.exp(s - mn)
            a = jnp.exp(mx - mn)
            return mn, a * l + jnp.sum(p, axis=1, keepdims=True), a * acc + _raw_dot(p, v_ref[rows, :], ((1,), (0,)))

        init = (jnp.full((t, 1), NEG, F32), jnp.zeros((t, 1), F32), jnp.zeros((t, HEAD_DIM), F32))
        mx, l, acc = step(i, lax.fori_loop(0, i, functools.partial(step, diag=False), init), True)
        o = acc / l
        o_ref[...] = o.astype(BF16)
        o32_ref[...] = o
        lse_ref[0] = jnp.broadcast_to(mx + jnp.log(l), (t, LANES))

    return pl.pallas_call(
        body, name="fox_fwd", grid=(N_HEADS, nb),
        in_specs=[pl.BlockSpec((t, HEAD_DIM), lambda h, i: (i, h)),
                  pl.BlockSpec((m, HEAD_DIM), lambda h, i: (0, h)),
                  pl.BlockSpec((m, HEAD_DIM), lambda h, i: (0, h)),
                  pl.BlockSpec((t, LANES), lambda h, i: (i, 0)),
                  pl.BlockSpec((N_HEADS * nb, 1, t), lambda h, i: (0, 0, 0))],
        out_specs=[pl.BlockSpec((t, HEAD_DIM), lambda h, i: (i, h)), pl.BlockSpec((t, HEAD_DIM), lambda h, i: (i, h)),
                   pl.BlockSpec((1, t, LANES), lambda h, i: (h, i, 0))],
        out_shape=[jax.ShapeDtypeStruct((m, WIDTH), BF16), jax.ShapeDtypeStruct((m, WIDTH), F32),
                   jax.ShapeDtypeStruct((N_HEADS, m, LANES), F32)],
        compiler_params=_params(("parallel", "parallel")),
    )(q, k, v, c_col, c_row)


def _fox_bwd_kv(q, k, v, o, do, lse, c_col, c_row):
    m = q.shape[0]
    t = ATT_T
    nb = m // t

    def body(q_ref, k_ref, v_ref, o_ref, do_ref, lse_ref, cc_ref, cr_ref, dk_ref, dv_ref, dc_ref):
        h, j = pl.program_id(0), pl.program_id(1)
        k, v = k_ref[...], v_ref[...]
        ck = cr_ref[h * nb + j]

        def step(i, carry, diag):
            dk, dv, dc = carry
            rows = pl.ds(pl.multiple_of(i * t, t), t)
            q, do_ = q_ref[rows, :], do_ref[rows, :]
            cq = _lane_pick(cc_ref[rows, :], h)
            lse_i = jnp.max(lse_ref[0, rows, :], axis=1, keepdims=True)
            dd = jnp.sum(do_.astype(F32) * o_ref[rows, :].astype(F32), axis=1, keepdims=True)
            p = _att_probs(_att_scores(q, k, cq, ck), lse_i, diag)
            dp = _raw_dot(do_, v, ((1,), (1,)))
            ds = p * (dp - dd)
            return (dk + _raw_dot(ds, q, ((0,), (0,))), dv + _raw_dot(p, do_, ((0,), (0,))),
                    dc - jnp.sum(ds, axis=0, keepdims=True))

        init = (jnp.zeros((t, HEAD_DIM), F32), jnp.zeros((t, HEAD_DIM), F32), jnp.zeros((1, t), F32))
        dk, dv, dc = lax.fori_loop(j + 1, nb, functools.partial(step, diag=False), step(j, init, True))
        dk_ref[...] = (dk * (HEAD_DIM ** -0.5)).astype(BF16)
        dv_ref[...] = dv.astype(BF16)
        dc_ref[0] = dc

    whole = lambda col: pl.BlockSpec((m, HEAD_DIM), col)
    return pl.pallas_call(
        body, name="fox_bwd_kv", grid=(N_HEADS, nb),
        in_specs=[whole(lambda h, j: (0, h)),
                  pl.BlockSpec((t, HEAD_DIM), lambda h, j: (j, h)),
                  pl.BlockSpec((t, HEAD_DIM), lambda h, j: (j, h)),
                  whole(lambda h, j: (0, h)), whole(lambda h, j: (0, h)),
                  pl.BlockSpec((1, m, LANES), lambda h, j: (h, 0, 0)),
                  pl.BlockSpec((m, LANES), lambda h, j: (0, 0)),
                  pl.BlockSpec((N_HEADS * nb, 1, t), lambda h, j: (0, 0, 0))],
        out_specs=[pl.BlockSpec((t, HEAD_DIM), lambda h, j: (j, h)),
                   pl.BlockSpec((t, HEAD_DIM), lambda h, j: (j, h)),
                   pl.BlockSpec((1, 1, t), lambda h, j: (h * nb + j, 0, 0))],
        out_shape=[jax.ShapeDtypeStruct((m, WIDTH), BF16), jax.ShapeDtypeStruct((m, WIDTH), BF16),
                   jax.ShapeDtypeStruct((N_HEADS * nb, 1, t), F32)],
        compiler_params=_params(("parallel", "parallel")),
    )(q, k, v, o, do, lse, c_col, c_row)


def _fox_bwd_q(q, k, v, o, do, lse, c_col, c_row):
    m = q.shape[0]
    t = ATT_T
    nb = m // t

    def body(q_ref, k_ref, v_ref, o_ref, do_ref, lse_ref, cc_ref, cr_ref, dq_ref, dcq_ref):
        h, i = pl.program_id(0), pl.program_id(1)
        q, do_ = q_ref[...], do_ref[...]
        cq = _lane_pick(cc_ref[...], h)
        lse_i = jnp.max(lse_ref[0], axis=1, keepdims=True)
        dd = jnp.sum(do_.astype(F32) * o_ref[...].astype(F32), axis=1, keepdims=True)

        def step(j, carry, diag):
            dq, dcq = carry
            rows = pl.ds(pl.multiple_of(j * t, t), t)
            k = k_ref[rows, :]
            p = _att_probs(_att_scores(q, k, cq, cr_ref[h * nb + j]), lse_i, diag)
            ds = p * (_raw_dot(do_, v_ref[rows, :], ((1,), (1,))) - dd)
            return dq + _raw_dot(ds, k, ((1,), (0,))), dcq + jnp.sum(ds, axis=1, keepdims=True)

        init = (jnp.zeros((t, HEAD_DIM), F32), jnp.zeros((t, 1), F32))
        dq, dcq = step(i, lax.fori_loop(0, i, functools.partial(step, diag=False), init), True)
        dq_ref[...] = (dq * (HEAD_DIM ** -0.5)).astype(BF16)
        dcq_ref[0] = jnp.broadcast_to(dcq, (t, LANES))

    tile = lambda col: pl.BlockSpec((t, HEAD_DIM), col)
    return pl.pallas_call(
        body, name="fox_bwd_q", grid=(N_HEADS, nb),
        in_specs=[tile(lambda h, i: (i, h)),
                  pl.BlockSpec((m, HEAD_DIM), lambda h, i: (0, h)),
                  pl.BlockSpec((m, HEAD_DIM), lambda h, i: (0, h)),
                  tile(lambda h, i: (i, h)), tile(lambda h, i: (i, h)),
                  pl.BlockSpec((1, t, LANES), lambda h, i: (h, i, 0)),
                  pl.BlockSpec((t, LANES), lambda h, i: (i, 0)),
                  pl.BlockSpec((N_HEADS * nb, 1, t), lambda h, i: (0, 0, 0))],
        out_specs=[tile(lambda h, i: (i, h)), pl.BlockSpec((1, t, LANES), lambda h, i: (h, i, 0))],
        out_shape=[jax.ShapeDtypeStruct((m, WIDTH), BF16), jax.ShapeDtypeStruct((N_HEADS, m, LANES), F32)],
        compiler_params=_params(("parallel", "parallel")),
    )(q, k, v, o, do, lse, c_col, c_row)


def _row_form(c_col, t):
    m = c_col.shape[0]
    return c_col[:, :N_HEADS].T.reshape(N_HEADS * (m // t), 1, t)


def _col_form(c_row, m):
    c = c_row.reshape(N_HEADS, m).T
    return jnp.pad(c, ((0, 0), (0, LANES - N_HEADS)))


SGU_T = 512


def _sgu_tile(u, v, lg, lb, w, bsb):
    r = lax.broadcasted_iota(jnp.int32, (SGU_SPAN, SGU_SPAN), 0) // CHUNK
    c = lax.broadcasted_iota(jnp.int32, (SGU_SPAN, SGU_SPAN), 1) // CHUNK
    wm = jnp.where(r >= c, w, 0.0)
    outs = []
    for n in range(u.shape[0] // SGU_SPAN):
        rows = slice(n * SGU_SPAN, (n + 1) * SGU_SPAN)
        vs = v[rows]
        mu = jnp.mean(vs, axis=-1, keepdims=True)
        d = vs - mu
        var = jnp.mean(d * d, axis=-1, keepdims=True)
        vg = d * lax.rsqrt(var + LN_EPS) * lg + lb
        outs.append(u[rows] * (_bdot(wm, vg) + bsb))
    return jnp.concatenate(outs, axis=0)


def _sgu_specs(m):
    t = SGU_T
    tile = lambda off: pl.BlockSpec((t, HEAD_DIM), lambda g, i: (i, off + g))
    vec = pl.BlockSpec((1, 1, HEAD_DIM), lambda g, i: (g, 0, 0))
    mat = pl.BlockSpec((1, SGU_SPAN, SGU_SPAN), lambda g, i: (g, 0, 0))
    return tile, vec, mat


def _sgu_operands(ln_g, ln_b, w, b):
    return (ln_g.reshape(N_HEADS, 1, HEAD_DIM), ln_b.reshape(N_HEADS, 1, HEAD_DIM), w,
            jnp.broadcast_to(b[:, :, None], (N_HEADS, SGU_SPAN, SGU_SPAN)))


def _sgu_fwd(u, v, ln_g, ln_b, w, b):
    m = u.shape[0]
    tile, vec, mat = _sgu_specs(m)

    def body(u_ref, v_ref, lg_ref, lb_ref, w_ref, b_ref, o_ref):
        o_ref[...] = _sgu_tile(u_ref[...], v_ref[...], lg_ref[0], lb_ref[0], w_ref[0], b_ref[0]).astype(BF16)

    return pl.pallas_call(
        body, name="sgu_fwd", grid=(N_HEADS, m // SGU_T),
        in_specs=[tile(0), tile(0), vec, vec, mat, mat], out_specs=tile(0),
        out_shape=jax.ShapeDtypeStruct((m, WIDTH), BF16),
        compiler_params=_params(("parallel", "parallel")),
    )(u, v, *_sgu_operands(ln_g, ln_b, w, b))


def _sgu_bwd(u, v, ln_g, ln_b, w, b, dy):
    m = u.shape[0]
    tile, vec, mat = _sgu_specs(m)

    def body(u_ref, v_ref, lg_ref, lb_ref, w_ref, b_ref, dy_ref, du_ref, dv_ref, dlg_ref, dlb_ref, dw_ref, db_ref):
        i = pl.program_id(1)
        _, vjp = jax.vjp(_sgu_tile, u_ref[...], v_ref[...], lg_ref[0], lb_ref[0], w_ref[0], b_ref[0])
        du, dv, dlg, dlb, dw, dbsb = vjp(dy_ref[...])
        du_ref[...] = du.astype(BF16)
        dv_ref[...] = dv.astype(BF16)

        @pl.when(i == 0)
        def _():
            dlg_ref[...] = jnp.zeros_like(dlg_ref)
            dlb_ref[...] = jnp.zeros_like(dlb_ref)
            dw_ref[...] = jnp.zeros_like(dw_ref)
            db_ref[...] = jnp.zeros_like(db_ref)

        dlg_ref[0] += dlg
        dlb_ref[0] += dlb
        dw_ref[0] += dw
        db_ref[0] += jnp.broadcast_to(jnp.sum(dbsb, axis=1, keepdims=True), (SGU_SPAN, SGU_SPAN))

    vshape = jax.ShapeDtypeStruct((N_HEADS, 1, HEAD_DIM), F32)
    mshape = jax.ShapeDtypeStruct((N_HEADS, SGU_SPAN, SGU_SPAN), F32)
    return pl.pallas_call(
        body, name="sgu_bwd", grid=(N_HEADS, m // SGU_T),
        in_specs=[tile(0), tile(0), vec, vec, mat, mat, tile(0)],
        out_specs=[tile(0), tile(0), vec, vec, mat, mat],
        out_shape=[jax.ShapeDtypeStruct((m, WIDTH), BF16), jax.ShapeDtypeStruct((m, WIDTH), BF16),
                   vshape, vshape, mshape, mshape],
        compiler_params=_params(("parallel", "arbitrary")),
    )(u, v, *_sgu_operands(ln_g, ln_b, w, b), dy)


def _merge_fwd(ys, ws, gates):
    m, d = gates[0].shape
    tm, tn = 512, _tile(d, 512)

    def body(ya, yb, yc, wa, wb, wc, ga, gb, gc, o_ref, za, zb, zc):
        acc = None
        for y_ref, w_ref, g_ref, z_ref in ((ya, wa, ga, za), (yb, wb, gb, zb), (yc, wc, gc, zc)):
            z = _raw_dot(y_ref[...], w_ref[...], ((1,), (0,)))
            z_ref[...] = z
            term = _sigmoid(g_ref[...]) * z
            acc = term if acc is None else acc + term
        o_ref[...] = acc.astype(BF16)

    ysp = pl.BlockSpec((tm, WIDTH), lambda i, j: (i, 0))
    wsp = pl.BlockSpec((WIDTH, tn), lambda i, j: (0, j))
    tsp = pl.BlockSpec((tm, tn), lambda i, j: (i, j))
    zs = jax.ShapeDtypeStruct((m, d), F32)
    return pl.pallas_call(
        body, name="merge_fwd", grid=(m // tm, d // tn),
        in_specs=[ysp] * 3 + [wsp] * 3 + [tsp] * 3, out_specs=[tsp] * 4,
        out_shape=[jax.ShapeDtypeStruct((m, d), BF16), zs, zs, zs],
        compiler_params=_params(("parallel", "parallel")),
    )(*ys, *ws, *gates)


def _merge_bwd(dmerged, zs, gates):
    m, d = dmerged.shape
    tm, tn = 512, _tile(d, 1024)

    def body(dm_ref, za, zb, zc, ga, gb, gc, dza, dzb, dzc, dga, dgb, dgc):
        dm = dm_ref[...]
        for z_ref, g_ref, dz_ref, dg_ref in ((za, ga, dza, dga), (zb, gb, dzb, dgb), (zc, gc, dzc, dgc)):
            s = _sigmoid(g_ref[...])
            dz_ref[...] = (dm * s).astype(BF16)
            dg_ref[...] = (dm * z_ref[...] * (s * (1.0 - s))).astype(BF16)

    tsp = pl.BlockSpec((tm, tn), lambda i, j: (i, j))
    bs = jax.ShapeDtypeStruct((m, d), BF16)
    return pl.pallas_call(
        body, name="merge_bwd", grid=(m // tm, d // tn),
        in_specs=[tsp] * 7, out_specs=[tsp] * 6, out_shape=[bs] * 6,
        compiler_params=_params(("parallel", "parallel")),
    )(dmerged, *zs, *gates)


GDN_T = 512
GDN_A_LANE = N_HEADS
GDN_B_LANE = 2 * N_HEADS


@functools.partial(jax.custom_vjp, nondiff_argnums=(2,))
def _conv_halo(ext, w, width):
    return _conv_ext(ext, w, width)


def _conv_halo_fwd(ext, w, width):
    return _conv_ext(ext, w, width), (ext, w)


def _conv_halo_bwd(width, res, dy):
    ext, w = res
    n = ext.shape[0]
    dyp = jnp.concatenate([jnp.zeros((SUBLANES, dy.shape[1]), dy.dtype), dy], axis=0)
    dext = None
    dw_rows = []
    for k in range(width):
        s = width - 1 - k
        sh = dyp if s == 0 else pltpu.roll(dyp, n - s, 0)
        term = sh * w[k:k + 1, :]
        dext = term if dext is None else dext + term
        xs = ext if s == 0 else pltpu.roll(ext, s, 0)
        dw_rows.append(jnp.sum(dy * xs[SUBLANES:, :], axis=0, keepdims=True))
    return dext, jnp.concatenate(dw_rows, axis=0)


_conv_halo.defvjp(_conv_halo_fwd, _conv_halo_bwd)


@jax.custom_jvp
def _softplus(x):
    return jnp.maximum(x, 0.0) + _log1p_small(jnp.exp(-jnp.abs(x)))


@_softplus.defjvp
def _softplus_jvp(primals, tangents):
    (x,), (t,) = primals, tangents
    return _softplus(x), t * _sigmoid(x)


def _tri_inv(a, ii, jj):
    eye = (ii == jj).astype(F32)
    same = (ii // 16) == (jj // 16)
    ad = jnp.where(same, a, 0.0)
    ao = a - ad
    b1 = -ad
    b2 = _dot3(b1, b1)
    b4 = _dot3(b2, b2)
    b8 = _dot3(b4, b4)
    dinv = _dot3(_dot3(_dot3(eye + b1, eye + b2), eye + b4), eye + b8)
    n1 = _dot3(dinv, ao)
    n2 = _dot3(n1, n1)
    return _dot3(_dot3(eye - n1, eye + n2), dinv)


def _gdn_chunk(s_in, eq, ek, ev, a_col, b_col, gate, wq, wk, wv, a_log, dt_bias, ng):
    q = _silu(_conv_halo(eq, wq, GDN_CONV))
    k = _silu(_conv_halo(ek, wk, GDN_CONV))
    v = _silu(_conv_halo(ev, wv, GDN_CONV))
    q = q * lax.rsqrt(jnp.sum(q * q, axis=-1, keepdims=True) + RMS_EPS) * (HEAD_DIM ** -0.5)
    k = k * lax.rsqrt(jnp.sum(k * k, axis=-1, keepdims=True) + RMS_EPS)
    g = -jnp.exp(a_log) * _softplus(a_col + dt_bias)
    beta = _sigmoid(b_col)
    ii = lax.broadcasted_iota(jnp.int32, (CHUNK, CHUNK), 0)
    jj = lax.broadcasted_iota(jnp.int32, (CHUNK, CHUNK), 1)
    gb = jnp.broadcast_to(g, (CHUNK, CHUNK))
    g_row = jnp.sum(jnp.where(ii == jj, gb, 0.0), axis=0, keepdims=True)
    gc_row = jnp.sum(jnp.where(ii <= jj, gb, 0.0), axis=0, keepdims=True)
    gc_col = jnp.sum(jnp.where(jj <= ii, jnp.broadcast_to(g_row, (CHUNK, CHUNK)), 0.0), axis=1, keepdims=True)
    g_last = jnp.sum(g, axis=0, keepdims=True)
    causal = ii >= jj
    decay = jnp.where(causal, jnp.exp(jnp.where(causal, gc_col - gc_row, 0.0)), 0.0)
    kb = k * beta
    a_kk = jnp.where(ii > jj, _bdot_nt(kb, k) * decay, 0.0)
    t_inv = _tri_inv(a_kk, ii, jj)
    u = _dot3(t_inv, v * beta)
    w = _dot3(t_inv, kb * jnp.exp(gc_col))
    qk = jnp.where(causal, _bdot_nt(q, k) * decay, 0.0)
    k_dec = k * jnp.exp(g_last - gc_col)
    q_dec = q * jnp.exp(gc_col)
    v_new = u - _bdot(w, s_in)
    o = _bdot(q_dec, s_in) + _bdot(qk, v_new)
    s_out = s_in * jnp.exp(g_last) + _bdot_tn(k_dec, v_new)
    o = o * lax.rsqrt(jnp.mean(o * o, axis=-1, keepdims=True) + RMS_EPS) * ng
    return s_out, o * _silu(gate)


GDN_HB = 4
GDN_W = GDN_HB * HEAD_DIM
GDN_LANES = tuple(slice(i * HEAD_DIM, (i + 1) * HEAD_DIM) for i in range(GDN_HB))


def _gdn_specs(m, rev):
    t = GDN_T
    ns = m // t
    hb = t // SUBLANES
    pos = (lambda s: ns - 1 - s) if rev else (lambda s: s)
    tile = pl.BlockSpec((t, GDN_W), lambda h, s: (pos(s), h))
    halo = pl.BlockSpec((SUBLANES, GDN_W), lambda h, s: (jnp.maximum(pos(s) * hb - 1, 0), h))
    small = pl.BlockSpec((t, LANES), lambda h, s: (pos(s), 0))
    wsp = pl.BlockSpec((GDN_CONV, GDN_W), lambda h, s: (0, h))
    sc = pl.BlockSpec((GDN_HB, 1, LANES), lambda h, s: (h, 0, 0))
    ngs = pl.BlockSpec((1, HEAD_DIM), lambda h, s: (0, 0))
    st = pl.BlockSpec((GDN_HB, t // CHUNK, HEAD_DIM, HEAD_DIM), lambda h, s: (h, pos(s), 0, 0))
    return tile, halo, small, wsp, sc, ngs, st


def _gdn_scalars(a_log, dt_bias):
    bc = lambda p: jnp.broadcast_to(p.reshape(N_HEADS, 1, 1), (N_HEADS, 1, LANES))
    return bc(a_log), bc(dt_bias)


def _gdn_fill_ext(xpad_ref, tiles, halos, keep):
    for sec in range(3):
        xpad_ref[sec, 0:SUBLANES, :] = halos[sec][...] * keep
        xpad_ref[sec, SUBLANES:, :] = tiles[sec][...]


def _gdn_fwd(xq, xk, xv, small, gate, wq, wk, wv, a_log, dt_bias, norm_g):
    m = xq.shape[0]
    t = GDN_T
    nc = t // CHUNK
    tile, halo, smallsp, wsp, sc, ngs, st = _gdn_specs(m, False)

    def body(q_ref, qh_ref, k_ref, kh_ref, v_ref, vh_ref, sm_ref, gate_ref, wq_ref, wk_ref, wv_ref,
             al_ref, dt_ref, ng_ref, y_ref, st_ref, xpad_ref, s_ref):
        hg, s = pl.program_id(0), pl.program_id(1)
        _gdn_fill_ext(xpad_ref, (q_ref, k_ref, v_ref), (qh_ref, kh_ref, vh_ref), (s > 0).astype(F32))

        @pl.when(s == 0)
        def _():
            s_ref[...] = jnp.zeros_like(s_ref)

        ng = ng_ref[...]
        par = [(wq_ref[:, ls], wk_ref[:, ls], wv_ref[:, ls], al_ref[i][:, 0:1], dt_ref[i][:, 0:1])
               for i, ls in enumerate(GDN_LANES)]

        def step(c, states):
            off = pl.multiple_of(c * CHUNK, CHUNK)
            ext = pl.ds(off, CHUNK + SUBLANES)
            sm = sm_ref[pl.ds(off, CHUNK), :]
            ins = [(xpad_ref[0, ext, ls], xpad_ref[1, ext, ls], xpad_ref[2, ext, ls],
                    _lane_pick(sm, GDN_A_LANE + hg * GDN_HB + i), _lane_pick(sm, GDN_B_LANE + hg * GDN_HB + i),
                    gate_ref[pl.ds(off, CHUNK), ls]) for i, ls in enumerate(GDN_LANES)]
            outs = [_gdn_chunk(states[i], *ins[i], *par[i], ng) for i in range(GDN_HB)]
            for i, ls in enumerate(GDN_LANES):
                st_ref[i, c] = states[i]
                y_ref[pl.ds(off, CHUNK), ls] = outs[i][1].astype(BF16)
            return tuple(o[0] for o in outs)

        states = lax.fori_loop(0, nc, step, tuple(s_ref[i] for i in range(GDN_HB)))
        for i in range(GDN_HB):
            s_ref[i] = states[i]

    return pl.pallas_call(
        body, name="gdn_fwd", grid=(N_HEADS // GDN_HB, m // t),
        in_specs=[tile, halo, tile, halo, tile, halo, smallsp, tile, wsp, wsp, wsp, sc, sc, ngs],
        out_specs=[tile, st],
        out_shape=[jax.ShapeDtypeStruct((m, WIDTH), BF16),
                   jax.ShapeDtypeStruct((N_HEADS, m // CHUNK, HEAD_DIM, HEAD_DIM), F32)],
        scratch_shapes=[pltpu.VMEM((3, t + SUBLANES, GDN_W), F32), pltpu.VMEM((GDN_HB, HEAD_DIM, HEAD_DIM), F32)],
        compiler_params=_params(("parallel", "arbitrary")),
    )(xq, xq, xk, xk, xv, xv, small, gate, wq, wk, wv, *_gdn_scalars(a_log, dt_bias), norm_g.reshape(1, HEAD_DIM))


def _gdn_bwd(xq, xk, xv, small, gate, wq, wk, wv, a_log, dt_bias, norm_g, states, dy):
    m = xq.shape[0]
    t = GDN_T
    nc = t // CHUNK
    tile, halo, smallsp, wsp, sc, ngs, st = _gdn_specs(m, True)

    def body(q_ref, qh_ref, k_ref, kh_ref, v_ref, vh_ref, sm_ref, gate_ref, wq_ref, wk_ref, wv_ref,
             al_ref, dt_ref, ng_ref, st_ref, dy_ref,
             dq_ref, dk_ref, dv_ref, dgate_ref, dab_ref, dwq_ref, dwk_ref, dwv_ref, dal_ref, ddt_ref, dng_ref,
             xpad_ref, dxpad_ref, ds_ref, carry_ref):
        hg, s = pl.program_id(0), pl.program_id(1)
        _gdn_fill_ext(xpad_ref, (q_ref, k_ref, v_ref), (qh_ref, kh_ref, vh_ref), (s < m // t - 1).astype(F32))

        @pl.when(s == 0)
        def _():
            ds_ref[...] = jnp.zeros_like(ds_ref)
            carry_ref[...] = jnp.zeros_like(carry_ref)
            for r in (dwq_ref, dwk_ref, dwv_ref, dal_ref, ddt_ref):
                r[...] = jnp.zeros_like(r)

        @pl.when((s == 0) & (hg == 0))
        def _():
            dng_ref[...] = jnp.zeros_like(dng_ref)

        dxpad_ref[:, 0:t, :] = jnp.zeros((3, t, GDN_W), F32)
        dxpad_ref[:, t:, :] = carry_ref[...]
        ng = ng_ref[...]
        par = [(wq_ref[:, ls], wk_ref[:, ls], wv_ref[:, ls], al_ref[i][:, 0:1], dt_ref[i][:, 0:1])
               for i, ls in enumerate(GDN_LANES)]
        lane = lax.broadcasted_iota(jnp.int32, (CHUNK, LANES), 1)

        def step(cc, carry):
            c = nc - 1 - cc
            off = pl.multiple_of(c * CHUNK, CHUNK)
            ext = pl.ds(off, CHUNK + SUBLANES)
            sm = sm_ref[pl.ds(off, CHUNK), :]
            ins = [(st_ref[i, c], xpad_ref[0, ext, ls], xpad_ref[1, ext, ls], xpad_ref[2, ext, ls],
                    _lane_pick(sm, GDN_A_LANE + hg * GDN_HB + i), _lane_pick(sm, GDN_B_LANE + hg * GDN_HB + i),
                    gate_ref[pl.ds(off, CHUNK), ls]) for i, ls in enumerate(GDN_LANES)]
            dys = [dy_ref[pl.ds(off, CHUNK), ls] for ls in GDN_LANES]
            gs = []
            for i in range(GDN_HB):
                _, vjp = jax.vjp(_gdn_chunk, *ins[i], *par[i], ng)
                gs.append(vjp((carry[i][0], dys[i])))
            for i, ls in enumerate(GDN_LANES):
                g = gs[i]
                for sec in range(3):
                    dxpad_ref[sec, ext, ls] += g[1 + sec]
                dab_ref[i, pl.ds(off, CHUNK), :] = jnp.where(lane == 0, g[4], jnp.where(lane == 1, g[5], 0.0))
                dgate_ref[pl.ds(off, CHUNK), ls] = g[6].astype(BF16)
            return tuple((gs[i][0], tuple(a + b for a, b in zip(carry[i][1], gs[i][7:]))) for i in range(GDN_HB))

        zero = (jnp.zeros((GDN_CONV, HEAD_DIM), F32),) * 3 + (jnp.zeros((1, 1), F32),) * 2 + (jnp.zeros((1, HEAD_DIM), F32),)
        done = lax.fori_loop(0, nc, step, tuple((ds_ref[i], zero) for i in range(GDN_HB)))
        carry_ref[...] = dxpad_ref[:, 0:SUBLANES, :]
        dq_ref[...] = dxpad_ref[0, SUBLANES:, :].astype(BF16)
        dk_ref[...] = dxpad_ref[1, SUBLANES:, :].astype(BF16)
        dv_ref[...] = dxpad_ref[2, SUBLANES:, :].astype(BF16)
        for i, ls in enumerate(GDN_LANES):
            d_state, acc = done[i]
            ds_ref[i] = d_state
            dwq_ref[:, ls] += acc[0]
            dwk_ref[:, ls] += acc[1]
            dwv_ref[:, ls] += acc[2]
            dal_ref[i] += jnp.broadcast_to(acc[3], (1, LANES))
            ddt_ref[i] += jnp.broadcast_to(acc[4], (1, LANES))
            dng_ref[...] += acc[5]

    bs = jax.ShapeDtypeStruct((m, WIDTH), BF16)
    ws = jax.ShapeDtypeStruct((GDN_CONV, WIDTH), F32)
    ss = jax.ShapeDtypeStruct((N_HEADS, 1, LANES), F32)
    ns = m // t
    dabsp = pl.BlockSpec((GDN_HB, t, LANES), lambda h, s: (h, ns - 1 - s, 0))
    return pl.pallas_call(
        body, name="gdn_bwd", grid=(N_HEADS // GDN_HB, ns),
        in_specs=[tile, halo, tile, halo, tile, halo, smallsp, tile, wsp, wsp, wsp, sc, sc, ngs, st, tile],
        out_specs=[tile, tile, tile, tile, dabsp, wsp, wsp, wsp, sc, sc, ngs],
        out_shape=[bs, bs, bs, bs, jax.ShapeDtypeStruct((N_HEADS, m, LANES), F32), ws, ws, ws, ss, ss,
                   jax.ShapeDtypeStruct((1, HEAD_DIM), F32)],
        scratch_shapes=[pltpu.VMEM((3, t + SUBLANES, GDN_W), F32), pltpu.VMEM((3, t + SUBLANES, GDN_W), F32),
                        pltpu.VMEM((GDN_HB, HEAD_DIM, HEAD_DIM), F32), pltpu.VMEM((3, SUBLANES, GDN_W), F32)],
        compiler_params=_params(("arbitrary", "arbitrary")),
    )(xq, xq, xk, xk, xv, xv, small, gate, wq, wk, wv, *_gdn_scalars(a_log, dt_bias), norm_g.reshape(1, HEAD_DIM),
      states, dy)


def _colsum(a, name):
    m, n = a.shape
    tm, tn = 512, _tile(n, 1408)

    def body(a_ref, o_ref):
        @pl.when(pl.program_id(1) == 0)
        def _():
            o_ref[...] = jnp.zeros_like(o_ref)

        o_ref[...] += jnp.sum(a_ref[...].astype(F32), axis=0, keepdims=True)

    return pl.pallas_call(
        body, name=name, grid=(n // tn, m // tm),
        in_specs=[pl.BlockSpec((tm, tn), lambda j, i: (i, j))], out_specs=pl.BlockSpec((1, tn), lambda j, i: (0, j)),
        out_shape=jax.ShapeDtypeStruct((1, n), F32),
        compiler_params=_params(("parallel", "arbitrary")),
    )(a)[0]


ANY = pl.BlockSpec(memory_space=pl.ANY)


def _mesh_pos():
    return lax.axis_index("x"), lax.axis_index("y"), lax.axis_index("c")


def _other_chips(px, py):
    return [(1 - px, py), (px, 1 - py), (1 - px, 1 - py)]


def _all_gather(xs, name):
    n = len(xs)

    def body(*refs):
        x_refs, out_refs = refs[:n], refs[n:2 * n]
        send_sems, recv_sems, local_sems = refs[2 * n:]
        px, py, pc = _mesh_pos()
        me, sibling = (px, py, pc), (px, py, 1 - pc)
        chips = _other_chips(px, py)

        def slot(a, qx, qy, qc):
            return out_refs[a].at[4 * qx + 2 * qy + qc]

        def copy(a, k, block, to, src=None):
            return pltpu.make_async_remote_copy(
                src_ref=slot(a, *block) if src is None else src, dst_ref=slot(a, *block),
                send_sem=send_sems.at[a, k], recv_sem=recv_sems.at[a, k], device_id=to, device_id_type=MESH)

        mine = [pltpu.make_async_copy(x_refs[a], slot(a, *me), local_sems.at[a]) for a in range(n)]
        for cp in mine:
            cp.start()
        first = []
        for a in range(n):
            first.append(copy(a, 0, me, sibling, src=x_refs[a]))
            first += [copy(a, 1 + j, me, (*chip, pc), src=x_refs[a]) for j, chip in enumerate(chips)]
        for cp in first:
            cp.start()
        passed = []
        for j, chip in enumerate(chips):
            for a in range(n):
                copy(a, 1 + j, (*chip, pc), me).wait_recv()
                passed.append(copy(a, 4 + j, (*chip, pc), sibling))
                passed[-1].start()
        for a in range(n):
            copy(a, 0, sibling, me).wait_recv()
            for j, chip in enumerate(chips):
                copy(a, 4 + j, (*chip, 1 - pc), me).wait_recv()
        for cp in first + passed:
            cp.wait_send()
        for cp in mine:
            cp.wait()

    return pl.pallas_call(
        body, name=name, out_shape=[jax.ShapeDtypeStruct((N_DEV,) + x.shape, x.dtype) for x in xs],
        in_specs=[ANY] * n, out_specs=[ANY] * n,
        scratch_shapes=[pltpu.SemaphoreType.DMA((n, 7)), pltpu.SemaphoreType.DMA((n, 7)), pltpu.SemaphoreType.DMA((n,))],
    )(*xs)


def _pair_exchange(gs, name):
    n = len(gs)

    def body(*refs):
        g_refs, r_refs = refs[:n], refs[n:2 * n]
        send_sems, recv_sems = refs[2 * n:]
        px, py, pc = _mesh_pos()
        copies = [pltpu.make_async_remote_copy(
            src_ref=g_refs[a].at[2 * j + (1 - pc)], dst_ref=r_refs[a].at[j], send_sem=send_sems.at[a, j],
            recv_sem=recv_sems.at[a, j], device_id=(px, py, 1 - pc), device_id_type=MESH)
            for a in range(n) for j in range(4)]
        for cp in copies:
            cp.start()
        for cp in copies:
            cp.wait()

    return pl.pallas_call(
        body, name=name, out_shape=[jax.ShapeDtypeStruct((4,) + g.shape[1:], g.dtype) for g in gs],
        in_specs=[ANY] * n, out_specs=[ANY] * n,
        scratch_shapes=[pltpu.SemaphoreType.DMA((n, 4)), pltpu.SemaphoreType.DMA((n, 4))],
    )(*gs)


def _chip_exchange(hs, name):
    n = len(hs)

    def body(*refs):
        h_refs, r_refs = refs[:n], refs[n:2 * n]
        send_sems, recv_sems = refs[2 * n:]
        px, py, pc = _mesh_pos()
        copies = [pltpu.make_async_remote_copy(
            src_ref=h_refs[a].at[2 * cx + cy], dst_ref=r_refs[a].at[k], send_sem=send_sems.at[a, k],
            recv_sem=recv_sems.at[a, k], device_id=(cx, cy, pc), device_id_type=MESH)
            for a in range(n) for k, (cx, cy) in enumerate(_other_chips(px, py))]
        for cp in copies:
            cp.start()
        for cp in copies:
            cp.wait()

    return pl.pallas_call(
        body, name=name, out_shape=[jax.ShapeDtypeStruct((3,) + h.shape[1:], h.dtype) for h in hs],
        in_specs=[ANY] * n, out_specs=[ANY] * n,
        scratch_shapes=[pltpu.SemaphoreType.DMA((n, 3)), pltpu.SemaphoreType.DMA((n, 3))],
    )(*hs)


SEM = pl.BlockSpec(memory_space=pltpu.SEMAPHORE)
HBM = pl.BlockSpec(memory_space=pltpu.HBM)
N_PEERS = N_DEV - 1


def _peer(px, py, pc, r):
    return ((1 - px) if r & 4 else px, (1 - py) if r & 2 else py, (1 - pc) if r & 1 else pc)


def _spread_copies(x_refs, land_refs, send_sems, recv_sems, mine):
    px, py, pc = _mesh_pos()
    out = []
    for a, (x_ref, land_ref) in enumerate(zip(x_refs, land_refs)):
        for r in range(1, N_DEV):
            qx, qy, qc = _peer(px, py, pc, r)
            slot = 4 * px + 2 * py + pc if mine else 4 * qx + 2 * qy + qc
            out.append(pltpu.make_async_remote_copy(
                src_ref=x_ref, dst_ref=land_ref.at[slot], send_sem=send_sems.at[a * N_PEERS + r - 1],
                recv_sem=recv_sems.at[a * N_PEERS + r - 1],
                device_id=(qx, qy, qc), device_id_type=MESH))
    return out


def _spread_start(xs, name, collective_id):
    n = len(xs)

    def body(*refs):
        x_refs, land_refs, send_sems, recv_sems, token = refs[:n], refs[n:2 * n], refs[2 * n], refs[2 * n + 1], refs[-1]
        px, py, pc = _mesh_pos()
        barrier = pltpu.get_barrier_semaphore()
        for r in range(1, N_DEV):
            pl.semaphore_signal(barrier, inc=1, device_id=_peer(px, py, pc, r), device_id_type=MESH)
        pl.semaphore_wait(barrier, N_PEERS)
        for cp in _spread_copies(x_refs, land_refs, send_sems, recv_sems, True):
            cp.start()
        token[...] = jnp.zeros_like(token)

    hbm = lambda t: pltpu.with_memory_space_constraint(t, pltpu.HBM)
    lands = [lax.empty((N_DEV,) + x.shape, x.dtype) for x in xs]
    sems = pltpu.SemaphoreType.DMA((n * N_PEERS,))
    outs = pl.pallas_call(
        body, name=name,
        out_shape=(sems, sems, *[pltpu.HBM(t.shape, t.dtype) for t in list(xs) + lands],
                   jax.ShapeDtypeStruct((SUBLANES, LANES), F32)),
        in_specs=[HBM] * (2 * n), out_specs=(SEM, SEM, *[HBM] * (2 * n), pl.BlockSpec(memory_space=pltpu.VMEM)),
        input_output_aliases={i: 2 + i for i in range(2 * n)},
        compiler_params=pltpu.CompilerParams(has_side_effects=pltpu.SideEffectType.DATAFLOW_SIDE_EFFECTING,
                                             collective_id=collective_id),
    )(*[hbm(t) for t in list(xs) + lands])
    return outs[0], outs[1], outs[2:2 + n], outs[2 + n:2 + 2 * n], outs[-1]


def _spread_wait(send_sems, recv_sems, xs, lands, after, name):
    n = len(xs)

    def body(*refs):
        for cp in _spread_copies(refs[:n], refs[n:2 * n], refs[2 * n], refs[2 * n + 1], False):
            cp.wait_send()
            cp.wait_recv()

    outs = pl.pallas_call(
        body, name=name, out_shape=tuple(pltpu.HBM(t.shape, t.dtype) for t in list(xs) + list(lands)),
        in_specs=[HBM] * (2 * n) + [SEM, SEM, ANY], out_specs=tuple([HBM] * (2 * n)),
        input_output_aliases={i: i for i in range(2 * n)},
        compiler_params=pltpu.CompilerParams(has_side_effects=pltpu.SideEffectType.DATAFLOW_SIDE_EFFECTING),
    )(*xs, *lands, send_sems, recv_sems, after)
    return outs[n:]


def _select_dot(blocks, target, valid):
    src = jnp.concatenate(blocks, axis=1)
    n = src.shape[1]
    pos = lax.broadcasted_iota(jnp.int32, (n, LANES), 0)
    sel = ((pos == target) & valid).astype(src.dtype)
    return lax.dot_general(src, sel, (((1,), (0,)), ((), ())), preferred_element_type=F32)


def _shards_to_pieces(g, pieces, shard_w, name):
    _, nl, r, pw = g.shape
    nb = pw // LANES
    c0s, nvs, blks = [], [], []
    for start, real, padded in pieces:
        for u in range(padded // LANES):
            c0, nv = start + LANES * u, max(0, min(LANES, real - LANES * u))
            cp_lo = c0 + (pw - shard_w) * min(c0 // shard_w, N_DEV - 1)
            cp_hi = c0 + nv - 1 + (pw - shard_w) * min((c0 + nv - 1) // shard_w, N_DEV - 1)
            assert nv == 0 or cp_hi - (cp_lo // LANES) * LANES < 3 * LANES
            c0s.append(c0)
            nvs.append(nv)
            blks.append(cp_lo // LANES if nv else 0)
    table = jnp.array([c0s, nvs, blks], jnp.int32)
    tr = min(r, 2048)
    last = N_DEV * nb - 1

    def body(tab_ref, b0, b1, b2, o_ref):
        t = pl.program_id(2)
        c0, nv, blk0 = tab_ref[0, t], tab_ref[1, t], tab_ref[2, t]
        j = lax.broadcasted_iota(jnp.int32, (1, LANES), 1)
        c = c0 + j
        k = jnp.zeros_like(c)
        for q in range(1, N_DEV):
            k = k + (c >= q * shard_w).astype(jnp.int32)
        target = c + (pw - shard_w) * k - blk0 * LANES
        o_ref[0] = _select_dot([b0[0, 0], b1[0, 0], b2[0, 0]], target, j < nv).astype(o_ref.dtype)

    def src(d):
        def index(l, i, t, tab_ref):
            blk = jnp.minimum(tab_ref[2, t] + d, last)
            return (blk // nb, l, i, blk % nb)
        return pl.BlockSpec((1, 1, tr, LANES), index)

    return pl.pallas_call(
        body, name=name, out_shape=jax.ShapeDtypeStruct((nl, r, LANES * len(c0s)), g.dtype),
        grid_spec=pltpu.PrefetchScalarGridSpec(
            num_scalar_prefetch=1, grid=(nl, r // tr, len(c0s)), in_specs=[src(0), src(1), src(2)],
            out_specs=pl.BlockSpec((1, tr, LANES), lambda l, i, t, tab_ref: (l, i, t))),
        compiler_params=_params(("parallel", "parallel", "parallel")),
    )(table, g, g, g)


def _pieces_to_shards(d, segs, far_start, shard_w, pw, name):
    nl, r, s_cols = d.shape
    nb = pw // LANES
    shifts = [src - gs for gs, src in segs]

    def source(c):
        return c + [sh for (gs, _), sh in zip(segs, shifts) if c >= gs][-1]

    nvs, blks = [], []
    for k in range(N_DEV):
        for b in range(nb):
            c0, nv = k * shard_w + LANES * b, max(0, min(LANES, shard_w - LANES * b))
            near = [source(c) for c in range(c0, c0 + nv)]
            near = [s for s in near if far_start is None or s < far_start]
            blk0 = min(near) // LANES if near else 0
            assert not near or max(near) - blk0 * LANES < 3 * LANES
            nvs.append(nv)
            blks.append(blk0)
    table = jnp.array([nvs, blks], jnp.int32)
    tr = min(r, 2048)
    last = s_cols // LANES - 1
    far_blk = 0 if far_start is None else far_start // LANES

    def body(tab_ref, b0, b1, b2, bf, o_ref):
        t = pl.program_id(2)
        nv, blk0 = tab_ref[0, t], tab_ref[1, t]
        j = lax.broadcasted_iota(jnp.int32, (1, LANES), 1)
        c = (t // nb) * shard_w + (t % nb) * LANES + j
        s = c + shifts[0]
        for (gs, _), prev, sh in zip(segs[1:], shifts[:-1], shifts[1:]):
            s = s + (sh - prev) * (c >= gs).astype(jnp.int32)
        target = s - blk0 * LANES
        if far_start is not None:
            target = jnp.where(s >= far_start, 3 * LANES + s - far_start, target)
        o_ref[0, 0] = _select_dot([b0[0], b1[0], b2[0], bf[0]], target, j < nv).astype(o_ref.dtype)

    def src(dd):
        return pl.BlockSpec((1, tr, LANES), lambda l, i, t, tab_ref: (l, i, jnp.minimum(tab_ref[1, t] + dd, last)))

    return pl.pallas_call(
        body, name=name, out_shape=jax.ShapeDtypeStruct((N_DEV, nl, r, pw), d.dtype),
        grid_spec=pltpu.PrefetchScalarGridSpec(
            num_scalar_prefetch=1, grid=(nl, r // tr, N_DEV * nb),
            in_specs=[src(0), src(1), src(2), pl.BlockSpec((1, tr, LANES), lambda l, i, t, tab_ref: (l, i, far_blk))],
            out_specs=pl.BlockSpec((1, 1, tr, LANES), lambda l, i, t, tab_ref: (t // nb, l, i, t % nb))),
        compiler_params=_params(("parallel", "parallel", "parallel")),
    )(table, d, d, d, d)


def _cat_blocks(g, axis):
    _, nl, r, c = g.shape
    shape = (nl, N_DEV * r, c) if axis == 0 else (nl, r, N_DEV * c)

    def body(g_ref, o_ref):
        o_ref[0] = g_ref[0, 0]

    return pl.pallas_call(
        body, name="cat_blocks", grid=(nl, N_DEV), out_shape=jax.ShapeDtypeStruct(shape, g.dtype),
        in_specs=[pl.BlockSpec((1, 1, r, c), lambda l, k: (k, l, 0, 0))],
        out_specs=pl.BlockSpec((1, r, c), (lambda l, k: (l, k, 0)) if axis == 0 else (lambda l, k: (l, 0, k))),
        compiler_params=_params(("parallel", "parallel")),
    )(g)


def _split_blocks(d, axis):
    nl = d.shape[0]
    r, c = (d.shape[1] // N_DEV, d.shape[2]) if axis == 0 else (d.shape[1], d.shape[2] // N_DEV)

    def body(d_ref, o_ref):
        o_ref[0, 0] = d_ref[0]

    return pl.pallas_call(
        body, name="split_blocks", grid=(nl, N_DEV), out_shape=jax.ShapeDtypeStruct((N_DEV, nl, r, c), d.dtype),
        in_specs=[pl.BlockSpec((1, r, c), (lambda l, k: (l, k, 0)) if axis == 0 else (lambda l, k: (l, 0, k)))],
        out_specs=pl.BlockSpec((1, 1, r, c), lambda l, k: (k, l, 0, 0)),
        compiler_params=_params(("parallel", "parallel")),
    )(d)


def _row_tile(r, mult, pref=256):
    return max(d for d in range(mult, pref + 1, mult) if r % d == 0)


def _pair_sum(g, r1):
    _, r, c = g.shape
    tr = _row_tile(r, 16)

    def body(g_ref, r_ref, h_ref, own_ref):
        j = pl.program_id(1)
        px, py, pc = _mesh_pos()
        mine = jnp.where(pc == 0, g_ref[0, 0].astype(F32), g_ref[0, 1].astype(F32))
        val = (mine + r_ref[0].astype(F32)).astype(h_ref.dtype)
        h_ref[0] = val

        @pl.when(j == 2 * px + py)
        def _():
            own_ref[...] = val

    return pl.pallas_call(
        body, name="rs_pair_sum", grid=(r // tr, 4),
        in_specs=[pl.BlockSpec((1, 2, tr, c), lambda i, j: (j, 0, i, 0)), pl.BlockSpec((1, tr, c), lambda i, j: (j, i, 0))],
        out_specs=[pl.BlockSpec((1, tr, c), lambda i, j: (j, i, 0)), pl.BlockSpec((tr, c), lambda i, j: (i, 0))],
        out_shape=[jax.ShapeDtypeStruct((4, r, c), g.dtype), jax.ShapeDtypeStruct((r, c), g.dtype)],
        compiler_params=_params(("parallel", "arbitrary")),
    )(g.reshape(4, 2, r, c), r1)


def _adamw(parts, w, m, v, name, tr):
    r, c = w.shape
    n = len(parts)
    slots = [slot for _, slot in parts]

    def body(*refs):
        w_ref, m_ref, v_ref = refs[n:n + 3]
        g_ref, d_ref, nm_ref, nv_ref = refs[n + 3:]
        g = None
        for slot, ref in zip(slots, refs[:n]):
            t = (ref[...] if slot is None else ref[0]).astype(F32)
            g = t if g is None else g + t
        nm = ADAM_B1 * m_ref[...] + (1.0 - ADAM_B1) * g
        nv = ADAM_B2 * v_ref[...] + (1.0 - ADAM_B2) * (g * g)
        m_hat = nm / (1.0 - ADAM_B1 ** ADAM_STEP)
        v_hat = nv / (1.0 - ADAM_B2 ** ADAM_STEP)
        g_ref[...] = g
        d_ref[...] = -ADAM_LR * (m_hat / (jnp.sqrt(v_hat) + ADAM_EPS) + ADAM_WD * w_ref[...])
        nm_ref[...] = nm
        nv_ref[...] = nv

    flat = pl.BlockSpec((tr, c), lambda i: (i, 0))
    specs = [flat if slot is None else pl.BlockSpec((1, tr, c), functools.partial(lambda i, s: (s, i, 0), s=slot))
             for _, slot in parts]
    out = jax.ShapeDtypeStruct((r, c), F32)
    return pl.pallas_call(
        body, name=name, grid=(r // tr,), in_specs=specs + [flat] * 3, out_specs=[flat] * 4, out_shape=[out] * 4,
        compiler_params=_params(("parallel",)),
    )(*[a for a, _ in parts], w, m, v)


def _adamw_whole(parts, w, m, v, name):
    n_parts = parts.shape[0]
    shape = w.shape[1:]
    zeros = (0,) * len(shape)

    def body(p_ref, w_ref, m_ref, v_ref, g_ref, d_ref, nm_ref, nv_ref):
        g = p_ref[0]
        for k in range(1, n_parts):
            g = g + p_ref[k]
        nm = ADAM_B1 * m_ref[...] + (1.0 - ADAM_B1) * g
        nv = ADAM_B2 * v_ref[...] + (1.0 - ADAM_B2) * (g * g)
        m_hat = nm / (1.0 - ADAM_B1 ** ADAM_STEP)
        v_hat = nv / (1.0 - ADAM_B2 ** ADAM_STEP)
        g_ref[...] = g
        d_ref[...] = -ADAM_LR * (m_hat / (jnp.sqrt(v_hat) + ADAM_EPS) + ADAM_WD * w_ref[...])
        nm_ref[...] = nm
        nv_ref[...] = nv

    one = pl.BlockSpec((1,) + shape, lambda l: (l,) + zeros)
    out = jax.ShapeDtypeStruct(w.shape, F32)
    return pl.pallas_call(
        body, name=name, grid=(w.shape[0],),
        in_specs=[pl.BlockSpec((n_parts, 1) + shape, lambda l: (0, l) + zeros), one, one, one],
        out_specs=[one] * 4, out_shape=[out] * 4, compiler_params=_params(("parallel",)),
    )(parts, w, m, v)


def _sum_parts(parts, name):
    n_parts = parts.shape[0]
    shape = parts.shape[2:]
    zeros = (0,) * len(shape)

    def body(p_ref, o_ref):
        g = p_ref[0]
        for k in range(1, n_parts):
            g = g + p_ref[k]
        o_ref[...] = g

    return pl.pallas_call(
        body, name=name, grid=(parts.shape[1],),
        in_specs=[pl.BlockSpec((n_parts, 1) + shape, lambda l: (0, l) + zeros)],
        out_specs=pl.BlockSpec((1,) + shape, lambda l: (l,) + zeros),
        out_shape=jax.ShapeDtypeStruct(parts.shape[1:], F32), compiler_params=_params(("parallel",)),
    )(parts)


SMALL = ("b_in", "sgu_ln_g", "sgu_ln_b", "sgu_w", "sgu_b", "gdn_a_log", "gdn_dt_bias", "gdn_norm_g", "ln1_g", "ln1_b",
         "ffn_conv_b", "ln2_g", "ln2_b")
WEIGHT_ORDER = ("w_in", "b_in", "sgu_ln_g", "sgu_ln_b", "sgu_w", "sgu_b", "gdn_conv_w", "gdn_a_log", "gdn_dt_bias",
                "gdn_norm_g", "w_proj_a", "w_proj_b", "w_proj_c", "w_out", "ln1_g", "ln1_b", "ffn_w_up", "ffn_conv_w",
                "ffn_conv_b", "ffn_w_down", "ln2_g", "ln2_b")
IN_SHARD = N_IN // N_DEV
IN_SHARD_PAD = -(-IN_SHARD // LANES) * LANES
UP_SHARD = 2 * D_FF // N_DEV
UP_SHARD_PAD = -(-UP_SHARD // LANES) * LANES

O_FOX, O_FF, O_SGU, O_GDN, O_A, O_B, O_GATE, O_GATES = 0, 3072, 3080, 5128, 8200, 8208, 8216, 9240
IN_PIECES = (("fq", O_FOX, WIDTH, BF16), ("fk", O_FOX + WIDTH, WIDTH, BF16), ("fv", O_FOX + 2 * WIDTH, WIDTH, BF16),
             ("su", O_SGU, WIDTH, F32), ("sv", O_SGU + WIDTH, WIDTH, F32),
             ("gq", O_GDN, WIDTH, F32), ("gk", O_GDN + WIDTH, WIDTH, F32), ("gv", O_GDN + 2 * WIDTH, WIDTH, F32),
             ("gg", O_GATE, WIDTH, F32),
             ("ga", O_GATES, D_MODEL, F32), ("gb", O_GATES + D_MODEL, D_MODEL, F32), ("gc", O_GATES + 2 * D_MODEL, D_MODEL, F32))
N_SMALL_COLS = 3 * N_HEADS


def _split_in(w):
    out = {n: w[..., o:o + k] for n, o, k, _ in IN_PIECES}
    sm = jnp.concatenate([w[..., O_FF:O_FF + N_HEADS], w[..., O_A:O_A + 2 * N_HEADS]], axis=-1)
    out["sm"] = jnp.pad(sm, [(0, 0)] * (w.ndim - 1) + [(0, LANES - N_SMALL_COLS)])
    return out


def _pad_to(a, axis, n):
    pad = [(0, 0)] * a.ndim
    pad[axis] = (0, n - a.shape[axis])
    return jnp.pad(a, pad)


IN_MAIN = ((O_FOX, 3 * WIDTH, 3 * WIDTH), (O_SGU, O_A - O_SGU, O_A - O_SGU), (O_GATE, N_IN - O_GATE, N_IN - O_GATE))
N_MAIN = sum(p[2] for p in IN_MAIN)
N_RE = N_MAIN + LANES
IN_OFF = {"fq": 0, "fk": WIDTH, "fv": 2 * WIDTH, "su": 3 * WIDTH, "sv": 4 * WIDTH, "gq": 5 * WIDTH, "gk": 6 * WIDTH,
          "gv": 7 * WIDTH, "gg": 8 * WIDTH, "ga": 9 * WIDTH, "gb": 9 * WIDTH + D_MODEL, "gc": 9 * WIDTH + 2 * D_MODEL,
          "sm": N_MAIN}
IN_SEGS = ((0, 0), (O_FF, N_MAIN), (O_SGU, 3 * WIDTH), (O_A, N_MAIN + N_HEADS), (O_GATE, 8 * WIDTH))
UP_PIECES = ((0, D_FF, D_FF_PAD), (D_FF, D_FF, D_FF_PAD))
UP_SEGS = ((0, 0), (D_FF, D_FF_PAD))


def _shard_cols(g, start, n, shard_w):
    k, r = start // shard_w, start % shard_w
    assert r + n <= shard_w
    return g[k, :, :, r:r + n]


def _reorder_bias(v):
    return jnp.concatenate([v[:3 * WIDTH], v[N_MAIN:N_MAIN + N_HEADS], v[3 * WIDTH:8 * WIDTH],
                            v[N_MAIN + N_HEADS:N_MAIN + N_SMALL_COLS], v[8 * WIDTH:N_MAIN]])


def _layer_weights(wt, l, conv, small):
    w = {"re": wt["in"][l], "b_in": _split_in(small["b_in"][None, :])}
    w["pa"], w["pb"], w["pc"], w["out"] = wt["pa"][l], wt["pb"][l], wt["pc"][l], wt["out"][l]
    w["up"], w["down"] = wt["up"][l], wt["down"][l]
    gcw, fcw, fcb = conv["gdn_conv_w"], conv["ffn_conv_w"], small["ffn_conv_b"][None, :]
    w["gcq"], w["gck"], w["gcv"] = gcw[:, :WIDTH], gcw[:, WIDTH:2 * WIDTH], gcw[:, 2 * WIDTH:]
    w["fcg"], w["fcv"] = _pad_to(fcw[:, :D_FF], 1, D_FF_PAD), _pad_to(fcw[:, D_FF:], 1, D_FF_PAD)
    w["fbg"], w["fbv"] = _pad_to(fcb[:, :D_FF], 1, D_FF_PAD), _pad_to(fcb[:, D_FF:], 1, D_FF_PAD)
    for n in ("sgu_ln_g", "sgu_ln_b", "sgu_w", "sgu_b", "gdn_a_log", "gdn_dt_bias", "gdn_norm_g", "ln1_g", "ln1_b",
              "ln2_g", "ln2_b"):
        w[n] = small[n]
    return w


def _layer_fwd(x, xb, w, tag):
    s = {"x": x, "xb": xb}
    for n, _, k, dt in IN_PIECES + (("sm", 0, LANES, F32),):
        s[n] = _mm(xb, w["re"], mode="nn", name="in_" + n, bias=w["b_in"][n][0], out_dtype=dt, b_off=IN_OFF[n], b_len=k)
    s["c_col"] = _fox_prefix_fwd(s["sm"])
    s["c_row"] = _row_form(s["c_col"], ATT_T)
    s["ya"], s["ya32"], s["lse"] = _fox_fwd(s["fq"], s["fk"], s["fv"], s["c_col"], s["c_row"])
    s["yb"] = _sgu_fwd(s["su"], s["sv"], w["sgu_ln_g"], w["sgu_ln_b"], w["sgu_w"], w["sgu_b"])
    s["yc"], s["states"] = _gdn_fwd(s["gq"], s["gk"], s["gv"], s["sm"], s["gg"], w["gcq"], w["gck"], w["gcv"],
                                    w["gdn_a_log"], w["gdn_dt_bias"], w["gdn_norm_g"])
    s["merged"], s["za"], s["zb"], s["zc"] = _merge_fwd(
        [s["ya"], s["yb"], s["yc"]], [w["pa"], w["pb"], w["pc"]], [s["ga"], s["gb"], s["gc"]])
    s["mix"] = _mm(s["merged"], w["out"], mode="nn", name="out_proj")
    s["x1"], s["x1b"] = _ln_fwd(x, s["mix"], w["ln1_g"], w["ln1_b"], name="ln_fwd")
    s["hug"] = _mm(s["x1b"], w["up"], mode="nn", name="ffn_up_gate", b_off=0, b_len=D_FF_PAD)
    s["huv"] = _mm(s["x1b"], w["up"], mode="nn", name="ffn_up_val", b_off=D_FF_PAD, b_len=D_FF_PAD)
    s["act"] = _ffn_act_fwd(s["hug"], s["huv"], w["fcg"], w["fcv"], w["fbg"], w["fbv"])
    s["ffn"] = _mm(s["act"], w["down"], mode="nn", name="ffn_down")
    x2, x2b = _ln_fwd(s["x1"], s["ffn"], w["ln2_g"], w["ln2_b"], name="ln_fwd")
    return x2, x2b, s


def _layer_bwd(dx2, s, w):
    gb, gc, gs = {}, {}, {}
    ds2, ds2b, dg2, db2 = _ln_bwd(s["x1"], s["ffn"], w["ln2_g"], dx2, name="ln_bwd")
    gs["ln2_g"], gs["ln2_b"] = dg2[0], db2[0]
    gb["ffn_w_down"] = _mm(s["act"], ds2b, mode="tn", name="dw_down", out_dtype=BF16)[:D_FF]
    dact = _mm(ds2b, w["down"], mode="nt", name="dact")
    dhug, dhuv, dcwg, dcwv, dcbg, dcbv = _ffn_act_bwd(s["hug"], s["huv"], w["fcg"], w["fcv"], w["fbg"], w["fbv"], dact)
    gc["ffn_conv_w"] = jnp.concatenate([dcwg[:, :D_FF], dcwv[:, :D_FF]], axis=1)
    gs["ffn_conv_b"] = jnp.concatenate([dcbg[0, :D_FF], dcbv[0, :D_FF]])
    dhu = jnp.concatenate([dhug, dhuv], axis=1)
    gb["ffn_w_up"] = _mm(s["x1b"], dhu, mode="tn", name="dw_up", out_dtype=BF16)
    dx1 = _mm(dhu, w["up"], mode="nt", name="dx_up", add=ds2, add_scale=DEEPNORM_ALPHA)
    ds1, ds1b, dg1, db1 = _ln_bwd(s["x"], s["mix"], w["ln1_g"], dx1, name="ln_bwd")
    gs["ln1_g"], gs["ln1_b"] = dg1[0], db1[0]
    gb["w_out"] = _mm(s["merged"], ds1b, mode="tn", name="dw_out", out_dtype=BF16)
    dmerged = _mm(ds1b, w["out"], mode="nt", name="dmerged")
    dza, dzb, dzc, dga, dgb, dgc = _merge_bwd(dmerged, [s["za"], s["zb"], s["zc"]], [s["ga"], s["gb"], s["gc"]])
    gb["w_proj_a"] = _mm(s["ya"], dza, mode="tn", name="dw_proj", out_dtype=BF16)
    gb["w_proj_b"] = _mm(s["yb"], dzb, mode="tn", name="dw_proj", out_dtype=BF16)
    gb["w_proj_c"] = _mm(s["yc"], dzc, mode="tn", name="dw_proj", out_dtype=BF16)
    dya = _mm(dza, w["pa"], mode="nt", name="dy_proj_bf16", out_dtype=BF16)
    dyb = _mm(dzb, w["pb"], mode="nt", name="dy_proj")
    dyc = _mm(dzc, w["pc"], mode="nt", name="dy_proj")
    d = {"ga": dga, "gb": dgb, "gc": dgc}
    fox = (s["fq"], s["fk"], s["fv"], s["ya32"], dya, s["lse"], s["c_col"], s["c_row"])
    d["fk"], d["fv"], dc_row = _fox_bwd_kv(*fox)
    d["fq"], dc_q = _fox_bwd_q(*fox)
    dc_col = _col_form(dc_row, x_rows(s)) + jnp.pad(dc_q[:, :, 0].T, ((0, 0), (0, LANES - N_HEADS)))
    dsm = _fox_prefix_bwd(s["sm"], dc_col)
    d["su"], d["sv"], dlg, dlb, dsw, dsb = _sgu_bwd(s["su"], s["sv"], w["sgu_ln_g"], w["sgu_ln_b"], w["sgu_w"], w["sgu_b"], dyb)
    gs["sgu_ln_g"], gs["sgu_ln_b"], gs["sgu_w"], gs["sgu_b"] = dlg.reshape(-1), dlb.reshape(-1), dsw, dsb[:, :, 0]
    (d["gq"], d["gk"], d["gv"], d["gg"], dab, dwq, dwk, dwv, dal, ddt, dng) = _gdn_bwd(
        s["gq"], s["gk"], s["gv"], s["sm"], s["gg"], w["gcq"], w["gck"], w["gcv"],
        w["gdn_a_log"], w["gdn_dt_bias"], w["gdn_norm_g"], s["states"], dyc)
    gc["gdn_conv_w"] = jnp.concatenate([dwq, dwk, dwv], axis=1)
    gs["gdn_a_log"], gs["gdn_dt_bias"], gs["gdn_norm_g"] = dal[:, 0, 0], ddt[:, 0, 0], dng[0]
    dab_cols = jnp.concatenate([dab[:, :, 0].T, dab[:, :, 1].T], axis=1)
    d["sm"] = (dsm + jnp.pad(dab_cols, ((0, 0), (N_HEADS, LANES - N_SMALL_COLS)))).astype(BF16)
    dp = jnp.concatenate([d[n] for n in sorted(IN_OFF, key=IN_OFF.get)], axis=1)
    gb["w_in"] = _mm(s["xb"], dp, mode="tn", name="dw_in", out_dtype=BF16)
    gs["b_in"] = _reorder_bias(_colsum(dp, "db_in"))
    dx = _mm(dp, w["re"], mode="nt", name="dx_in", add=ds1, add_scale=DEEPNORM_ALPHA)
    return dx, gb, gc, gs


def x_rows(s):
    return s["x"].shape[0]


def _step(a):
    x = a["x"][0]
    kinds = ("grad_", "delta_", "new_m_", "new_v_")
    res = {}
    bf = lambda n: a[n].astype(BF16)
    pad_in = lambda t: _pad_to(t, 2, IN_SHARD_PAD)
    pad_up = lambda t: _pad_to(t, 2, UP_SHARD_PAD)
    shards = [pad_in(bf("w_in")), pad_up(bf("ffn_w_up")), bf("w_proj_a"), bf("w_proj_b"), bf("w_proj_c"), bf("w_out"),
              bf("ffn_w_down"), a["gdn_conv_w"], a["ffn_conv_w"]]

    def layer_weights(g, n_layers, first, after=0.0):
        g_in, g_up, g_pa, g_pb, g_pc, g_out, g_down, g_gc, g_fc = g
        sm = jnp.concatenate([_shard_cols(g_in, O_FF, N_HEADS, IN_SHARD), _shard_cols(g_in, O_A, 2 * N_HEADS, IN_SHARD)], axis=-1)
        wt = {"in": jnp.concatenate([_shards_to_pieces(g_in, IN_MAIN, IN_SHARD, "in_to_pieces"), _pad_to(sm, 2, LANES)], axis=-1),
              "up": _shards_to_pieces(g_up, UP_PIECES, UP_SHARD, "up_to_pieces"),
              "pa": _cat_blocks(g_pa, 1), "pb": _cat_blocks(g_pb, 1), "pc": _cat_blocks(g_pc, 1),
              "out": _cat_blocks(g_out, 0), "down": _pad_to(_cat_blocks(g_down, 0), 1, D_FF_PAD)}
        conv_full = {"gdn_conv_w": g_gc.transpose(1, 2, 0, 3).reshape(n_layers, GDN_CONV, 3 * WIDTH),
                     "ffn_conv_w": g_fc.transpose(1, 2, 0, 3).reshape(n_layers, FFN_CONV, 2 * D_FF)}
        small = lambda l: {n: a[n][first + l] + after if n == "b_in" else a[n][first + l] for n in SMALL}
        return [_layer_weights(wt, l, {n: t[l] for n, t in conv_full.items()}, small(l)) for l in range(n_layers)]

    g_first = _all_gather([t[:1] for t in shards], "gather_weights")
    groups = [(l, l + 1) for l in range(1, DEPTH)]

    def start_group(i, after):
        lo, hi = groups[i]
        part = [t[lo:hi] for t in shards]
        part[-1], _ = lax.optimization_barrier((part[-1], after))
        return _spread_start(part, "gather_start_%d" % i, 7 + i)

    flying = start_group(0, g_first[-1]) if groups else None
    layers = layer_weights(g_first, 1, 0, flying[-1][0, 0] if groups else 0.0)

    xb = x.astype(BF16)
    saved = []
    me = 4 * lax.axis_index("x") + 2 * lax.axis_index("y") + lax.axis_index("c")
    for l in range(DEPTH):
        for i, (lo, hi) in enumerate(groups):
            if l == lo:
                send_sems, recv_sems, part, lands, _ = flying
                lands = _spread_wait(send_sems, recv_sems, part, lands, x, "gather_wait_%d" % i)
                lands = [lax.dynamic_update_index_in_dim(land, t[None], me, 0) for land, t in zip(lands, part)]
                flying = start_group(i + 1, lands[-1]) if i + 1 < len(groups) else None
                layers += layer_weights(lands, hi - lo, lo, flying[-1][0, 0] if flying else 0.0)
        x, xb, s = _layer_fwd(x, xb, layers[l], l)
        saved.append(s)
    dx, loss = _loss_head(x, a["loss_target"][0])
    res["loss"] = lax.psum(loss, ("x", "y", "c"))

    grads = [None] * DEPTH
    for l in reversed(range(DEPTH)):
        dx, gb, gc, gs = _layer_bwd(dx, saved[l], layers[l])
        grads[l] = {**gb, **gc, **gs}
    res["grad_x"] = dx[None]
    stacked = {n: jnp.stack([g[n] for g in grads]) for n in grads[0]}

    big = (("w_in", _pieces_to_shards(stacked["w_in"], IN_SEGS, N_MAIN, IN_SHARD, IN_SHARD_PAD, "in_to_shards"), pad_in),
           ("ffn_w_up", _pieces_to_shards(stacked["ffn_w_up"], UP_SEGS, None, UP_SHARD, UP_SHARD_PAD, "up_to_shards"), pad_up),
           ("w_proj_a", _split_blocks(stacked["w_proj_a"], 1), None), ("w_proj_b", _split_blocks(stacked["w_proj_b"], 1), None),
           ("w_proj_c", _split_blocks(stacked["w_proj_c"], 1), None), ("w_out", _split_blocks(stacked["w_out"], 0), None),
           ("ffn_w_down", _split_blocks(stacked["ffn_w_down"], 0), None))
    flat3 = lambda t: t.reshape(t.shape[0], -1, t.shape[-1])
    gs_ = [flat3(g) for _, g, _ in big]
    r1s = _pair_exchange(gs_, "rs_pair_exchange")
    sums = [_pair_sum(g, r1) for g, r1 in zip(gs_, r1s)]
    r2s = _chip_exchange([h for h, _ in sums], "rs_chip_exchange")
    for (n, _, pad), (_, own), r2 in zip(big, sums, r2s):
        prep = lambda t: (t if pad is None else pad(t)).reshape(own.shape)
        outs = _adamw([(own, None), (r2, 0), (r2, 1), (r2, 2)], prep(a[n]), prep(a["m_" + n]), prep(a["v_" + n]),
                      "adamw_" + n, _row_tile(own.shape[0], 16, max(16, (1 << 18) // own.shape[1])))
        for kind, o in zip(kinds, outs):
            o = o.reshape((DEPTH, -1, own.shape[1]))
            res[kind + n] = o if pad is None else o[:, :, :a[n].shape[2]]

    nd = lambda t: t.reshape(t.shape[0], 1, t.shape[1]) if t.ndim == 2 else t
    names = list(SMALL) + ["gdn_conv_w", "ffn_conv_w"]
    parts = dict(zip(names, _all_gather([nd(stacked[n]) for n in names], "gather_small_grads")))
    for n in SMALL:
        outs = _adamw_whole(parts[n], nd(a[n]), nd(a["m_" + n]), nd(a["v_" + n]), "adamw_" + n)
        for kind, o in zip(kinds, outs):
            res[kind + n] = o.reshape(a[n].shape)
    me = 4 * lax.axis_index("x") + 2 * lax.axis_index("y") + lax.axis_index("c")
    for n in ("gdn_conv_w", "ffn_conv_w"):
        width = a[n].shape[2]
        g_own = lax.dynamic_slice_in_dim(_sum_parts(parts[n], "sum_" + n), me * width, width, axis=2)
        outs = _adamw_whole(g_own[None], a[n], a["m_" + n], a["v_" + n], "adamw_" + n)
        for kind, o in zip(kinds, outs):
            res[kind + n] = o
    return res


INPUT_ORDER = (("x",) + WEIGHT_ORDER + ("loss_target",) + tuple("m_" + n for n in WEIGHT_ORDER)
               + tuple("v_" + n for n in WEIGHT_ORDER))
OUTPUT_ORDER = (("loss", "grad_x") + tuple(k + n for k in ("grad_", "delta_", "new_m_", "new_v_") for n in WEIGHT_ORDER))


def kernel(x, w_in, b_in, sgu_ln_g, sgu_ln_b, sgu_w, sgu_b, gdn_conv_w, gdn_a_log, gdn_dt_bias, gdn_norm_g, w_proj_a, w_proj_b, w_proj_c, w_out, ln1_g, ln1_b, ffn_w_up, ffn_conv_w, ffn_conv_b, ffn_w_down, ln2_g, ln2_b, loss_target, m_w_in, m_b_in, m_sgu_ln_g, m_sgu_ln_b, m_sgu_w, m_sgu_b, m_gdn_conv_w, m_gdn_a_log, m_gdn_dt_bias, m_gdn_norm_g, m_w_proj_a, m_w_proj_b, m_w_proj_c, m_w_out, m_ln1_g, m_ln1_b, m_ffn_w_up, m_ffn_conv_w, m_ffn_conv_b, m_ffn_w_down, m_ln2_g, m_ln2_b, v_w_in, v_b_in, v_sgu_ln_g, v_sgu_ln_b, v_sgu_w, v_sgu_b, v_gdn_conv_w, v_gdn_a_log, v_gdn_dt_bias, v_gdn_norm_g, v_w_proj_a, v_w_proj_b, v_w_proj_c, v_w_out, v_ln1_g, v_ln1_b, v_ffn_w_up, v_ffn_conv_w, v_ffn_conv_b, v_ffn_w_down, v_ln2_g, v_ln2_b):
    args = (x, w_in, b_in, sgu_ln_g, sgu_ln_b, sgu_w, sgu_b, gdn_conv_w, gdn_a_log, gdn_dt_bias, gdn_norm_g, w_proj_a, w_proj_b, w_proj_c, w_out, ln1_g, ln1_b, ffn_w_up, ffn_conv_w, ffn_conv_b, ffn_w_down, ln2_g, ln2_b, loss_target, m_w_in, m_b_in, m_sgu_ln_g, m_sgu_ln_b, m_sgu_w, m_sgu_b, m_gdn_conv_w, m_gdn_a_log, m_gdn_dt_bias, m_gdn_norm_g, m_w_proj_a, m_w_proj_b, m_w_proj_c, m_w_out, m_ln1_g, m_ln1_b, m_ffn_w_up, m_ffn_conv_w, m_ffn_conv_b, m_ffn_w_down, m_ln2_g, m_ln2_b, v_w_in, v_b_in, v_sgu_ln_g, v_sgu_ln_b, v_sgu_w, v_sgu_b, v_gdn_conv_w, v_gdn_a_log, v_gdn_dt_bias, v_gdn_norm_g, v_w_proj_a, v_w_proj_b, v_w_proj_c, v_w_out, v_ln1_g, v_ln1_b, v_ffn_w_up, v_ffn_conv_w, v_ffn_conv_b, v_ffn_w_down, v_ln2_g, v_ln2_b)
    res = _step(dict(zip(INPUT_ORDER, args)))
    return tuple(res[n] for n in OUTPUT_ORDER)
```
